```python
import math
import jax, jax.numpy as jnp
from jax import lax
import numpy as np

D_MODEL = 1024
BATCH = 8
SEQ = 4096
DEPTH = 1

GDN_HEADS = 8
GDN_DK = 64
GDN_DV = 64
GDN_CONV = 4
GDN_CHUNK = 64
NSA_HEADS = 8
NSA_KV_GROUPS = 2
NSA_HPG = NSA_HEADS // NSA_KV_GROUPS
NSA_DH = 64
NSA_BRANCHES = 3
CMP_BLOCK = 32
CMP_STRIDE = 16
SEL_BLOCK = 64
N_SELECT = 16
WINDOW = 512
NSA_QBLOCK = 64
FORCE_BONUS = 1e4
REL_BUCKETS = 32
REL_MAX_DIST = 128
PEER_HEADS = 8
PEER_NKEYS = 128
PEER_N_EXPERTS = PEER_NKEYS * PEER_NKEYS
PEER_TOPK = 16
PEER_DQ = 256
PEER_DQH = PEER_DQ // 2
PEER_TOKBLOCK = 128
GDN_QKV_W = 2 * GDN_HEADS * GDN_DK + GDN_HEADS * GDN_DV
GDN_Z_W = GDN_HEADS * GDN_DV
GDN_BETA_W = GDN_HEADS
GDN_DECAY_W = GDN_HEADS
NSA_Q_W = NSA_HEADS * NSA_DH
NSA_KV_W = NSA_BRANCHES * 2 * NSA_KV_GROUPS * NSA_DH
NSA_GATE_W = NSA_HEADS * NSA_BRANCHES
MERGE_W = 2 * D_MODEL
N_IN = GDN_QKV_W + GDN_Z_W + GDN_BETA_W + GDN_DECAY_W + NSA_Q_W + NSA_KV_W + NSA_GATE_W + MERGE_W
RMS_EPS = 1e-6
NEG = -1e30

kernel_name = 'hybrid_gdn_nsa_peer_block'


def rms_norm(x, w):
    xf = x.astype(jnp.float32)
    y = xf * lax.rsqrt(jnp.mean(xf * xf, axis=-1, keepdims=True) + RMS_EPS)
    return (y * w.astype(jnp.float32)).astype(x.dtype)


def l2_normalize(x):
    xf = x.astype(jnp.float32)
    return xf * lax.rsqrt(jnp.sum(xf * xf, axis=-1, keepdims=True) + 1e-6)


def t5_bucket(dist):
    dist = jnp.maximum(dist, 0)
    max_exact = REL_BUCKETS // 2
    d = jnp.maximum(dist, 1).astype(jnp.float32)
    large = max_exact + (jnp.log(d / max_exact) / math.log(REL_MAX_DIST / max_exact)
                         * (REL_BUCKETS - max_exact)).astype(jnp.int32)
    large = jnp.minimum(large, REL_BUCKETS - 1)
    return jnp.where(dist < max_exact, dist, large)


def masked_softmax(s, mask):
    s = jnp.where(mask, s.astype(jnp.float32), NEG)
    p = jax.nn.softmax(s, axis=-1)
    return jnp.where(mask, p, 0.0)


def causal_depthwise_conv(x, w):
    k = w.shape[0]
    return lax.conv_general_dilated(x, w[:, None, :].astype(x.dtype), (1,), ((k - 1, 0),),
                                    dimension_numbers=('NWC', 'WIO', 'NWC'),
                                    feature_group_count=x.shape[-1])


def gated_delta_rule_chunked(q, k, v, g, beta):
    b, h, s, dk = q.shape
    dv = v.shape[-1]
    c = GDN_CHUNK
    n = s // c
    q = (q * dk ** -0.5).reshape(b, h, n, c, dk)
    k = k.reshape(b, h, n, c, dk)
    v = v.reshape(b, h, n, c, dv)
    beta = beta.reshape(b, h, n, c)
    g = jnp.cumsum(g.reshape(b, h, n, c), axis=-1)
    lower = jnp.tril(jnp.ones((c, c), bool))
    strict = jnp.tril(jnp.ones((c, c), bool), -1)
    decay = jnp.where(lower, jnp.exp(jnp.where(lower, g[..., :, None] - g[..., None, :], 0.0)), 0.0)
    k_beta = k * beta[..., None]
    a_mat = jnp.where(strict, jnp.einsum('bhnid,bhnjd->bhnij', k_beta, k) * decay, 0.0)
    tri = a_mat + jnp.eye(c, dtype=a_mat.dtype)
    u = lax.linalg.triangular_solve(tri, v * beta[..., None], left_side=True, lower=True, unit_diagonal=True)
    w = lax.linalg.triangular_solve(tri, k_beta * jnp.exp(g)[..., None], left_side=True, lower=True,
                                    unit_diagonal=True)
    attn = jnp.einsum('bhnid,bhnjd->bhnij', q, k) * decay
    q_dec = q * jnp.exp(g)[..., None]
    k_dec = k * jnp.exp(g[..., -1:] - g)[..., None]
    g_last = jnp.exp(g[..., -1])

    def step(state, inp):
        u_i, w_i, q_i, a_i, k_i, gl_i = inp
        v_new = u_i - jnp.einsum('bhcd,bhde->bhce', w_i, state)
        o_i = jnp.einsum('bhcd,bhde->bhce', q_i, state) + jnp.einsum('bhij,bhje->bhie', a_i, v_new)
        state = state * gl_i[..., None, None] + jnp.einsum('bhcd,bhce->bhde', k_i, v_new)
        return state, o_i

    xs = (jnp.moveaxis(u, 2, 0), jnp.moveaxis(w, 2, 0), jnp.moveaxis(q_dec, 2, 0),
          jnp.moveaxis(attn, 2, 0), jnp.moveaxis(k_dec, 2, 0), jnp.moveaxis(g_last, 2, 0))
    state0 = jnp.zeros((b, h, dk, dv), jnp.float32)
    _, o = lax.scan(step, state0, xs)
    return jnp.moveaxis(o, 0, 2).reshape(b, h, s, dv)


def nsa_attention(q, kv, gates, cmp_pe_k, cmp_pe_v, cmp_w_k, cmp_w_v, rel_bias):
    b, s = q.shape[0], q.shape[1]
    G, HPG, DH, QB = NSA_KV_GROUPS, NSA_HPG, NSA_DH, NSA_QBLOCK
    scale = DH ** -0.5
    k_c, v_c = kv[:, :, 0, 0], kv[:, :, 0, 1]
    k_s, v_s = kv[:, :, 1, 0], kv[:, :, 1, 1]
    k_w, v_w = kv[:, :, 2, 0], kv[:, :, 2, 1]
    n_cmp = (s - CMP_BLOCK) // CMP_STRIDE + 1
    blk_idx = jnp.arange(n_cmp)[:, None] * CMP_STRIDE + jnp.arange(CMP_BLOCK)[None, :]
    k_cmp = jnp.einsum('bnlgd,lde->bnge', k_c[:, blk_idx] + cmp_pe_k[:, None, :], cmp_w_k)
    v_cmp = jnp.einsum('bnlgd,lde->bnge', v_c[:, blk_idx] + cmp_pe_v[:, None, :], cmp_w_v)
    cmp_start = jnp.arange(n_cmp) * CMP_STRIDE
    cmp_end = cmp_start + CMP_BLOCK - 1
    n_sel = s // SEL_BLOCK
    top_n = min(N_SELECT, n_sel)
    sel_start = jnp.arange(n_sel) * SEL_BLOCK
    overlap = jnp.clip(jnp.minimum(cmp_end[:, None] + 1, sel_start[None, :] + SEL_BLOCK)
                       - jnp.maximum(cmp_start[:, None], sel_start[None, :]), 0, None).astype(jnp.float32) / CMP_BLOCK
    k_s_g = k_s.transpose(0, 2, 1, 3)
    v_s_g = v_s.transpose(0, 2, 1, 3)
    pad = ((0, 0), (WINDOW, 0), (0, 0), (0, 0))
    k_w_pad = jnp.pad(k_w, pad)
    v_w_pad = jnp.pad(v_w, pad)
    tbl_g = rel_bias.reshape(REL_BUCKETS, G, HPG).transpose(1, 0, 2)
    b_ix = jnp.arange(b)[:, None, None, None, None]
    g_ix = jnp.arange(G)[None, :, None, None, None]

    def head_bias(dist):
        bb = rel_bias[t5_bucket(dist)]
        return bb.reshape(dist.shape + (G, HPG)).transpose(2, 3, 0, 1)

    def block(args):
        qi, q_blk, g_blk = args
        t = qi * QB + jnp.arange(QB)
        s_c = jnp.einsum('bqghd,bngd->bghqn', q_blk, k_cmp) * scale + head_bias(t[:, None] - cmp_end[None, :])
        p_c = masked_softmax(s_c, cmp_end[None, :] <= t[:, None])
        o_c = jnp.einsum('bghqn,bngd->bqghd', p_c.astype(v_cmp.dtype), v_cmp)
        imp = jnp.einsum('bghqn,nm->bgqm', p_c, overlap)
        blk = jnp.arange(n_sel)[None, :]
        cur = (t // SEL_BLOCK)[:, None]
        valid_blk = sel_start[None, :] <= t[:, None]
        forced = (blk == 0) | (blk == cur) | (blk == cur - 1)
        imp = jnp.where(valid_blk, imp + jnp.where(forced, FORCE_BONUS, 0.0), -1.0)
        top_val, top_idx = lax.top_k(imp, top_n)
        pos = top_idx[..., None] * SEL_BLOCK + jnp.arange(SEL_BLOCK)
        m_s = ((top_val >= 0.0)[..., None] & (pos <= t[None, None, :, None, None])).reshape(b, G, QB, -1)
        k_sel = k_s_g[b_ix, g_ix, pos].reshape(b, G, QB, -1, DH)
        v_sel = v_s_g[b_ix, g_ix, pos].reshape(b, G, QB, -1, DH)
        pos = pos.reshape(b, G, QB, -1)
        bias_s = tbl_g[g_ix[..., 0], t5_bucket(t[None, None, :, None] - pos)]
        s_s = jnp.einsum('bqghd,bgqld->bghql', q_blk, k_sel) * scale + bias_s.transpose(0, 1, 4, 2, 3)
        p_s = masked_softmax(s_s, m_s[:, :, None])
        o_s = jnp.einsum('bghql,bgqld->bqghd', p_s.astype(v_sel.dtype), v_sel)
        start = qi * QB
        k_win = lax.dynamic_slice_in_dim(k_w_pad, start, WINDOW + QB, axis=1)
        v_win = lax.dynamic_slice_in_dim(v_w_pad, start, WINDOW + QB, axis=1)
        kpos = start - WINDOW + jnp.arange(WINDOW + QB)
        dist = t[:, None] - kpos[None, :]
        m_w = (kpos[None, :] >= 0) & (dist >= 0) & (dist < WINDOW)
        s_w = jnp.einsum('bqghd,bkgd->bghqk', q_blk, k_win) * scale + head_bias(dist)
        p_w = masked_softmax(s_w, m_w)
        o_w = jnp.einsum('bghqk,bkgd->bqghd', p_w.astype(v_win.dtype), v_win)
        gate = jax.nn.sigmoid(g_blk.astype(jnp.float32))
        return gate[..., 0:1] * o_c + gate[..., 1:2] * o_s + gate[..., 2:3] * o_w

    nq = s // QB
    q_blocks = q.reshape(b, nq, QB, G, HPG, DH).swapaxes(0, 1)
    g_blocks = gates.reshape(b, nq, QB, G, HPG, NSA_BRANCHES).swapaxes(0, 1)
    out = lax.map(block, (jnp.arange(nq, dtype=jnp.int32), q_blocks, g_blocks))
    return out.swapaxes(0, 1).reshape(b, s, NSA_HEADS * DH).astype(q.dtype)


def peer_ffn(h, w_query, sub_keys, expert_u, expert_v):
    b, s, d = h.shape
    tok = h.reshape(-1, PEER_TOKBLOCK, d)
    kk = PEER_TOPK

    def block(h_blk):
        tb = h_blk.shape[0]
        qry = (h_blk @ w_query).reshape(tb, PEER_HEADS, 2, PEER_DQH)
        sc = jnp.einsum('thpd,hpkd->thpk', qry, sub_keys).astype(jnp.float32)
        s1, i1 = lax.top_k(sc[:, :, 0], kk)
        s2, i2 = lax.top_k(sc[:, :, 1], kk)
        cand = (s1[..., :, None] + s2[..., None, :]).reshape(tb, PEER_HEADS, kk * kk)
        cidx = (i1[..., :, None] * PEER_NKEYS + i2[..., None, :]).reshape(tb, PEER_HEADS, kk * kk)
        top, sel = lax.top_k(cand, kk)
        eidx = jnp.take_along_axis(cidx, sel, axis=-1)
        gate = jax.nn.softmax(top, axis=-1)
        u = expert_u[eidx]
        v = expert_v[eidx]
        act = jax.nn.gelu(jnp.einsum('td,thkd->thk', h_blk, u).astype(jnp.float32))
        return jnp.einsum('thk,thkd->td', (gate * act).astype(v.dtype), v)

    out = lax.map(block, tok)
    return out.reshape(b, s, d).astype(h.dtype)


def _split_points():
    widths = (GDN_QKV_W, GDN_Z_W, GDN_BETA_W, GDN_DECAY_W, NSA_Q_W, NSA_KV_W, NSA_GATE_W)
    pts = []
    acc = 0
    for wd in widths:
        acc += wd
        pts.append(acc)
    return pts


def hybrid_layer(x, rel_bias, norm_mix, w_in, conv_w, a_log, dt_bias, gdn_norm, cmp_pe_k, cmp_pe_v,
                 cmp_w_k, cmp_w_v, w_branch_a, w_branch_b, w_out, norm_ffn, w_query, sub_keys,
                 expert_u, expert_v):
    b, s, _ = x.shape
    h = rms_norm(x, norm_mix)
    proj = h @ w_in
    qkv, z, beta_in, a_in, q_n, kv_n, gate_n, merge = jnp.split(proj, _split_points(), axis=-1)
    qkv = jax.nn.silu(causal_depthwise_conv(qkv, conv_w))
    qa, ka, va = jnp.split(qkv, [GDN_HEADS * GDN_DK, 2 * GDN_HEADS * GDN_DK], axis=-1)
    qa = l2_normalize(qa.reshape(b, s, GDN_HEADS, GDN_DK).transpose(0, 2, 1, 3))
    ka = l2_normalize(ka.reshape(b, s, GDN_HEADS, GDN_DK).transpose(0, 2, 1, 3))
    va = va.reshape(b, s, GDN_HEADS, GDN_DV).transpose(0, 2, 1, 3).astype(jnp.float32)
    beta = jax.nn.sigmoid(beta_in.astype(jnp.float32)).transpose(0, 2, 1)
    g = (-jnp.exp(a_log.astype(jnp.float32))
         * jax.nn.softplus(a_in.astype(jnp.float32) + dt_bias.astype(jnp.float32))).transpose(0, 2, 1)
    oa = gated_delta_rule_chunked(qa, ka, va, g, beta).transpose(0, 2, 1, 3)
    oa = rms_norm(oa, gdn_norm) * jax.nn.silu(z.reshape(b, s, GDN_HEADS, GDN_DV).astype(jnp.float32))
    ya = oa.reshape(b, s, -1).astype(x.dtype) @ w_branch_a
    qb = q_n.reshape(b, s, NSA_KV_GROUPS, NSA_HPG, NSA_DH)
    kvb = kv_n.reshape(b, s, NSA_BRANCHES, 2, NSA_KV_GROUPS, NSA_DH)
    gb = gate_n.reshape(b, s, NSA_KV_GROUPS, NSA_HPG, NSA_BRANCHES)
    ob = nsa_attention(qb, kvb, gb, cmp_pe_k, cmp_pe_v, cmp_w_k, cmp_w_v, rel_bias)
    yb = ob @ w_branch_b
    gate_a, gate_b = jnp.split(merge, 2, axis=-1)
    mixed = (jax.nn.sigmoid(gate_a) * ya + jax.nn.sigmoid(gate_b) * yb).astype(x.dtype)
    x = x + mixed @ w_out
    x = x + peer_ffn(rms_norm(x, norm_ffn), w_query, sub_keys, expert_u, expert_v)
    return x


def setup_inputs(seed: int = 0) -> dict:
    key = jax.random.key(seed)
    ks = jax.random.split(key, 24)
    f32 = jnp.float32

    def nrm(k, shape, sc):
        return jax.random.normal(k, shape, f32) * sc

    L = DEPTH
    dt = jnp.exp(jax.random.uniform(ks[5], (L, GDN_HEADS), f32, math.log(1e-3), math.log(1e-1)))
    return {
        'x': nrm(ks[0], (BATCH, SEQ, D_MODEL), 1.0),
        'rel_bias': nrm(ks[1], (REL_BUCKETS, NSA_HEADS), 0.5),
        'norm_final': 1.0 + nrm(ks[2], (D_MODEL,), 0.1),
        'norm_mix': 1.0 + nrm(ks[3], (L, D_MODEL), 0.1),
        'w_in': nrm(ks[4], (L, D_MODEL, N_IN), D_MODEL ** -0.5),
        'conv_w': nrm(ks[6], (L, GDN_CONV, GDN_QKV_W), GDN_CONV ** -0.5),
        'a_log': jnp.log(jax.random.uniform(ks[7], (L, GDN_HEADS), f32, 1.0, 16.0)),
        'dt_bias': dt + jnp.log(-jnp.expm1(-dt)),
        'gdn_norm': 1.0 + nrm(ks[8], (L, GDN_DV), 0.1),
        'cmp_pe_k': nrm(ks[9], (L, CMP_BLOCK, NSA_DH), 0.5),
        'cmp_pe_v': nrm(ks[10], (L, CMP_BLOCK, NSA_DH), 0.5),
        'cmp_w_k': nrm(ks[11], (L, CMP_BLOCK, NSA_DH, NSA_DH), (CMP_BLOCK * NSA_DH) ** -0.5),
        'cmp_w_v': nrm(ks[12], (L, CMP_BLOCK, NSA_DH, NSA_DH), (CMP_BLOCK * NSA_DH) ** -0.5),
        'w_branch_a': nrm(ks[13], (L, GDN_HEADS * GDN_DV, D_MODEL), (GDN_HEADS * GDN_DV) ** -0.5),
        'w_branch_b': nrm(ks[14], (L, NSA_HEADS * NSA_DH, D_MODEL), (NSA_HEADS * NSA_DH) ** -0.5),
        'w_out': nrm(ks[15], (L, D_MODEL, D_MODEL), D_MODEL ** -0.5),
        'norm_ffn': 1.0 + nrm(ks[16], (L, D_MODEL), 0.1),
        'w_query': nrm(ks[17], (L, D_MODEL, PEER_HEADS * PEER_DQ), D_MODEL ** -0.5),
        'sub_keys': nrm(ks[18], (L, PEER_HEADS, 2, PEER_NKEYS, PEER_DQH), PEER_DQH ** -0.5),
        'expert_u': nrm(ks[19], (L, PEER_N_EXPERTS, D_MODEL), D_MODEL ** -0.5),
        'expert_v': nrm(ks[20], (L, PEER_N_EXPERTS, D_MODEL), 0.5),
    }


def reference(x, rel_bias, norm_final, norm_mix, w_in, conv_w, a_log, dt_bias, gdn_norm, cmp_pe_k,
              cmp_pe_v, cmp_w_k, cmp_w_v, w_branch_a, w_branch_b, w_out, norm_ffn, w_query, sub_keys,
              expert_u, expert_v):
    for layer in range(DEPTH):
        x = hybrid_layer(x, rel_bias, norm_mix[layer], w_in[layer], conv_w[layer], a_log[layer],
                         dt_bias[layer], gdn_norm[layer], cmp_pe_k[layer], cmp_pe_v[layer],
                         cmp_w_k[layer], cmp_w_v[layer], w_branch_a[layer], w_branch_b[layer],
                         w_out[layer], norm_ffn[layer], w_query[layer], sub_keys[layer],
                         expert_u[layer], expert_v[layer])
    return rms_norm(x, norm_final)
```

```python
import functools
import math

import numpy as np
import jax
import jax.numpy as jnp
from jax import lax
from jax.experimental import pallas as pl
from jax.experimental.pallas import tpu as pltpu

F32 = jnp.float32
BF16 = jnp.bfloat16
HI = lax.Precision.HIGHEST

RMS_EPS = 1e-6
NEG = -1e30

GDN_HEADS = 8
GDN_DK = 64
GDN_DV = 64
GDN_CONV = 4
GDN_CHUNK = 64

NSA_HEADS = 8
NSA_KV_GROUPS = 2
NSA_HPG = NSA_HEADS // NSA_KV_GROUPS
NSA_DH = 64
NSA_BRANCHES = 3
CMP_BLOCK = 32
CMP_STRIDE = 16
SEL_BLOCK = 64
N_SELECT = 16
WINDOW = 512
FORCE_BONUS = 1e4
REL_BUCKETS = 32
REL_MAX_DIST = 128

PEER_HEADS = 8
PEER_NKEYS = 128
PEER_TOPK = 16
PEER_DQH = 128

NSA_TQ = 128
PEER_TOK = 8
VMEM_LIMIT = 56 * 1024 * 1024


def _t5_thresholds():
    d = np.arange(0, 4 * REL_MAX_DIST)
    max_exact = REL_BUCKETS // 2
    dd = np.maximum(d, 1).astype(np.float64)
    large = max_exact + (np.log(dd / max_exact) / math.log(REL_MAX_DIST / max_exact)
                         * (REL_BUCKETS - max_exact)).astype(np.int32)
    large = np.minimum(large, REL_BUCKETS - 1)
    b = np.where(d < max_exact, d, large)
    return [int(np.argmax(b >= k)) for k in range(1, REL_BUCKETS)]


T5_THETA = _t5_thresholds()


def _dot(a, b, precision=None):
    return jnp.dot(a, b, preferred_element_type=F32, precision=precision)


def _dot_nt(a, b, precision=None):
    return lax.dot_general(a, b, (((1,), (1,)), ((), ())), preferred_element_type=F32, precision=precision)


def _dot_tn(a, b, precision=None):
    return lax.dot_general(a, b, (((0,), (0,)), ((), ())), preferred_element_type=F32, precision=precision)


def _sigmoid(x):
    return 1.0 / (1.0 + jnp.exp(-x))


def _silu(x):
    return x * _sigmoid(x)


def _softplus(x):
    return jnp.maximum(x, 0.0) + jnp.log(1.0 + jnp.exp(-jnp.abs(x)))


def _proj_kernel(x_ref, nw_ref, w_ref, *out_refs, widths):
    x = x_ref[...]
    h = x * lax.rsqrt(jnp.mean(x * x, axis=-1, keepdims=True) + RMS_EPS) * nw_ref[...]
    hb = h.astype(BF16)
    off = 0
    for o_ref, wd in zip(out_refs, widths):
        o_ref[...] = _dot(hb, w_ref[:, off:off + wd])
        off += wd


def _proj(xf, norm_w, w_cat, widths, tm=256):
    t, d = xf.shape
    n = w_cat.shape[1]
    return pl.pallas_call(
        functools.partial(_proj_kernel, widths=widths),
        grid=(t // tm,),
        in_specs=[pl.BlockSpec((tm, d), lambda i: (i, 0)),
                  pl.BlockSpec((1, d), lambda i: (0, 0)),
                  pl.BlockSpec((d, n), lambda i: (0, 0))],
        out_specs=[pl.BlockSpec((tm, wd), lambda i: (i, 0)) for wd in widths],
        out_shape=[jax.ShapeDtypeStruct((t, wd), F32) for wd in widths],
        compiler_params=pltpu.CompilerParams(dimension_semantics=("arbitrary",), vmem_limit_bytes=VMEM_LIMIT),
    )(xf, norm_w.reshape(1, d), w_cat)


def _gdn_kernel(qkv_ref, z_ref, bcol_ref, acol_ref, brow_ref, arow_ref, convw_ref, alog_c_ref, dtb_c_ref,
                alog_r_ref, dtb_r_ref, gnorm_ref, o_ref, xbuf_ref, state_ref):
    c = GDN_CHUNK
    hd = GDN_HEADS * GDN_DK

    @pl.when(pl.program_id(1) == 0)
    def _():
        xbuf_ref[0:8, :] = jnp.zeros((8, xbuf_ref.shape[1]), F32)
        state_ref[...] = jnp.zeros(state_ref.shape, F32)

    xbuf_ref[8:8 + c, :] = qkv_ref[...]
    acc = xbuf_ref[pl.ds(8 - (GDN_CONV - 1), c), :] * convw_ref[0:1, :]
    for j in range(1, GDN_CONV):
        acc = acc + xbuf_ref[pl.ds(8 - (GDN_CONV - 1) + j, c), :] * convw_ref[j:j + 1, :]
    xbuf_ref[0:8, :] = xbuf_ref[c:c + 8, :]
    qkv = _silu(acc)

    row = lax.broadcasted_iota(jnp.int32, (c, c), 0)
    col = lax.broadcasted_iota(jnp.int32, (c, c), 1)
    lower = row >= col
    strict = row > col
    eye = (row == col).astype(F32)
    ltri = lower.astype(F32)
    utri = (row <= col).astype(F32)

    beta_c = _sigmoid(bcol_ref[...])
    g_c = -jnp.exp(alog_c_ref[...]) * _softplus(acol_ref[...] + dtb_c_ref[...])
    g_r = -jnp.exp(alog_r_ref[...]) * _softplus(arow_ref[0, 0] + dtb_r_ref[...])
    gc_c = _dot(ltri, g_c, HI)
    gc_r = _dot(g_r, utri, HI)

    for h in range(GDN_HEADS):
        q = qkv[:, h * GDN_DK:(h + 1) * GDN_DK]
        k = qkv[:, hd + h * GDN_DK:hd + (h + 1) * GDN_DK]
        v = qkv[:, 2 * hd + h * GDN_DV:2 * hd + (h + 1) * GDN_DV]
        q = q * lax.rsqrt(jnp.sum(q * q, axis=-1, keepdims=True) + 1e-6) * (GDN_DK ** -0.5)
        k = k * lax.rsqrt(jnp.sum(k * k, axis=-1, keepdims=True) + 1e-6)
        beta = beta_c[:, h:h + 1]
        gi = gc_c[:, h:h + 1]
        gj = gc_r[h:h + 1, :]
        decay = jnp.where(lower, jnp.exp(jnp.where(lower, gi - gj, 0.0)), 0.0)
        kb = k * beta
        a = jnp.where(strict, _dot_nt(kb, k, HI) * decay, 0.0)
        tinv = eye - a
        p = _dot(a, a, HI)
        for s in range(5):
            tinv = tinv + _dot(tinv, p, HI)
            if s < 4:
                p = _dot(p, p, HI)
        eg = jnp.exp(gi)
        u = _dot(tinv, v * beta, HI)
        w = _dot(tinv, kb * eg, HI)
        attn = _dot_nt(q, k, HI) * decay
        g_last = gi[c - 1:c, :]
        q_dec = q * eg
        k_dec = k * jnp.exp(g_last - gi)
        st = state_ref[h]
        v_new = u - _dot(w, st, HI)
        o = _dot(q_dec, st, HI) + _dot(attn, v_new, HI)
        state_ref[h] = st * jnp.exp(g_last) + _dot_tn(k_dec, v_new, HI)
        o = o * lax.rsqrt(jnp.mean(o * o, axis=-1, keepdims=True) + RMS_EPS) * gnorm_ref[...]
        o_ref[:, h * GDN_DV:(h + 1) * GDN_DV] = o * _silu(z_ref[:, h * GDN_DV:(h + 1) * GDN_DV])


def _gdn(qkv, z, small, conv_w, a_log, dt_bias, gdn_norm, b, s):
    c = GDN_CHUNK
    nc = s // c
    t = b * s
    hh = GDN_HEADS
    beta_in = small[:, 0:hh]
    a_in = small[:, hh:2 * hh]
    brow = beta_in.reshape(b, nc, c, hh).transpose(0, 1, 3, 2)
    arow = a_in.reshape(b, nc, c, hh).transpose(0, 1, 3, 2)
    wq = qkv.shape[1]
    return pl.pallas_call(
        _gdn_kernel,
        grid=(b, nc),
        in_specs=[pl.BlockSpec((c, wq), lambda i, j: (i * nc + j, 0)),
                  pl.BlockSpec((c, hh * GDN_DV), lambda i, j: (i * nc + j, 0)),
                  pl.BlockSpec((c, hh), lambda i, j: (i * nc + j, 0)),
                  pl.BlockSpec((c, hh), lambda i, j: (i * nc + j, 0)),
                  pl.BlockSpec((1, 1, hh, c), lambda i, j: (i, j, 0, 0)),
                  pl.BlockSpec((1, 1, hh, c), lambda i, j: (i, j, 0, 0)),
                  pl.BlockSpec((GDN_CONV, wq), lambda i, j: (0, 0)),
                  pl.BlockSpec((1, hh), lambda i, j: (0, 0)),
                  pl.BlockSpec((1, hh), lambda i, j: (0, 0)),
                  pl.BlockSpec((hh, 1), lambda i, j: (0, 0)),
                  pl.BlockSpec((hh, 1), lambda i, j: (0, 0)),
                  pl.BlockSpec((1, GDN_DV), lambda i, j: (0, 0))],
        out_specs=pl.BlockSpec((c, hh * GDN_DV), lambda i, j: (i * nc + j, 0)),
        out_shape=jax.ShapeDtypeStruct((t, hh * GDN_DV), F32),
        scratch_shapes=[pltpu.VMEM((c + 8, wq), F32), pltpu.VMEM((hh, GDN_DK, GDN_DV), F32)],
        compiler_params=pltpu.CompilerParams(dimension_semantics=("arbitrary", "arbitrary"),
                                             vmem_limit_bytes=VMEM_LIMIT),
    )(qkv, z, beta_in, a_in, brow, arow, conv_w, a_log.reshape(1, hh), dt_bias.reshape(1, hh),
      a_log.reshape(hh, 1), dt_bias.reshape(hh, 1), gdn_norm.reshape(1, GDN_DV))


def _cmp_kernel(kc_ref, vc_ref, pek_ref, pev_ref, wk_ref, wv_ref, ko_ref, vo_ref):
    for src, pe, w, dst in ((kc_ref, pek_ref, wk_ref, ko_ref), (vc_ref, pev_ref, wv_ref, vo_ref)):
        r = src[0, 0]
        y_lo = _dot(r + pe[0], w[0], HI)
        y_hi = _dot(r + pe[1], w[1], HI)
        n = y_hi.shape[0]
        dst[0, 0] = y_lo + pltpu.roll(y_hi, n - 1, 0)


def _compress(kc, vc, pe_k, pe_v, w_k, w_v):
    b, g, s, dh = kc.shape
    half = CMP_STRIDE * dh
    nr = s // CMP_STRIDE
    kc2 = kc.reshape(b, g, nr, half)
    vc2 = vc.reshape(b, g, nr, half)
    spec_in = pl.BlockSpec((1, 1, nr, half), lambda i, j: (i, j, 0, 0))
    spec_pe = pl.BlockSpec((2, 1, half), lambda i, j: (0, 0, 0))
    spec_w = pl.BlockSpec((2, half, dh), lambda i, j: (0, 0, 0))
    spec_o = pl.BlockSpec((1, 1, nr, dh), lambda i, j: (i, j, 0, 0))
    return pl.pallas_call(
        _cmp_kernel,
        grid=(b, g),
        in_specs=[spec_in, spec_in, spec_pe, spec_pe, spec_w, spec_w],
        out_specs=[spec_o, spec_o],
        out_shape=[jax.ShapeDtypeStruct((b, g, nr, dh), F32)] * 2,
        compiler_params=pltpu.CompilerParams(dimension_semantics=("arbitrary", "arbitrary"),
                                             vmem_limit_bytes=VMEM_LIMIT),
    )(kc2, vc2, pe_k.reshape(2, 1, half), pe_v.reshape(2, 1, half),
      w_k.reshape(2, half, dh), w_v.reshape(2, half, dh))


def _bias_chain(dist, rb_ref, heads):
    accs = [jnp.full(dist.shape, rb_ref[0, hd], F32) for hd in heads]
    for k in range(1, REL_BUCKETS):
        m = dist >= T5_THETA[k - 1]
        accs = [jnp.where(m, rb_ref[k, hd], a) for a, hd in zip(accs, heads)]
    return accs


def _nsa_kernel(rb_ref, q_ref, gate_ref, kcmp_ref, vcmp_ref, ks_ref, vs_ref, kw_ref, vw_ref, ovl_ref, o_ref,
                btab_ref, m_ref, l_ref, acc_ref, *, group_axis):
    tq = NSA_TQ
    dh = NSA_DH
    hpg = NSA_HPG
    scale = dh ** -0.5
    qi = pl.program_id(2)
    t0 = qi * tq
    ncmp = kcmp_ref.shape[2]
    nsel = ovl_ref.shape[1]
    blk_per_tile = tq // SEL_BLOCK
    nwin = WINDOW // tq

    ri = lax.broadcasted_iota(jnp.int32, (tq, tq), 0)
    ci = lax.broadcasted_iota(jnp.int32, (tq, tq), 1)
    dloc = ri - ci

    gsel = pl.program_id(group_axis)
    heads = [gsel * hpg + hh for hh in range(hpg)]

    @pl.when(qi == 0)
    def _():
        for dl in range(2):
            tabs = _bias_chain(dloc + dl * tq, rb_ref, heads)
            for hh in range(hpg):
                btab_ref[hh, dl] = tabs[hh]

    def far_bias(hh):
        return rb_ref[REL_BUCKETS - 1, heads[hh]]

    q_all = q_ref[...]
    qs = [q_all[:, hh * dh:(hh + 1) * dh] for hh in range(hpg)]
    qb = [x.astype(BF16) for x in qs]
    gates = _sigmoid(gate_ref[...])

    trow = t0 + lax.broadcasted_iota(jnp.int32, (tq, ncmp), 0)
    ncol = lax.broadcasted_iota(jnp.int32, (tq, ncmp), 1)
    dist_c = trow - (ncol * CMP_STRIDE + CMP_BLOCK - 1)
    mask_c = dist_c >= 0
    kcmp = kcmp_ref[0, 0]
    vcmp = vcmp_ref[0, 0]
    o_c = []
    psum = jnp.zeros((tq, ncmp), F32)
    bias_c = _bias_chain(dist_c, rb_ref, heads)
    for hh in range(hpg):
        s = _dot_nt(qs[hh], kcmp, HI) * scale + bias_c[hh]
        s = jnp.where(mask_c, s, NEG)
        mx = jnp.max(s, axis=-1, keepdims=True)
        e = jnp.where(mask_c, jnp.exp(s - mx), 0.0)
        l = jnp.sum(e, axis=-1, keepdims=True)
        p = e * (1.0 / jnp.where(l > 0.0, l, 1.0))
        o_c.append(_dot(p, vcmp, HI))
        psum = psum + p
    imp = _dot(psum, ovl_ref[...], HI)

    tcol = t0 + lax.broadcasted_iota(jnp.int32, (tq, nsel), 0)
    blk = lax.broadcasted_iota(jnp.int32, (tq, nsel), 1)
    cur = tcol // SEL_BLOCK
    valid = blk <= cur
    forced = (blk == 0) | (blk == cur) | (blk == cur - 1)
    work = jnp.where(valid, imp + jnp.where(forced, FORCE_BONUS, 0.0), -1.0)
    sel = jnp.zeros((tq, nsel), F32)
    blkf = blk.astype(F32)
    for _ in range(min(N_SELECT, nsel)):
        mx = jnp.max(work, axis=-1, keepdims=True)
        first = jnp.min(jnp.where(work == mx, blkf, float(nsel)), axis=-1, keepdims=True)
        hit = blkf == first
        sel = jnp.where(hit, 1.0, sel)
        work = jnp.where(hit, -2.0, work)
    sel = jnp.where(valid, sel, 0.0).astype(BF16)

    m_ref[...] = jnp.full(m_ref.shape, NEG, F32)
    l_ref[...] = jnp.zeros(l_ref.shape, F32)
    acc_ref[...] = jnp.zeros(acc_ref.shape, F32)

    def update(idx, s, mask, v_tile):
        s = jnp.where(mask, s, NEG)
        m_old = m_ref[idx]
        m_new = jnp.maximum(m_old, jnp.max(s, axis=-1, keepdims=True))
        alpha = jnp.exp(m_old - m_new)
        p = jnp.where(mask, jnp.exp(s - m_new), 0.0)
        l_ref[idx] = alpha * l_ref[idx] + jnp.sum(p, axis=-1, keepdims=True)
        acc_ref[idx] = alpha * acc_ref[idx] + _dot(p.astype(BF16), v_tile)
        m_ref[idx] = m_new

    eb = lax.broadcasted_iota(jnp.int32, (nsel, tq), 0)
    ek = lax.broadcasted_iota(jnp.int32, (nsel, tq), 1) // SEL_BLOCK

    def sel_tile(j, dl):
        expand = (eb == ek + j * blk_per_tile).astype(BF16)
        mask = _dot(sel, expand) > 0.5
        if dl == 0:
            mask = mask & (dloc >= 0)
        start = pl.multiple_of(j * tq, tq)
        kt = ks_ref[0, 0, pl.ds(start, tq), :].astype(BF16)
        vt = vs_ref[0, 0, pl.ds(start, tq), :].astype(BF16)
        for hh in range(hpg):
            bias = far_bias(hh) if dl is None else btab_ref[hh, dl]
            update(hh, _dot_nt(qb[hh], kt) * scale + bias, mask, vt)

    def far_body(j, carry):
        sel_tile(j, None)
        return carry

    lax.fori_loop(0, jnp.maximum(qi - 1, 0), far_body, 0)

    @pl.when(qi >= 1)
    def _():
        sel_tile(qi - 1, 1)

    sel_tile(qi, 0)

    for dl in range(nwin, -1, -1):
        @pl.when(qi >= dl)
        def _():
            j = qi - dl
            dist = dloc + dl * tq
            mask = (dist >= 0) & (dist < WINDOW)
            start = pl.multiple_of(j * tq, tq)
            kt = kw_ref[0, 0, pl.ds(start, tq), :].astype(BF16)
            vt = vw_ref[0, 0, pl.ds(start, tq), :].astype(BF16)
            for hh in range(hpg):
                bias = far_bias(hh) if dl >= 2 else btab_ref[hh, dl]
                update(hpg + hh, _dot_nt(qb[hh], kt) * scale + bias, mask, vt)

    for hh in range(hpg):
        o_s = acc_ref[hh] * (1.0 / l_ref[hh])
        o_w = acc_ref[hpg + hh] * (1.0 / l_ref[hpg + hh])
        gc = hh * NSA_BRANCHES
        out = gates[:, gc:gc + 1] * o_c[hh] + gates[:, gc + 1:gc + 2] * o_s + gates[:, gc + 2:gc + 3] * o_w
        o_ref[:, hh * dh:(hh + 1) * dh] = out


def _overlap_matrix(s):
    n_rows = s // CMP_STRIDE
    n_cmp = (s - CMP_BLOCK) // CMP_STRIDE + 1
    n_sel = s // SEL_BLOCK
    cmp_start = np.arange(n_rows) * CMP_STRIDE
    cmp_end = cmp_start + CMP_BLOCK - 1
    sel_start = np.arange(n_sel) * SEL_BLOCK
    ov = np.clip(np.minimum(cmp_end[:, None] + 1, sel_start[None, :] + SEL_BLOCK)
                 - np.maximum(cmp_start[:, None], sel_start[None, :]), 0, None).astype(np.float32) / CMP_BLOCK
    ov[n_cmp:] = 0.0
    return jnp.asarray(ov)


def _nsa(q, gates, kcmp, vcmp, ks, vs, kw, vw, rel_bias, b, s):
    t = b * s
    tq = NSA_TQ
    nq = s // tq
    g = NSA_KV_GROUPS
    dh = NSA_DH
    gw = NSA_HPG * dh
    ngate = NSA_HPG * NSA_BRANCHES
    ncmp = kcmp.shape[2]
    nsel = s // SEL_BLOCK
    ovl = _overlap_matrix(s)
    gates_g = gates.reshape(t, g, ngate).transpose(1, 0, 2)
    seq_spec = pl.BlockSpec((1, 1, s, dh), lambda i, j, k: (i, j, 0, 0))
    cmp_spec = pl.BlockSpec((1, 1, ncmp, dh), lambda i, j, k: (i, j, 0, 0))
    nstate = 2 * NSA_HPG
    return pl.pallas_call(
        functools.partial(_nsa_kernel, group_axis=1),
        grid=(b, g, nq),
        in_specs=[pl.BlockSpec(memory_space=pltpu.SMEM),
                  pl.BlockSpec((tq, gw), lambda i, j, k: (i * nq + k, j)),
                  pl.BlockSpec((None, tq, ngate), lambda i, j, k: (j, i * nq + k, 0)),
                  cmp_spec, cmp_spec, seq_spec, seq_spec, seq_spec, seq_spec,
                  pl.BlockSpec((ncmp, nsel), lambda i, j, k: (0, 0))],
        out_specs=pl.BlockSpec((tq, gw), lambda i, j, k: (i * nq + k, j)),
        out_shape=jax.ShapeDtypeStruct((t, g * gw), F32),
        scratch_shapes=[pltpu.VMEM((NSA_HPG, 2, tq, tq), F32),
                        pltpu.VMEM((nstate, tq, 1), F32),
                        pltpu.VMEM((nstate, tq, 1), F32),
                        pltpu.VMEM((nstate, tq, dh), F32)],
        compiler_params=pltpu.CompilerParams(dimension_semantics=("arbitrary", "arbitrary", "arbitrary"),
                                             vmem_limit_bytes=VMEM_LIMIT),
    )(rel_bias, q, gates_g, kcmp, vcmp, ks, vs, kw, vw, ovl)


def _merge_kernel(x_ref, oa_ref, ob_ref, mg_ref, wa_ref, wb_ref, wo_ref, nf_ref, wq_ref, x1_ref, h2_ref, qry_ref):
    d = x_ref.shape[1]
    ya = _dot(oa_ref[...].astype(BF16), wa_ref[...])
    yb = _dot(ob_ref[...].astype(BF16), wb_ref[...])
    mg = mg_ref[...]
    mixed = _sigmoid(mg[:, 0:d]) * ya + _sigmoid(mg[:, d:2 * d]) * yb
    x1 = x_ref[...] + _dot(mixed.astype(BF16), wo_ref[...])
    x1_ref[...] = x1
    h2 = x1 * lax.rsqrt(jnp.mean(x1 * x1, axis=-1, keepdims=True) + RMS_EPS) * nf_ref[...]
    h2_ref[...] = h2
    qry_ref[...] = _dot(h2.astype(BF16), wq_ref[...])


def _merge(xf, oa, ob, mg, w_a, w_b, w_o, norm_ffn, w_query, tm=256):
    t, d = xf.shape
    nq = w_query.shape[1]
    row = lambda w: pl.BlockSpec((tm, w), lambda i: (i, 0))
    full = lambda a: pl.BlockSpec(a.shape, lambda i: (0, 0))
    nf = norm_ffn.reshape(1, d)
    return pl.pallas_call(
        _merge_kernel,
        grid=(t // tm,),
        in_specs=[row(d), row(oa.shape[1]), row(ob.shape[1]), row(mg.shape[1]),
                  full(w_a), full(w_b), full(w_o), full(nf), full(w_query)],
        out_specs=[row(d), row(d), row(nq)],
        out_shape=[jax.ShapeDtypeStruct((t, d), F32), jax.ShapeDtypeStruct((t, d), F32),
                   jax.ShapeDtypeStruct((t, nq), F32)],
        compiler_params=pltpu.CompilerParams(dimension_semantics=("arbitrary",), vmem_limit_bytes=VMEM_LIMIT),
    )(xf, oa, ob, mg, w_a, w_b, w_o, nf, w_query)


def _top_rows(work, k, payload=None):
    n_rows = work.shape[0]
    rows = lax.broadcasted_iota(jnp.int32, work.shape, 0).astype(F32)
    vals, idxs = [], []
    for _ in range(k):
        mx = jnp.max(work, axis=0, keepdims=True)
        first = jnp.min(jnp.where(work == mx, rows, float(n_rows)), axis=0, keepdims=True)
        hit = rows == first
        vals.append(mx)
        if payload is None:
            idxs.append(first)
        else:
            idxs.append(jnp.max(jnp.where(hit, payload, -1.0), axis=0, keepdims=True))
        work = jnp.where(hit, -jnp.inf, work)
    return jnp.concatenate(vals, axis=0), jnp.concatenate(idxs, axis=0)


def _peersel_kernel(qry_ref, keys_ref, eidx_ref, gate_ref):
    kk = PEER_TOPK
    for h in range(PEER_HEADS):
        tops = []
        for p in range(2):
            c0 = (h * 2 + p) * PEER_DQH
            sc = _dot_nt(keys_ref[h, p], qry_ref[:, c0:c0 + PEER_DQH], HI)
            tops.append(_top_rows(sc, kk))
        (s1, i1), (s2, i2) = tops
        cand = jnp.concatenate([s1[a:a + 1, :] + s2 for a in range(kk)], axis=0)
        cidx = jnp.concatenate([i1[a:a + 1, :] * float(PEER_NKEYS) + i2 for a in range(kk)], axis=0)
        top, eidx = _top_rows(cand, kk, payload=cidx)
        e = jnp.exp(top - top[0:1, :])
        gate = e * (1.0 / jnp.sum(e, axis=0, keepdims=True))
        eidx_ref[h * kk:(h + 1) * kk, :] = eidx.astype(jnp.int32)
        gate_ref[h * kk:(h + 1) * kk, :] = gate


def _peer_select(qry, sub_keys, tt=128):
    t, nq = qry.shape
    npair = PEER_HEADS * PEER_TOPK
    return pl.pallas_call(
        _peersel_kernel,
        grid=(t // tt,),
        in_specs=[pl.BlockSpec((tt, nq), lambda i: (i, 0)),
                  pl.BlockSpec(sub_keys.shape, lambda i: (0, 0, 0, 0))],
        out_specs=[pl.BlockSpec((npair, tt), lambda i: (0, i)), pl.BlockSpec((npair, tt), lambda i: (0, i))],
        out_shape=[jax.ShapeDtypeStruct((npair, t), jnp.int32), jax.ShapeDtypeStruct((npair, t), F32)],
        compiler_params=pltpu.CompilerParams(dimension_semantics=("arbitrary",), vmem_limit_bytes=VMEM_LIMIT),
    )(qry, sub_keys)


def _gelu_tanh(x):
    return 0.5 * x * (1.0 + jnp.tanh(math.sqrt(2.0 / math.pi) * (x + 0.044715 * (x * x * x))))


def _peerffn_kernel(idx_cur_ref, idx_nxt_ref, h_ref, gate_ref, x1_ref, nw_ref, u_hbm, v_hbm, o_ref,
                    ubuf, vbuf, usem, vsem):
    i = pl.program_id(0)
    n = pl.num_programs(0)
    npairs = ubuf.shape[1]
    slot = i % 2

    def issue(idx_ref, sl):
        def body(p, carry):
            e = idx_ref[0, 0, p]
            pltpu.make_async_copy(u_hbm.at[pl.ds(e, 1), :], ubuf.at[sl, pl.ds(p, 1), :], usem.at[sl]).start()
            pltpu.make_async_copy(v_hbm.at[pl.ds(e, 1), :], vbuf.at[sl, pl.ds(p, 1), :], vsem.at[sl]).start()
            return carry
        lax.fori_loop(0, npairs, body, 0)

    @pl.when(i == 0)
    def _():
        issue(idx_cur_ref, 0)

    @pl.when(i + 1 < n)
    def _():
        issue(idx_nxt_ref, 1 - slot)

    pltpu.make_async_copy(ubuf.at[slot], ubuf.at[slot], usem.at[slot]).wait()
    h = h_ref[...]
    act = _dot_nt(h.astype(BF16), ubuf[slot].astype(BF16))
    tok = h.shape[0]
    per = npairs // tok
    gate = jnp.concatenate([gate_ref[...]] * tok, axis=1)
    own = (lax.broadcasted_iota(jnp.int32, act.shape, 1) // per) == lax.broadcasted_iota(jnp.int32, act.shape, 0)
    w = jnp.where(own, gate * _gelu_tanh(act), 0.0)
    pltpu.make_async_copy(vbuf.at[slot], vbuf.at[slot], vsem.at[slot]).wait()
    y = _dot(w.astype(BF16), vbuf[slot].astype(BF16))
    x2 = x1_ref[...] + y
    o_ref[...] = x2 * lax.rsqrt(jnp.mean(x2 * x2, axis=-1, keepdims=True) + RMS_EPS) * nw_ref[...]


def _peer_ffn(eidx_tok, gate_tok, h2, x1, norm_final, expert_u, expert_v):
    t, d = h2.shape
    tok = PEER_TOK
    per = eidx_tok.shape[1]
    npairs = tok * per
    n = t // tok
    idx2 = eidx_tok.reshape(n, 1, npairs)
    return pl.pallas_call(
        _peerffn_kernel,
        grid=(n,),
        in_specs=[pl.BlockSpec((1, 1, npairs), lambda i: (i, 0, 0), memory_space=pltpu.SMEM),
                  pl.BlockSpec((1, 1, npairs), lambda i: (jnp.minimum(i + 1, n - 1), 0, 0),
                               memory_space=pltpu.SMEM),
                  pl.BlockSpec((tok, d), lambda i: (i, 0)),
                  pl.BlockSpec((tok, per), lambda i: (i, 0)),
                  pl.BlockSpec((tok, d), lambda i: (i, 0)),
                  pl.BlockSpec((1, d), lambda i: (0, 0)),
                  pl.BlockSpec(memory_space=pl.ANY),
                  pl.BlockSpec(memory_space=pl.ANY)],
        out_specs=pl.BlockSpec((tok, d), lambda i: (i, 0)),
        out_shape=jax.ShapeDtypeStruct((t, d), F32),
        scratch_shapes=[pltpu.VMEM((2, npairs, d), F32), pltpu.VMEM((2, npairs, d), F32),
                        pltpu.SemaphoreType.DMA((2,)), pltpu.SemaphoreType.DMA((2,))],
        compiler_params=pltpu.CompilerParams(dimension_semantics=("arbitrary",), vmem_limit_bytes=VMEM_LIMIT),
    )(idx2, idx2, h2, gate_tok, x1, norm_final.reshape(1, d), expert_u, expert_v)


def _layer(x, rel_bias, norm_mix, w_in, conv_w, a_log, dt_bias, gdn_norm, cmp_pe_k, cmp_pe_v, cmp_w_k, cmp_w_v,
           w_branch_a, w_branch_b, w_out, norm_ffn, w_query, sub_keys, expert_u, expert_v, norm_out):
    b, s, d = x.shape
    t = b * s
    xf = x.reshape(t, d)
    hk = GDN_HEADS * GDN_DK
    w_qkv = 2 * hk + GDN_HEADS * GDN_DV
    w_z = GDN_HEADS * GDN_DV
    w_nq = NSA_HEADS * NSA_DH
    w_nkv = NSA_BRANCHES * 2 * NSA_KV_GROUPS * NSA_DH
    w_gate = NSA_HEADS * NSA_BRANCHES
    w_small = 2 * GDN_HEADS + w_gate
    pad = (-w_small) % 128
    c0 = w_qkv + w_z
    c1 = c0 + 2 * GDN_HEADS
    c2 = c1 + w_nq
    c3 = c2 + w_nkv
    c4 = c3 + w_gate
    w_cat = jnp.concatenate([w_in[:, :c0], w_in[:, c1:c3], w_in[:, c4:], w_in[:, c0:c1], w_in[:, c3:c4],
                             jnp.zeros((d, pad), w_in.dtype)], axis=1).astype(BF16)
    widths = (w_qkv, w_z, w_nq, w_nkv, 2 * d, w_small + pad)
    qkv, z, nq, nkv, mg, small = _proj(xf, norm_mix, w_cat, widths)

    oa = _gdn(qkv, z, small, conv_w, a_log, dt_bias, gdn_norm, b, s)

    kv = nkv.reshape(b, s, NSA_BRANCHES, 2, NSA_KV_GROUPS, NSA_DH).transpose(2, 3, 0, 4, 1, 5)
    kcmp, vcmp = _compress(kv[0, 0], kv[0, 1], cmp_pe_k, cmp_pe_v, cmp_w_k, cmp_w_v)
    gates = small[:, 2 * GDN_HEADS:2 * GDN_HEADS + w_gate]
    ob = _nsa(nq, gates, kcmp, vcmp, kv[1, 0], kv[1, 1], kv[2, 0], kv[2, 1], rel_bias, b, s)

    x1, h2, qry = _merge(xf, oa, ob, mg, w_branch_a.astype(BF16), w_branch_b.astype(BF16), w_out.astype(BF16),
                         norm_ffn, w_query.astype(BF16))
    eidx, gate = _peer_select(qry, sub_keys)
    out = _peer_ffn(eidx.T, gate.T, h2, x1, norm_out, expert_u, expert_v)
    return out.reshape(b, s, d)


def kernel(x, rel_bias, norm_final, norm_mix, w_in, conv_w, a_log, dt_bias, gdn_norm, cmp_pe_k, cmp_pe_v, cmp_w_k,
           cmp_w_v, w_branch_a, w_branch_b, w_out, norm_ffn, w_query, sub_keys, expert_u, expert_v):
    assert norm_mix.shape[0] == 1, "single-layer block"
    return _layer(x, rel_bias, norm_mix[0], w_in[0], conv_w[0], a_log[0], dt_bias[0], gdn_norm[0], cmp_pe_k[0],
                  cmp_pe_v[0], cmp_w_k[0], cmp_w_v[0], w_branch_a[0], w_branch_b[0], w_out[0], norm_ffn[0],
                  w_query[0], sub_keys[0], expert_u[0], expert_v[0], norm_final)
```

```python
import functools
import math

import numpy as np
import jax
import jax.numpy as jnp
from jax import lax
from jax.experimental import pallas as pl
from jax.experimental.pallas import tpu as pltpu

F32 = jnp.float32
BF16 = jnp.bfloat16
HI = lax.Precision.HIGHEST

RMS_EPS = 1e-6
NEG = -1e30

GDN_HEADS = 8
GDN_DK = 64
GDN_DV = 64
GDN_CONV = 4
GDN_CHUNK = 64

NSA_HEADS = 8
NSA_KV_GROUPS = 2
NSA_HPG = NSA_HEADS // NSA_KV_GROUPS
NSA_DH = 64
NSA_BRANCHES = 3
CMP_BLOCK = 32
CMP_STRIDE = 16
SEL_BLOCK = 64
N_SELECT = 16
WINDOW = 512
FORCE_BONUS = 1e4
REL_BUCKETS = 32
REL_MAX_DIST = 128

PEER_HEADS = 8
PEER_NKEYS = 128
PEER_TOPK = 16
PEER_DQH = 128

NSA_TQ = 128
PEER_TOK = 8
VMEM_LIMIT = 56 * 1024 * 1024


def _t5_thresholds():
    d = np.arange(0, 4 * REL_MAX_DIST)
    max_exact = REL_BUCKETS // 2
    dd = np.maximum(d, 1).astype(np.float64)
    large = max_exact + (np.log(dd / max_exact) / math.log(REL_MAX_DIST / max_exact)
                         * (REL_BUCKETS - max_exact)).astype(np.int32)
    large = np.minimum(large, REL_BUCKETS - 1)
    b = np.where(d < max_exact, d, large)
    return [int(np.argmax(b >= k)) for k in range(1, REL_BUCKETS)]


T5_THETA = _t5_thresholds()


def _dot(a, b, precision=None):
    return jnp.dot(a, b, preferred_element_type=F32, precision=precision)


def _dot_nt(a, b, precision=None):
    return lax.dot_general(a, b, (((1,), (1,)), ((), ())), preferred_element_type=F32, precision=precision)


def _dot_tn(a, b, precision=None):
    return lax.dot_general(a, b, (((0,), (0,)), ((), ())), preferred_element_type=F32, precision=precision)


def _sigmoid(x):
    return 1.0 / (1.0 + jnp.exp(-x))


def _silu(x):
    return x * _sigmoid(x)


def _softplus(x):
    return jnp.maximum(x, 0.0) + jnp.log(1.0 + jnp.exp(-jnp.abs(x)))


def _proj_kernel(x_ref, nw_ref, w_ref, *out_refs, widths):
    x = x_ref[...]
    h = x * lax.rsqrt(jnp.mean(x * x, axis=-1, keepdims=True) + RMS_EPS) * nw_ref[...]
    hb = h.astype(BF16)
    off = 0
    for o_ref, wd in zip(out_refs, widths):
        o_ref[...] = _dot(hb, w_ref[:, off:off + wd])
        off += wd


def _proj(xf, norm_w, w_cat, widths, tm=256):
    t, d = xf.shape
    n = w_cat.shape[1]
    return pl.pallas_call(
        functools.partial(_proj_kernel, widths=widths),
        grid=(t // tm,),
        in_specs=[pl.BlockSpec((tm, d), lambda i: (i, 0)),
                  pl.BlockSpec((1, d), lambda i: (0, 0)),
                  pl.BlockSpec((d, n), lambda i: (0, 0))],
        out_specs=[pl.BlockSpec((tm, wd), lambda i: (i, 0)) for wd in widths],
        out_shape=[jax.ShapeDtypeStruct((t, wd), F32) for wd in widths],
        compiler_params=pltpu.CompilerParams(dimension_semantics=("arbitrary",), vmem_limit_bytes=VMEM_LIMIT),
    )(xf, norm_w.reshape(1, d), w_cat)


def _gdn_kernel(qkv_ref, z_ref, bcol_ref, acol_ref, brow_ref, arow_ref, convw_ref, alog_c_ref, dtb_c_ref,
                alog_r_ref, dtb_r_ref, gnorm_ref, o_ref, xbuf_ref, state_ref):
    c = GDN_CHUNK
    hd = GDN_HEADS * GDN_DK

    @pl.when(pl.program_id(1) == 0)
    def _():
        xbuf_ref[0:8, :] = jnp.zeros((8, xbuf_ref.shape[1]), F32)
        state_ref[...] = jnp.zeros(state_ref.shape, F32)

    xbuf_ref[8:8 + c, :] = qkv_ref[...]
    acc = xbuf_ref[pl.ds(8 - (GDN_CONV - 1), c), :] * convw_ref[0:1, :]
    for j in range(1, GDN_CONV):
        acc = acc + xbuf_ref[pl.ds(8 - (GDN_CONV - 1) + j, c), :] * convw_ref[j:j + 1, :]
    xbuf_ref[0:8, :] = xbuf_ref[c:c + 8, :]
    qkv = _silu(acc)

    row = lax.broadcasted_iota(jnp.int32, (c, c), 0)
    col = lax.broadcasted_iota(jnp.int32, (c, c), 1)
    lower = row >= col
    strict = row > col
    eye = (row == col).astype(F32)
    ltri = lower.astype(F32)
    utri = (row <= col).astype(F32)

    beta_c = _sigmoid(bcol_ref[...])
    g_c = -jnp.exp(alog_c_ref[...]) * _softplus(acol_ref[...] + dtb_c_ref[...])
    g_r = -jnp.exp(alog_r_ref[...]) * _softplus(arow_ref[0, 0] + dtb_r_ref[...])
    gc_c = _dot(ltri, g_c, HI)
    gc_r = _dot(g_r, utri, HI)

    for h in range(GDN_HEADS):
        q = qkv[:, h * GDN_DK:(h + 1) * GDN_DK]
        k = qkv[:, hd + h * GDN_DK:hd + (h + 1) * GDN_DK]
        v = qkv[:, 2 * hd + h * GDN_DV:2 * hd + (h + 1) * GDN_DV]
        q = q * lax.rsqrt(jnp.sum(q * q, axis=-1, keepdims=True) + 1e-6) * (GDN_DK ** -0.5)
        k = k * lax.rsqrt(jnp.sum(k * k, axis=-1, keepdims=True) + 1e-6)
        beta = beta_c[:, h:h + 1]
        gi = gc_c[:, h:h + 1]
        gj = gc_r[h:h + 1, :]
        decay = jnp.where(lower, jnp.exp(jnp.where(lower, gi - gj, 0.0)), 0.0)
        kb = k * beta
        a = jnp.where(strict, _dot_nt(kb, k, HI) * decay, 0.0)
        tinv = eye - a
        p = _dot(a, a, HI)
        for s in range(5):
            tinv = tinv + _dot(tinv, p, HI)
            if s < 4:
                p = _dot(p, p, HI)
        eg = jnp.exp(gi)
        u = _dot(tinv, v * beta, HI)
        w = _dot(tinv, kb * eg, HI)
        attn = _dot_nt(q, k, HI) * decay
        g_last = gi[c - 1:c, :]
        q_dec = q * eg
        k_dec = k * jnp.exp(g_last - gi)
        st = state_ref[h]
        v_new = u - _dot(w, st, HI)
        o = _dot(q_dec, st, HI) + _dot(attn, v_new, HI)
        state_ref[h] = st * jnp.exp(g_last) + _dot_tn(k_dec, v_new, HI)
        o = o * lax.rsqrt(jnp.mean(o * o, axis=-1, keepdims=True) + RMS_EPS) * gnorm_ref[...]
        o_ref[:, h * GDN_DV:(h + 1) * GDN_DV] = o * _silu(z_ref[:, h * GDN_DV:(h + 1) * GDN_DV])


def _gdn(qkv, z, small, conv_w, a_log, dt_bias, gdn_norm, b, s):
    c = GDN_CHUNK
    nc = s // c
    t = b * s
    hh = GDN_HEADS
    beta_in = small[:, 0:hh]
    a_in = small[:, hh:2 * hh]
    brow = beta_in.reshape(b, nc, c, hh).transpose(0, 1, 3, 2)
    arow = a_in.reshape(b, nc, c, hh).transpose(0, 1, 3, 2)
    wq = qkv.shape[1]
    return pl.pallas_call(
        _gdn_kernel,
        grid=(b, nc),
        in_specs=[pl.BlockSpec((c, wq), lambda i, j: (i * nc + j, 0)),
                  pl.BlockSpec((c, hh * GDN_DV), lambda i, j: (i * nc + j, 0)),
                  pl.BlockSpec((c, hh), lambda i, j: (i * nc + j, 0)),
                  pl.BlockSpec((c, hh), lambda i, j: (i * nc + j, 0)),
                  pl.BlockSpec((1, 1, hh, c), lambda i, j: (i, j, 0, 0)),
                  pl.BlockSpec((1, 1, hh, c), lambda i, j: (i, j, 0, 0)),
                  pl.BlockSpec((GDN_CONV, wq), lambda i, j: (0, 0)),
                  pl.BlockSpec((1, hh), lambda i, j: (0, 0)),
                  pl.BlockSpec((1, hh), lambda i, j: (0, 0)),
                  pl.BlockSpec((hh, 1), lambda i, j: (0, 0)),
                  pl.BlockSpec((hh, 1), lambda i, j: (0, 0)),
                  pl.BlockSpec((1, GDN_DV), lambda i, j: (0, 0))],
        out_specs=pl.BlockSpec((c, hh * GDN_DV), lambda i, j: (i * nc + j, 0)),
        out_shape=jax.ShapeDtypeStruct((t, hh * GDN_DV), F32),
        scratch_shapes=[pltpu.VMEM((c + 8, wq), F32), pltpu.VMEM((hh, GDN_DK, GDN_DV), F32)],
        compiler_params=pltpu.CompilerParams(dimension_semantics=("arbitrary", "arbitrary"),
                                             vmem_limit_bytes=VMEM_LIMIT),
    )(qkv, z, beta_in, a_in, brow, arow, conv_w, a_log.reshape(1, hh), dt_bias.reshape(1, hh),
      a_log.reshape(hh, 1), dt_bias.reshape(hh, 1), gdn_norm.reshape(1, GDN_DV))


def _cmp_kernel(kc_ref, vc_ref, pek_ref, pev_ref, wk_ref, wv_ref, ko_ref, vo_ref):
    for src, pe, w, dst in ((kc_ref, pek_ref, wk_ref, ko_ref), (vc_ref, pev_ref, wv_ref, vo_ref)):
        r = src[0, 0]
        y_lo = _dot(r + pe[0], w[0], HI)
        y_hi = _dot(r + pe[1], w[1], HI)
        n = y_hi.shape[0]
        dst[0, 0] = y_lo + pltpu.roll(y_hi, n - 1, 0)


def _compress(kc, vc, pe_k, pe_v, w_k, w_v):
    b, g, s, dh = kc.shape
    half = CMP_STRIDE * dh
    nr = s // CMP_STRIDE
    kc2 = kc.reshape(b, g, nr, half)
    vc2 = vc.reshape(b, g, nr, half)
    spec_in = pl.BlockSpec((1, 1, nr, half), lambda i, j: (i, j, 0, 0))
    spec_pe = pl.BlockSpec((2, 1, half), lambda i, j: (0, 0, 0))
    spec_w = pl.BlockSpec((2, half, dh), lambda i, j: (0, 0, 0))
    spec_o = pl.BlockSpec((1, 1, nr, dh), lambda i, j: (i, j, 0, 0))
    return pl.pallas_call(
        _cmp_kernel,
        grid=(b, g),
        in_specs=[spec_in, spec_in, spec_pe, spec_pe, spec_w, spec_w],
        out_specs=[spec_o, spec_o],
        out_shape=[jax.ShapeDtypeStruct((b, g, nr, dh), F32)] * 2,
        compiler_params=pltpu.CompilerParams(dimension_semantics=("arbitrary", "arbitrary"),
                                             vmem_limit_bytes=VMEM_LIMIT),
    )(kc2, vc2, pe_k.reshape(2, 1, half), pe_v.reshape(2, 1, half),
      w_k.reshape(2, half, dh), w_v.reshape(2, half, dh))


def _bias_chain(dist, rb_ref, heads):
    accs = [jnp.full(dist.shape, rb_ref[0, hd], F32) for hd in heads]
    for k in range(1, REL_BUCKETS):
        m = dist >= T5_THETA[k - 1]
        accs = [jnp.where(m, rb_ref[k, hd], a) for a, hd in zip(accs, heads)]
    return accs


CMP_BAND_LO = (CMP_BLOCK - 1 + REL_MAX_DIST - 1) // CMP_STRIDE
CMP_BAND_HI = (NSA_TQ - 1 - (CMP_BLOCK - 1)) // CMP_STRIDE
CMP_BAND = CMP_BAND_LO + CMP_BAND_HI + 1


def _nsa_kernel(rb_ref, q_ref, gate_ref, kcmp_ref, vcmp_ref, ks_ref, vs_ref, kw_ref, vw_ref, ovlt_ref, o_ref,
                btab_ref, cpatch_ref, mrun_ref, lrun_ref, acc_ref, *, group_axis):
    tq = NSA_TQ
    dh = NSA_DH
    hpg = NSA_HPG
    scale = dh ** -0.5
    qi = pl.program_id(2)
    t0 = qi * tq
    ncmp = kcmp_ref.shape[2]
    nsel = ovlt_ref.shape[0]
    blk_per_tile = tq // SEL_BLOCK
    nwin = WINDOW // tq

    ri = lax.broadcasted_iota(jnp.int32, (tq, tq), 0)
    ci = lax.broadcasted_iota(jnp.int32, (tq, tq), 1)
    dloc = ri - ci

    gsel = pl.program_id(group_axis)
    heads = [gsel * hpg + hh for hh in range(hpg)]

    def far_bias(hh):
        return rb_ref[REL_BUCKETS - 1, heads[hh]]

    @pl.when(qi == 0)
    def _():
        for dl in range(2):
            tabs = _bias_chain(dloc + dl * tq, rb_ref, heads)
            for hh in range(hpg):
                btab_ref[hh, dl] = tabs[hh]
        tabs = _bias_chain(ri - CMP_STRIDE * (ci - CMP_BAND_LO) - (CMP_BLOCK - 1), rb_ref, heads)
        for hh in range(hpg):
            cpatch_ref[hh] = tabs[hh] - far_bias(hh)

    q_all = q_ref[...] * scale
    qs = [q_all[:, hh * dh:(hh + 1) * dh] for hh in range(hpg)]
    qstk = jnp.concatenate(qs, axis=0).astype(BF16)
    gates = _sigmoid(gate_ref[...])

    trow = t0 + lax.broadcasted_iota(jnp.int32, (tq, ncmp), 0)
    ncol = lax.broadcasted_iota(jnp.int32, (tq, ncmp), 1)
    mask_c = trow - (ncol * CMP_STRIDE + CMP_BLOCK - 1) >= 0
    pr = lax.broadcasted_iota(jnp.int32, (tq, ncmp), 0)
    place = ((ncol == t0 // CMP_STRIDE - CMP_BAND_LO + pr) & (pr < CMP_BAND)).astype(F32)
    kcmp = kcmp_ref[0, 0]
    vcmp = vcmp_ref[0, 0]
    o_c = []
    psum = jnp.zeros((tq, ncmp), F32)
    for hh in range(hpg):
        bias = far_bias(hh) + _dot(cpatch_ref[hh], place, HI)
        s = jnp.where(mask_c, _dot_nt(qs[hh], kcmp, HI) + bias, NEG)
        mx = jnp.max(s, axis=-1, keepdims=True)
        e = jnp.where(mask_c, jnp.exp(s - mx), 0.0)
        l = jnp.sum(e, axis=-1, keepdims=True)
        p = e * (1.0 / jnp.where(l > 0.0, l, 1.0))
        o_c.append(_dot(p, vcmp, HI))
        psum = psum + p
    imp_t = _dot_nt(ovlt_ref[...], psum, HI)

    blk_t = lax.broadcasted_iota(jnp.int32, (nsel, tq), 0)
    cur_t = (t0 + lax.broadcasted_iota(jnp.int32, (nsel, tq), 1)) // SEL_BLOCK
    valid_t = blk_t <= cur_t
    forced_t = (blk_t == 0) | (blk_t == cur_t) | (blk_t == cur_t - 1)
    work = jnp.where(valid_t, imp_t + jnp.where(forced_t, FORCE_BONUS, 0.0), -1.0)
    rank = jnp.zeros((nsel, tq), F32)
    for i in range(nsel):
        xi = work[i:i + 1, :]
        rank = rank + jnp.where(blk_t > i, jnp.where(xi >= work, 1.0, 0.0), jnp.where(xi > work, 1.0, 0.0))
    sel_t = jnp.where((rank < float(min(N_SELECT, nsel))) & valid_t, 1.0, 0.0)

    mrun_ref[...] = jnp.full(mrun_ref.shape, NEG, F32)
    lrun_ref[...] = jnp.zeros(lrun_ref.shape, F32)
    acc_ref[...] = jnp.zeros(acc_ref.shape, F32)

    def scores(k_ref, j, dl, mask):
        start = pl.multiple_of(j * tq, tq)
        kt = k_ref[0, 0, pl.ds(start, tq), :].astype(BF16)
        s = _dot_nt(qstk, kt)
        out = []
        for hh in range(hpg):
            bias = far_bias(hh) if (dl is None or dl >= 2) else btab_ref[hh, dl]
            out.append(jnp.where(mask, s[hh * tq:(hh + 1) * tq] + bias, NEG))
        return out

    def pass1(br, k_ref, v_ref, j, dl, mask):
        for hh, s in enumerate(scores(k_ref, j, dl, mask)):
            mrun_ref[br, hh] = jnp.maximum(mrun_ref[br, hh], s)

    def pass2(br, k_ref, v_ref, j, dl, mask):
        start = pl.multiple_of(j * tq, tq)
        vt = v_ref[0, 0, pl.ds(start, tq), :].astype(BF16)
        ps = []
        for hh, s in enumerate(scores(k_ref, j, dl, mask)):
            p = jnp.exp(s - mrun_ref[br, hh])
            lrun_ref[br, hh] = lrun_ref[br, hh] + p
            ps.append(p.astype(BF16))
        acc_ref[br] = acc_ref[br] + _dot(jnp.concatenate(ps, axis=0), vt)

    eb = lax.broadcasted_iota(jnp.int32, (nsel, tq), 0)
    ek = lax.broadcasted_iota(jnp.int32, (nsel, tq), 1) // SEL_BLOCK

    def sel_mask(j, dl):
        expand = jnp.where(eb == ek + j * blk_per_tile, 1.0, 0.0)
        mask = _dot_tn(sel_t, expand) > 0.5
        if dl == 0:
            mask = mask & (dloc >= 0)
        return mask

    def win_mask(dl):
        dist = dloc + dl * tq
        return (dist >= 0) & (dist < WINDOW)

    def sweep(fn):
        def far_body(j, carry):
            fn(0, ks_ref, vs_ref, j, None, sel_mask(j, None))
            return carry
        lax.fori_loop(0, jnp.maximum(qi - 1, 0), far_body, 0)

        @pl.when(qi >= 1)
        def _():
            fn(0, ks_ref, vs_ref, qi - 1, 1, sel_mask(qi - 1, 1))

        fn(0, ks_ref, vs_ref, qi, 0, sel_mask(qi, 0))
        for dl in range(nwin, -1, -1):
            @pl.when(qi >= dl)
            def _():
                fn(1, kw_ref, vw_ref, qi - dl, dl, win_mask(dl))

    sweep(pass1)
    for br in range(2):
        for hh in range(hpg):
            mrun_ref[br, hh] = jnp.broadcast_to(jnp.max(mrun_ref[br, hh], axis=-1, keepdims=True), (tq, tq))
    sweep(pass2)

    for hh in range(hpg):
        o_b = []
        for br in range(2):
            l = jnp.sum(lrun_ref[br, hh], axis=-1, keepdims=True)
            o_b.append(acc_ref[br, hh * tq:(hh + 1) * tq, :] * (1.0 / l))
        gc = hh * NSA_BRANCHES
        out = gates[:, gc:gc + 1] * o_c[hh] + gates[:, gc + 1:gc + 2] * o_b[0] + gates[:, gc + 2:gc + 3] * o_b[1]
        o_ref[:, hh * dh:(hh + 1) * dh] = out


def _overlap_matrix_t(s):
    n_rows = s // CMP_STRIDE
    n_cmp = (s - CMP_BLOCK) // CMP_STRIDE + 1
    n_sel = s // SEL_BLOCK
    cmp_start = np.arange(n_rows) * CMP_STRIDE
    cmp_end = cmp_start + CMP_BLOCK - 1
    sel_start = np.arange(n_sel) * SEL_BLOCK
    ov = np.clip(np.minimum(cmp_end[:, None] + 1, sel_start[None, :] + SEL_BLOCK)
                 - np.maximum(cmp_start[:, None], sel_start[None, :]), 0, None).astype(np.float32) / CMP_BLOCK
    ov[n_cmp:] = 0.0
    return jnp.asarray(ov.T)


def _nsa(q, gates, kcmp, vcmp, ks, vs, kw, vw, rel_bias, b, s):
    t = b * s
    tq = NSA_TQ
    nq = s // tq
    g = NSA_KV_GROUPS
    dh = NSA_DH
    gw = NSA_HPG * dh
    ngate = NSA_HPG * NSA_BRANCHES
    ncmp = kcmp.shape[2]
    nsel = s // SEL_BLOCK
    ovlt = _overlap_matrix_t(s)
    gates_g = gates.reshape(t, g, ngate).transpose(1, 0, 2)
    seq_spec = pl.BlockSpec((1, 1, s, dh), lambda i, j, k: (i, j, 0, 0))
    cmp_spec = pl.BlockSpec((1, 1, ncmp, dh), lambda i, j, k: (i, j, 0, 0))
    return pl.pallas_call(
        functools.partial(_nsa_kernel, group_axis=1),
        grid=(b, g, nq),
        in_specs=[pl.BlockSpec(memory_space=pltpu.SMEM),
                  pl.BlockSpec((tq, gw), lambda i, j, k: (i * nq + k, j)),
                  pl.BlockSpec((None, tq, ngate), lambda i, j, k: (j, i * nq + k, 0)),
                  cmp_spec, cmp_spec, seq_spec, seq_spec, seq_spec, seq_spec,
                  pl.BlockSpec((nsel, ncmp), lambda i, j, k: (0, 0))],
        out_specs=pl.BlockSpec((tq, gw), lambda i, j, k: (i * nq + k, j)),
        out_shape=jax.ShapeDtypeStruct((t, g * gw), F32),
        scratch_shapes=[pltpu.VMEM((NSA_HPG, 2, tq, tq), F32),
                        pltpu.VMEM((NSA_HPG, tq, tq), F32),
                        pltpu.VMEM((2, NSA_HPG, tq, tq), F32),
                        pltpu.VMEM((2, NSA_HPG, tq, tq), F32),
                        pltpu.VMEM((2, NSA_HPG * tq, dh), F32)],
        compiler_params=pltpu.CompilerParams(dimension_semantics=("arbitrary", "arbitrary", "arbitrary"),
                                             vmem_limit_bytes=VMEM_LIMIT),
    )(rel_bias, q, gates_g, kcmp, vcmp, ks, vs, kw, vw, ovlt)


def _merge_kernel(x_ref, oa_ref, ob_ref, mg_ref, wa_ref, wb_ref, wo_ref, nf_ref, wq_ref, x1_ref, h2_ref, qry_ref):
    d = x_ref.shape[1]
    ya = _dot(oa_ref[...].astype(BF16), wa_ref[...])
    yb = _dot(ob_ref[...].astype(BF16), wb_ref[...])
    mg = mg_ref[...]
    mixed = _sigmoid(mg[:, 0:d]) * ya + _sigmoid(mg[:, d:2 * d]) * yb
    x1 = x_ref[...] + _dot(mixed.astype(BF16), wo_ref[...])
    x1_ref[...] = x1
    h2 = x1 * lax.rsqrt(jnp.mean(x1 * x1, axis=-1, keepdims=True) + RMS_EPS) * nf_ref[...]
    h2_ref[...] = h2
    qry_ref[...] = _dot(h2.astype(BF16), wq_ref[...])


def _merge(xf, oa, ob, mg, w_a, w_b, w_o, norm_ffn, w_query, tm=256):
    t, d = xf.shape
    nq = w_query.shape[1]
    row = lambda w: pl.BlockSpec((tm, w), lambda i: (i, 0))
    full = lambda a: pl.BlockSpec(a.shape, lambda i: (0, 0))
    nf = norm_ffn.reshape(1, d)
    return pl.pallas_call(
        _merge_kernel,
        grid=(t // tm,),
        in_specs=[row(d), row(oa.shape[1]), row(ob.shape[1]), row(mg.shape[1]),
                  full(w_a), full(w_b), full(w_o), full(nf), full(w_query)],
        out_specs=[row(d), row(d), row(nq)],
        out_shape=[jax.ShapeDtypeStruct((t, d), F32), jax.ShapeDtypeStruct((t, d), F32),
                   jax.ShapeDtypeStruct((t, nq), F32)],
        compiler_params=pltpu.CompilerParams(dimension_semantics=("arbitrary",), vmem_limit_bytes=VMEM_LIMIT),
    )(xf, oa, ob, mg, w_a, w_b, w_o, nf, w_query)


def _top_rows(work, k, payload=None):
    n_rows = work.shape[0]
    rows = lax.broadcasted_iota(jnp.int32, work.shape, 0).astype(F32)
    vals, idxs = [], []
    for _ in range(k):
        mx = jnp.max(work, axis=0, keepdims=True)
        first = jnp.min(jnp.where(work == mx, rows, float(n_rows)), axis=0, keepdims=True)
        hit = rows == first
        vals.append(mx)
        if payload is None:
            idxs.append(first)
        else:
            idxs.append(jnp.max(jnp.where(hit, payload, -1.0), axis=0, keepdims=True))
        work = jnp.where(hit, -jnp.inf, work)
    return jnp.concatenate(vals, axis=0), jnp.concatenate(idxs, axis=0)


def _peersel_kernel(qry_ref, keys_ref, eidx_ref, gate_ref):
    kk = PEER_TOPK
    for h in range(PEER_HEADS):
        tops = []
        for p in range(2):
            c0 = (h * 2 + p) * PEER_DQH
            sc = _dot_nt(keys_ref[h, p], qry_ref[:, c0:c0 + PEER_DQH], HI)
            tops.append(_top_rows(sc, kk))
        (s1, i1), (s2, i2) = tops
        cand = jnp.concatenate([s1[a:a + 1, :] + s2 for a in range(kk)], axis=0)
        cidx = jnp.concatenate([i1[a:a + 1, :] * float(PEER_NKEYS) + i2 for a in range(kk)], axis=0)
        top, eidx = _top_rows(cand, kk, payload=cidx)
        e = jnp.exp(top - top[0:1, :])
        gate = e * (1.0 / jnp.sum(e, axis=0, keepdims=True))
        eidx_ref[h * kk:(h + 1) * kk, :] = eidx.astype(jnp.int32)
        gate_ref[h * kk:(h + 1) * kk, :] = gate


def _peer_select(qry, sub_keys, tt=128):
    t, nq = qry.shape
    npair = PEER_HEADS * PEER_TOPK
    return pl.pallas_call(
        _peersel_kernel,
        grid=(t // tt,),
        in_specs=[pl.BlockSpec((tt, nq), lambda i: (i, 0)),
                  pl.BlockSpec(sub_keys.shape, lambda i: (0, 0, 0, 0))],
        out_specs=[pl.BlockSpec((npair, tt), lambda i: (0, i)), pl.BlockSpec((npair, tt), lambda i: (0, i))],
        out_shape=[jax.ShapeDtypeStruct((npair, t), jnp.int32), jax.ShapeDtypeStruct((npair, t), F32)],
        compiler_params=pltpu.CompilerParams(dimension_semantics=("arbitrary",), vmem_limit_bytes=VMEM_LIMIT),
    )(qry, sub_keys)


def _gelu_tanh(x):
    return 0.5 * x * (1.0 + jnp.tanh(math.sqrt(2.0 / math.pi) * (x + 0.044715 * (x * x * x))))


def _peerffn_kernel(idx_cur_ref, idx_nxt_ref, h_ref, gate_ref, x1_ref, nw_ref, uv_hbm, o_ref, buf, sem):
    i = pl.program_id(0)
    n = pl.num_programs(0)
    tok, per = gate_ref.shape
    npairs = tok * per
    slot = i % 2
    nxt = 1 - slot

    def row_copy(idx_ref, sl, p):
        e = idx_ref[0, 0, p]
        return pltpu.make_async_copy(uv_hbm.at[pl.ds(e, 1), :], buf.at[sl, pl.ds(p, 1), :], sem.at[sl])

    @pl.when(i == 0)
    def _():
        def body(p, carry):
            row_copy(idx_cur_ref, 0, p).start()
            return carry
        lax.fori_loop(0, npairs, body, 0)

    pltpu.make_async_copy(buf.at[slot], buf.at[slot], sem.at[slot]).wait()

    hb = h_ref[...].astype(BF16)
    gate = gate_ref[...]
    trow = lax.broadcasted_iota(jnp.int32, (tok, per), 0)
    y = jnp.zeros(x1_ref.shape, F32)
    for tt in range(tok):
        for k in range(per):
            row_copy(idx_nxt_ref, nxt, tt * per + k).start()
        rows = buf[slot, tt * per:(tt + 1) * per, :]
        u = lax.bitcast_convert_type(rows << 16, F32).astype(BF16)
        v = lax.bitcast_convert_type(rows & jnp.uint32(0xFFFF0000), F32).astype(BF16)
        act = _dot_nt(hb, u)
        w = jnp.where(trow == tt, gate * _gelu_tanh(act), 0.0)
        y = y + _dot(w.astype(BF16), v)

    @pl.when(i == n - 1)
    def _():
        pltpu.make_async_copy(buf.at[nxt], buf.at[nxt], sem.at[nxt]).wait()

    x2 = x1_ref[...] + y
    o_ref[...] = x2 * lax.rsqrt(jnp.mean(x2 * x2, axis=-1, keepdims=True) + RMS_EPS) * nw_ref[...]


def _pack_tables(expert_u, expert_v):
    u16 = lax.bitcast_convert_type(expert_u.astype(BF16), jnp.uint16).astype(jnp.uint32)
    v16 = lax.bitcast_convert_type(expert_v.astype(BF16), jnp.uint16).astype(jnp.uint32)
    return u16 | (v16 << 16)


def _peer_ffn(eidx_tok, gate_tok, h2, x1, norm_final, expert_u, expert_v):
    t, d = h2.shape
    tok = PEER_TOK
    per = eidx_tok.shape[1]
    npairs = tok * per
    n = t // tok
    idx2 = eidx_tok.reshape(n, 1, npairs)
    uv = _pack_tables(expert_u, expert_v)
    return pl.pallas_call(
        _peerffn_kernel,
        grid=(n,),
        in_specs=[pl.BlockSpec((1, 1, npairs), lambda i: (i, 0, 0), memory_space=pltpu.SMEM),
                  pl.BlockSpec((1, 1, npairs), lambda i: (jnp.minimum(i + 1, n - 1), 0, 0),
                               memory_space=pltpu.SMEM),
                  pl.BlockSpec((tok, d), lambda i: (i, 0)),
                  pl.BlockSpec((tok, per), lambda i: (i, 0)),
                  pl.BlockSpec((tok, d), lambda i: (i, 0)),
                  pl.BlockSpec((1, d), lambda i: (0, 0)),
                  pl.BlockSpec(memory_space=pl.ANY)],
        out_specs=pl.BlockSpec((tok, d), lambda i: (i, 0)),
        out_shape=jax.ShapeDtypeStruct((t, d), F32),
        scratch_shapes=[pltpu.VMEM((2, npairs, d), jnp.uint32), pltpu.SemaphoreType.DMA((2,))],
        compiler_params=pltpu.CompilerParams(dimension_semantics=("arbitrary",), vmem_limit_bytes=VMEM_LIMIT),
    )(idx2, idx2, h2, gate_tok, x1, norm_final.reshape(1, d), uv)


def _layer(x, rel_bias, norm_mix, w_in, conv_w, a_log, dt_bias, gdn_norm, cmp_pe_k, cmp_pe_v, cmp_w_k, cmp_w_v,
           w_branch_a, w_branch_b, w_out, norm_ffn, w_query, sub_keys, expert_u, expert_v, norm_out):
    b, s, d = x.shape
    t = b * s
    xf = x.reshape(t, d)
    hk = GDN_HEADS * GDN_DK
    w_qkv = 2 * hk + GDN_HEADS * GDN_DV
    w_z = GDN_HEADS * GDN_DV
    w_nq = NSA_HEADS * NSA_DH
    w_nkv = NSA_BRANCHES * 2 * NSA_KV_GROUPS * NSA_DH
    w_gate = NSA_HEADS * NSA_BRANCHES
    w_small = 2 * GDN_HEADS + w_gate
    pad = (-w_small) % 128
    c0 = w_qkv + w_z
    c1 = c0 + 2 * GDN_HEADS
    c2 = c1 + w_nq
    c3 = c2 + w_nkv
    c4 = c3 + w_gate
    w_cat = jnp.concatenate([w_in[:, :c0], w_in[:, c1:c3], w_in[:, c4:], w_in[:, c0:c1], w_in[:, c3:c4],
                             jnp.zeros((d, pad), w_in.dtype)], axis=1).astype(BF16)
    widths = (w_qkv, w_z, w_nq, w_nkv, 2 * d, w_small + pad)
    qkv, z, nq, nkv, mg, small = _proj(xf, norm_mix, w_cat, widths)

    oa = _gdn(qkv, z, small, conv_w, a_log, dt_bias, gdn_norm, b, s)

    kv = nkv.reshape(b, s, NSA_BRANCHES, 2, NSA_KV_GROUPS, NSA_DH).transpose(2, 3, 0, 4, 1, 5)
    kcmp, vcmp = _compress(kv[0, 0], kv[0, 1], cmp_pe_k, cmp_pe_v, cmp_w_k, cmp_w_v)
    gates = small[:, 2 * GDN_HEADS:2 * GDN_HEADS + w_gate]
    ob = _nsa(nq, gates, kcmp, vcmp, kv[1, 0], kv[1, 1], kv[2, 0], kv[2, 1], rel_bias, b, s)

    x1, h2, qry = _merge(xf, oa, ob, mg, w_branch_a.astype(BF16), w_branch_b.astype(BF16), w_out.astype(BF16),
                         norm_ffn, w_query.astype(BF16))
    eidx, gate = _peer_select(qry, sub_keys)
    out = _peer_ffn(eidx.T, gate.T, h2, x1, norm_out, expert_u, expert_v)
    return out.reshape(b, s, d)


def kernel(x, rel_bias, norm_final, norm_mix, w_in, conv_w, a_log, dt_bias, gdn_norm, cmp_pe_k, cmp_pe_v, cmp_w_k,
           cmp_w_v, w_branch_a, w_branch_b, w_out, norm_ffn, w_query, sub_keys, expert_u, expert_v):
    assert norm_mix.shape[0] == 1, "single-layer block"
    return _layer(x, rel_bias, norm_mix[0], w_in[0], conv_w[0], a_log[0], dt_bias[0], gdn_norm[0], cmp_pe_k[0],
                  cmp_pe_v[0], cmp_w_k[0], cmp_w_v[0], w_branch_a[0], w_branch_b[0], w_out[0], norm_ffn[0],
                  w_query[0], sub_keys[0], expert_u[0], expert_v[0], norm_final)
```

```python
import functools
import math

import numpy as np
import jax
import jax.numpy as jnp
from jax import lax
from jax.experimental import pallas as pl
from jax.experimental.pallas import tpu as pltpu

F32 = jnp.float32
BF16 = jnp.bfloat16
HI = lax.Precision.HIGHEST

RMS_EPS = 1e-6
NEG = -1e30

GDN_HEADS = 8
GDN_DK = 64
GDN_DV = 64
GDN_CONV = 4
GDN_CHUNK = 64

NSA_HEADS = 8
NSA_KV_GROUPS = 2
NSA_HPG = NSA_HEADS // NSA_KV_GROUPS
NSA_DH = 64
NSA_BRANCHES = 3
CMP_BLOCK = 32
CMP_STRIDE = 16
SEL_BLOCK = 64
N_SELECT = 16
WINDOW = 512
FORCE_BONUS = 1e4
REL_BUCKETS = 32
REL_MAX_DIST = 128

PEER_HEADS = 8
PEER_NKEYS = 128
PEER_TOPK = 16
PEER_DQH = 128

NSA_TQ = 128
PEER_TOK = 8
VMEM_LIMIT = 56 * 1024 * 1024


def _t5_thresholds():
    d = np.arange(0, 4 * REL_MAX_DIST)
    max_exact = REL_BUCKETS // 2
    dd = np.maximum(d, 1).astype(np.float64)
    large = max_exact + (np.log(dd / max_exact) / math.log(REL_MAX_DIST / max_exact)
                         * (REL_BUCKETS - max_exact)).astype(np.int32)
    large = np.minimum(large, REL_BUCKETS - 1)
    b = np.where(d < max_exact, d, large)
    return [int(np.argmax(b >= k)) for k in range(1, REL_BUCKETS)]


T5_THETA = _t5_thresholds()


def _dot(a, b, precision=None):
    return jnp.dot(a, b, preferred_element_type=F32, precision=precision)


def _dot_nt(a, b, precision=None):
    return lax.dot_general(a, b, (((1,), (1,)), ((), ())), preferred_element_type=F32, precision=precision)


def _dot_tn(a, b, precision=None):
    return lax.dot_general(a, b, (((0,), (0,)), ((), ())), preferred_element_type=F32, precision=precision)


def _dot_split(x, w_bf16):
    hi = x.astype(BF16)
    lo = (x - hi.astype(F32)).astype(BF16)
    return _dot(hi, w_bf16) + _dot(lo, w_bf16)


def _sigmoid(x):
    return 1.0 / (1.0 + jnp.exp(-x))


def _silu(x):
    return x * _sigmoid(x)


def _softplus(x):
    return jnp.maximum(x, 0.0) + jnp.log(1.0 + jnp.exp(-jnp.abs(x)))


def _proj_kernel(x_ref, nw_ref, w_ref, *out_refs, widths):
    x = x_ref[...]
    h = x * lax.rsqrt(jnp.mean(x * x, axis=-1, keepdims=True) + RMS_EPS) * nw_ref[...]
    hb = h.astype(BF16)
    off = 0
    for o_ref, wd in zip(out_refs, widths):
        o_ref[...] = _dot(hb, w_ref[:, off:off + wd])
        off += wd


def _proj(xf, norm_w, w_cat, widths, tm=256):
    t, d = xf.shape
    n = w_cat.shape[1]
    return pl.pallas_call(
        functools.partial(_proj_kernel, widths=widths),
        grid=(t // tm,),
        in_specs=[pl.BlockSpec((tm, d), lambda i: (i, 0)),
                  pl.BlockSpec((1, d), lambda i: (0, 0)),
                  pl.BlockSpec((d, n), lambda i: (0, 0))],
        out_specs=[pl.BlockSpec((tm, wd), lambda i: (i, 0)) for wd in widths],
        out_shape=[jax.ShapeDtypeStruct((t, wd), F32) for wd in widths],
        compiler_params=pltpu.CompilerParams(dimension_semantics=("arbitrary",), vmem_limit_bytes=VMEM_LIMIT),
    )(xf, norm_w.reshape(1, d), w_cat)


def _gdn_kernel(qkv_ref, z_ref, bcol_ref, acol_ref, arow_ref, convw_ref, alog_c_ref, dtb_c_ref,
                alog_r_ref, dtb_r_ref, gnorm_ref, expand_ref, blk1_ref, o_ref, xbuf_ref, state_ref):
    c = GDN_CHUNK
    nh = GDN_HEADS
    dk = GDN_DK
    hd = nh * dk

    @pl.when(pl.program_id(1) == 0)
    def _():
        xbuf_ref[0:8, :] = jnp.zeros((8, xbuf_ref.shape[1]), F32)
        state_ref[...] = jnp.zeros(state_ref.shape, F32)

    xbuf_ref[8:8 + c, :] = qkv_ref[...]
    acc = xbuf_ref[pl.ds(8 - (GDN_CONV - 1), c), :] * convw_ref[0:1, :]
    for j in range(1, GDN_CONV):
        acc = acc + xbuf_ref[pl.ds(8 - (GDN_CONV - 1) + j, c), :] * convw_ref[j:j + 1, :]
    xbuf_ref[0:8, :] = xbuf_ref[c:c + 8, :]
    qkv = _silu(acc)

    row = lax.broadcasted_iota(jnp.int32, (c, c), 0)
    col = lax.broadcasted_iota(jnp.int32, (c, c), 1)
    lower = row >= col
    strict = row > col
    eye = (row == col).astype(F32)
    ltri = lower.astype(F32)
    utri = (row <= col).astype(F32)

    expand = expand_ref[...]
    blk1 = blk1_ref[...]
    q = qkv[:, 0:hd]
    k = qkv[:, hd:2 * hd]
    v = qkv[:, 2 * hd:3 * hd]
    q = q * lax.rsqrt(_dot_split(q * q, blk1) + 1e-6) * (dk ** -0.5)
    k = k * lax.rsqrt(_dot_split(k * k, blk1) + 1e-6)
    beta = _dot(_sigmoid(bcol_ref[...]), expand, HI)
    g_c = -jnp.exp(alog_c_ref[...]) * _softplus(acol_ref[...] + dtb_c_ref[...])
    g_r = -jnp.exp(alog_r_ref[...]) * _softplus(arow_ref[0, 0] + dtb_r_ref[...])
    gc = _dot(ltri, _dot(g_c, expand, HI), HI)
    gc_r = _dot(g_r, utri, HI)
    eg = jnp.exp(gc)
    g_last = gc[c - 1:c, :]
    gl = jnp.exp(g_last)
    kb = k * beta
    vb = v * beta
    kbe = kb * eg
    q_dec = q * eg
    k_dec = k * jnp.exp(g_last - gc)

    def hs(x, h):
        return x[:, h * dk:(h + 1) * dk]

    def bf(x):
        return x.astype(BF16)

    heads = range(nh)
    decay = [jnp.where(lower, jnp.exp(jnp.where(lower, hs(gc, h) - gc_r[h:h + 1, :], 0.0)), 0.0) for h in heads]
    k_b = [bf(hs(k, h)) for h in heads]
    a = [jnp.where(strict, _dot_nt(bf(hs(kb, h)), k_b[h]) * decay[h], 0.0) for h in heads]
    tinv = [eye - a[h] for h in heads]
    p_b = [bf(a[h]) for h in heads]
    p_b = [bf(_dot(p_b[h], p_b[h])) for h in heads]
    for s in range(5):
        tinv = [tinv[h] + _dot(bf(tinv[h]), p_b[h]) for h in heads]
        if s < 4:
            p_b = [bf(_dot(p_b[h], p_b[h])) for h in heads]
    t_b = [bf(tinv[h]) for h in heads]
    u = [_dot(t_b[h], bf(hs(vb, h))) for h in heads]
    w = [_dot(t_b[h], bf(hs(kbe, h))) for h in heads]
    attn = [_dot_nt(bf(hs(q, h)), k_b[h]) * decay[h] for h in heads]
    st = [state_ref[h] for h in heads]
    st_b = [bf(st[h]) for h in heads]
    v_new = [u[h] - _dot(bf(w[h]), st_b[h]) for h in heads]
    vn_b = [bf(v_new[h]) for h in heads]
    o = [_dot(bf(hs(q_dec, h)), st_b[h]) + _dot(bf(attn[h]), vn_b[h]) for h in heads]
    for h in heads:
        state_ref[h] = st[h] * hs(gl, h) + _dot_tn(bf(hs(k_dec, h)), vn_b[h])
    o_all = jnp.concatenate(o, axis=1)
    ms = _dot_split(o_all * o_all, blk1) * (1.0 / GDN_DV)
    o_ref[...] = o_all * lax.rsqrt(ms + RMS_EPS) * gnorm_ref[...] * _silu(z_ref[...])


def _gdn(qkv, z, small, conv_w, a_log, dt_bias, gdn_norm, b, s):
    c = GDN_CHUNK
    nc = s // c
    t = b * s
    hh = GDN_HEADS
    hd = hh * GDN_DK
    beta_in = small[:, 0:hh]
    a_in = small[:, hh:2 * hh]
    arow = a_in.reshape(b, nc, c, hh).transpose(0, 1, 3, 2)
    wq = qkv.shape[1]
    lane_head = np.arange(hd) // GDN_DK
    expand = jnp.asarray((np.arange(hh)[:, None] == lane_head[None, :]).astype(np.float32))
    blk1 = jnp.asarray((lane_head[:, None] == lane_head[None, :]).astype(np.float32)).astype(BF16)
    const = lambda a: pl.BlockSpec(a.shape, lambda i, j: (0,) * a.ndim)
    args = (conv_w, a_log.reshape(1, hh), dt_bias.reshape(1, hh), a_log.reshape(hh, 1), dt_bias.reshape(hh, 1),
            jnp.tile(gdn_norm.reshape(1, GDN_DV), (1, hh)), expand, blk1)
    return pl.pallas_call(
        _gdn_kernel,
        grid=(b, nc),
        in_specs=[pl.BlockSpec((c, wq), lambda i, j: (i * nc + j, 0)),
                  pl.BlockSpec((c, hh * GDN_DV), lambda i, j: (i * nc + j, 0)),
                  pl.BlockSpec((c, hh), lambda i, j: (i * nc + j, 0)),
                  pl.BlockSpec((c, hh), lambda i, j: (i * nc + j, 0)),
                  pl.BlockSpec((1, 1, hh, c), lambda i, j: (i, j, 0, 0))] + [const(a) for a in args],
        out_specs=pl.BlockSpec((c, hh * GDN_DV), lambda i, j: (i * nc + j, 0)),
        out_shape=jax.ShapeDtypeStruct((t, hh * GDN_DV), F32),
        scratch_shapes=[pltpu.VMEM((c + 8, wq), F32), pltpu.VMEM((hh, GDN_DK, GDN_DV), F32)],
        compiler_params=pltpu.CompilerParams(dimension_semantics=("arbitrary", "arbitrary"),
                                             vmem_limit_bytes=VMEM_LIMIT),
    )(qkv, z, beta_in, a_in, arow, *args)


def _cmp_kernel(kc_ref, vc_ref, pek_ref, pev_ref, wk_ref, wv_ref, ko_ref, vo_ref):
    for src, pe, w, dst in ((kc_ref, pek_ref, wk_ref, ko_ref), (vc_ref, pev_ref, wv_ref, vo_ref)):
        r = src[0, 0]
        y_lo = _dot(r + pe[0], w[0], HI)
        y_hi = _dot(r + pe[1], w[1], HI)
        n = y_hi.shape[0]
        dst[0, 0] = y_lo + pltpu.roll(y_hi, n - 1, 0)


def _compress(kc, vc, pe_k, pe_v, w_k, w_v):
    b, g, s, dh = kc.shape
    half = CMP_STRIDE * dh
    nr = s // CMP_STRIDE
    kc2 = kc.reshape(b, g, nr, half)
    vc2 = vc.reshape(b, g, nr, half)
    spec_in = pl.BlockSpec((1, 1, nr, half), lambda i, j: (i, j, 0, 0))
    spec_pe = pl.BlockSpec((2, 1, half), lambda i, j: (0, 0, 0))
    spec_w = pl.BlockSpec((2, half, dh), lambda i, j: (0, 0, 0))
    spec_o = pl.BlockSpec((1, 1, nr, dh), lambda i, j: (i, j, 0, 0))
    return pl.pallas_call(
        _cmp_kernel,
        grid=(b, g),
        in_specs=[spec_in, spec_in, spec_pe, spec_pe, spec_w, spec_w],
        out_specs=[spec_o, spec_o],
        out_shape=[jax.ShapeDtypeStruct((b, g, nr, dh), F32)] * 2,
        compiler_params=pltpu.CompilerParams(dimension_semantics=("arbitrary", "arbitrary"),
                                             vmem_limit_bytes=VMEM_LIMIT),
    )(kc2, vc2, pe_k.reshape(2, 1, half), pe_v.reshape(2, 1, half),
      w_k.reshape(2, half, dh), w_v.reshape(2, half, dh))


def _bias_chain(dist, rb_ref, heads):
    accs = [jnp.full(dist.shape, rb_ref[0, hd], F32) for hd in heads]
    for k in range(1, REL_BUCKETS):
        m = dist >= T5_THETA[k - 1]
        accs = [jnp.where(m, rb_ref[k, hd], a) for a, hd in zip(accs, heads)]
    return accs


CMP_BAND_LO = (CMP_BLOCK - 1 + REL_MAX_DIST - 1) // CMP_STRIDE
CMP_BAND_HI = (NSA_TQ - 1 - (CMP_BLOCK - 1)) // CMP_STRIDE
CMP_BAND = CMP_BAND_LO + CMP_BAND_HI + 1


def _nsa_kernel(rb_ref, q_ref, gate_ref, kcmp_ref, vcmp_ref, ks_ref, vs_ref, kw_ref, vw_ref, ovlt_ref, o_ref,
                btab_ref, cpatch_ref, mrun_ref, lrun_ref, acc_ref, *, group_axis):
    tq = NSA_TQ
    dh = NSA_DH
    hpg = NSA_HPG
    scale = dh ** -0.5
    qi = pl.program_id(2)
    t0 = qi * tq
    ncmp = kcmp_ref.shape[2]
    nsel = ovlt_ref.shape[0]
    blk_per_tile = tq // SEL_BLOCK
    nwin = WINDOW // tq

    ri = lax.broadcasted_iota(jnp.int32, (tq, tq), 0)
    ci = lax.broadcasted_iota(jnp.int32, (tq, tq), 1)
    dloc = ri - ci

    gsel = pl.program_id(group_axis)
    heads = [gsel * hpg + hh for hh in range(hpg)]

    def far_bias(hh):
        return rb_ref[REL_BUCKETS - 1, heads[hh]]

    @pl.when(qi == 0)
    def _():
        for dl in range(2):
            tabs = _bias_chain(dloc + dl * tq, rb_ref, heads)
            for hh in range(hpg):
                btab_ref[hh, dl] = tabs[hh]
        tabs = _bias_chain(ri - CMP_STRIDE * (ci - CMP_BAND_LO) - (CMP_BLOCK - 1), rb_ref, heads)
        for hh in range(hpg):
            cpatch_ref[hh] = tabs[hh] - far_bias(hh)

    q_all = q_ref[...] * scale
    qs = [q_all[:, hh * dh:(hh + 1) * dh] for hh in range(hpg)]
    qstk = jnp.concatenate(qs, axis=0).astype(BF16)
    gates = _sigmoid(gate_ref[...])

    trow = t0 + lax.broadcasted_iota(jnp.int32, (tq, ncmp), 0)
    ncol = lax.broadcasted_iota(jnp.int32, (tq, ncmp), 1)
    mask_c = trow - (ncol * CMP_STRIDE + CMP_BLOCK - 1) >= 0
    pr = lax.broadcasted_iota(jnp.int32, (tq, ncmp), 0)
    place = jnp.where((ncol == t0 // CMP_STRIDE - CMP_BAND_LO + pr) & (pr < CMP_BAND), 1.0, 0.0).astype(BF16)
    kcmp = kcmp_ref[0, 0]
    vcmp = vcmp_ref[0, 0].astype(BF16)
    o_c = []
    psum = jnp.zeros((tq, ncmp), F32)
    for hh in range(hpg):
        bias = far_bias(hh) + _dot_split(cpatch_ref[hh], place)
        s = jnp.where(mask_c, _dot_nt(qs[hh], kcmp, HI) + bias, NEG)
        mx = jnp.max(s, axis=-1, keepdims=True)
        e = jnp.where(mask_c, jnp.exp(s - mx), 0.0)
        l = jnp.sum(e, axis=-1, keepdims=True)
        p = e * (1.0 / jnp.where(l > 0.0, l, 1.0))
        o_c.append(_dot(p.astype(BF16), vcmp))
        psum = psum + p
    imp_t = _dot_nt(ovlt_ref[...], psum, HI)

    blk_t = lax.broadcasted_iota(jnp.int32, (nsel, tq), 0)
    cur_t = (t0 + lax.broadcasted_iota(jnp.int32, (nsel, tq), 1)) // SEL_BLOCK
    valid_t = blk_t <= cur_t
    forced_t = (blk_t == 0) | (blk_t == cur_t) | (blk_t == cur_t - 1)
    work = jnp.where(valid_t, imp_t + jnp.where(forced_t, FORCE_BONUS, 0.0), -1.0)
    rank = jnp.zeros((nsel, tq), F32)
    for i in range(nsel):
        xi = work[i:i + 1, :]
        rank = rank + jnp.where(blk_t > i, jnp.where(xi >= work, 1.0, 0.0), jnp.where(xi > work, 1.0, 0.0))
    sel_t = jnp.where((rank < float(min(N_SELECT, nsel))) & valid_t, 1.0, 0.0)

    mrun_ref[...] = jnp.full(mrun_ref.shape, NEG, F32)
    lrun_ref[...] = jnp.zeros(lrun_ref.shape, F32)
    acc_ref[...] = jnp.zeros(acc_ref.shape, F32)

    def scores(k_ref, j, dls, mask):
        width = len(dls) * tq
        start = pl.multiple_of(j * tq, tq)
        kt = k_ref[0, 0, pl.ds(start, width), :].astype(BF16)
        s = _dot_nt(qstk, kt)
        out = []
        for hh in range(hpg):
            parts = []
            for cblk, dl in enumerate(dls):
                bias = far_bias(hh) if (dl is None or dl >= 2) else btab_ref[hh, dl]
                parts.append(s[hh * tq:(hh + 1) * tq, cblk * tq:(cblk + 1) * tq] + bias)
            sh = parts[0] if len(parts) == 1 else jnp.concatenate(parts, axis=1)
            out.append(jnp.where(mask, sh, NEG))
        return out

    def fold(x, op):
        r = x[:, 0:tq]
        for cblk in range(1, x.shape[1] // tq):
            r = op(r, x[:, cblk * tq:(cblk + 1) * tq])
        return r

    def pass1(br, k_ref, v_ref, j, dls, mask):
        for hh, s in enumerate(scores(k_ref, j, dls, mask)):
            mrun_ref[br, hh] = jnp.maximum(mrun_ref[br, hh], fold(s, jnp.maximum))

    def pass2(br, k_ref, v_ref, j, dls, mask):
        start = pl.multiple_of(j * tq, tq)
        vt = v_ref[0, 0, pl.ds(start, len(dls) * tq), :].astype(BF16)
        ps = []
        for hh, s in enumerate(scores(k_ref, j, dls, mask)):
            m = mrun_ref[br, hh]
            p = jnp.exp(s - jnp.concatenate([m] * len(dls), axis=1))
            lrun_ref[br, hh] = lrun_ref[br, hh] + fold(p, jnp.add)
            ps.append(p.astype(BF16))
        acc_ref[br] = acc_ref[br] + _dot(jnp.concatenate(ps, axis=0), vt)

    def span_dist(dls):
        n = len(dls)
        r = lax.broadcasted_iota(jnp.int32, (tq, n * tq), 0)
        cidx = lax.broadcasted_iota(jnp.int32, (tq, n * tq), 1)
        return r - cidx + (dls[-1] + n - 1) * tq

    def sel_mask(j, dls):
        width = len(dls) * tq
        eb = lax.broadcasted_iota(jnp.int32, (nsel, width), 0)
        ek = lax.broadcasted_iota(jnp.int32, (nsel, width), 1) // SEL_BLOCK
        expand = jnp.where(eb == ek + j * blk_per_tile, 1.0, 0.0)
        mask = _dot_tn(sel_t, expand) > 0.5
        if dls[-1] == 0:
            mask = mask & (span_dist(dls) >= 0)
        return mask

    def win_mask(dls):
        dist = span_dist(dls)
        return (dist >= 0) & (dist < WINDOW)

    far_dls = [None] * 4
    win_dls = list(range(nwin, -1, -1))

    def sweep(fn):
        nfar = jnp.maximum(qi - 1, 0)
        nspan = nfar // len(far_dls)

        def span_body(c, carry):
            fn(0, ks_ref, vs_ref, c * len(far_dls), far_dls, sel_mask(c * len(far_dls), far_dls))
            return carry
        lax.fori_loop(0, nspan, span_body, 0)

        def far_body(j, carry):
            fn(0, ks_ref, vs_ref, j, [None], sel_mask(j, [None]))
            return carry
        lax.fori_loop(nspan * len(far_dls), nfar, far_body, 0)

        @pl.when(qi >= 1)
        def _():
            fn(0, ks_ref, vs_ref, qi - 1, [1, 0], sel_mask(qi - 1, [1, 0]))

        @pl.when(qi == 0)
        def _():
            fn(0, ks_ref, vs_ref, qi, [0], sel_mask(qi, [0]))

        @pl.when(qi >= nwin)
        def _():
            fn(1, kw_ref, vw_ref, qi - nwin, win_dls, win_mask(win_dls))

        for dl in range(nwin - 1, -1, -1):
            @pl.when((qi >= dl) & (qi < nwin))
            def _():
                fn(1, kw_ref, vw_ref, qi - dl, [dl], win_mask([dl]))

    sweep(pass1)
    for br in range(2):
        for hh in range(hpg):
            mrun_ref[br, hh] = jnp.broadcast_to(jnp.max(mrun_ref[br, hh], axis=-1, keepdims=True), (tq, tq))
    sweep(pass2)

    for hh in range(hpg):
        o_b = []
        for br in range(2):
            l = jnp.sum(lrun_ref[br, hh], axis=-1, keepdims=True)
            o_b.append(acc_ref[br, hh * tq:(hh + 1) * tq, :] * (1.0 / l))
        gc = hh * NSA_BRANCHES
        out = gates[:, gc:gc + 1] * o_c[hh] + gates[:, gc + 1:gc + 2] * o_b[0] + gates[:, gc + 2:gc + 3] * o_b[1]
        o_ref[:, hh * dh:(hh + 1) * dh] = out


def _overlap_matrix_t(s):
    n_rows = s // CMP_STRIDE
    n_cmp = (s - CMP_BLOCK) // CMP_STRIDE + 1
    n_sel = s // SEL_BLOCK
    cmp_start = np.arange(n_rows) * CMP_STRIDE
    cmp_end = cmp_start + CMP_BLOCK - 1
    sel_start = np.arange(n_sel) * SEL_BLOCK
    ov = np.clip(np.minimum(cmp_end[:, None] + 1, sel_start[None, :] + SEL_BLOCK)
                 - np.maximum(cmp_start[:, None], sel_start[None, :]), 0, None).astype(np.float32) / CMP_BLOCK
    ov[n_cmp:] = 0.0
    return jnp.asarray(ov.T)


def _nsa(q, gates, kcmp, vcmp, ks, vs, kw, vw, rel_bias, b, s):
    t = b * s
    tq = NSA_TQ
    nq = s // tq
    g = NSA_KV_GROUPS
    dh = NSA_DH
    gw = NSA_HPG * dh
    ngate = NSA_HPG * NSA_BRANCHES
    ncmp = kcmp.shape[2]
    nsel = s // SEL_BLOCK
    ovlt = _overlap_matrix_t(s)
    gates_g = gates.reshape(t, g, ngate).transpose(1, 0, 2)
    seq_spec = pl.BlockSpec((1, 1, s, dh), lambda i, j, k: (i, j, 0, 0))
    cmp_spec = pl.BlockSpec((1, 1, ncmp, dh), lambda i, j, k: (i, j, 0, 0))
    return pl.pallas_call(
        functools.partial(_nsa_kernel, group_axis=1),
        grid=(b, g, nq),
        in_specs=[pl.BlockSpec(memory_space=pltpu.SMEM),
                  pl.BlockSpec((tq, gw), lambda i, j, k: (i * nq + k, j)),
                  pl.BlockSpec((None, tq, ngate), lambda i, j, k: (j, i * nq + k, 0)),
                  cmp_spec, cmp_spec, seq_spec, seq_spec, seq_spec, seq_spec,
                  pl.BlockSpec((nsel, ncmp), lambda i, j, k: (0, 0))],
        out_specs=pl.BlockSpec((tq, gw), lambda i, j, k: (i * nq + k, j)),
        out_shape=jax.ShapeDtypeStruct((t, g * gw), F32),
        scratch_shapes=[pltpu.VMEM((NSA_HPG, 2, tq, tq), F32),
                        pltpu.VMEM((NSA_HPG, tq, tq), F32),
                        pltpu.VMEM((2, NSA_HPG, tq, tq), F32),
                        pltpu.VMEM((2, NSA_HPG, tq, tq), F32),
                        pltpu.VMEM((2, NSA_HPG * tq, dh), F32)],
        compiler_params=pltpu.CompilerParams(dimension_semantics=("arbitrary", "arbitrary", "arbitrary"),
                                             vmem_limit_bytes=VMEM_LIMIT),
    )(rel_bias, q, gates_g, kcmp, vcmp, ks, vs, kw, vw, ovlt)


def _merge_kernel(x_ref, oa_ref, ob_ref, mg_ref, wa_ref, wb_ref, wo_ref, nf_ref, wq_ref, x1_ref, h2_ref, qry_ref):
    d = x_ref.shape[1]
    ya = _dot(oa_ref[...].astype(BF16), wa_ref[...])
    yb = _dot(ob_ref[...].astype(BF16), wb_ref[...])
    mg = mg_ref[...]
    mixed = _sigmoid(mg[:, 0:d]) * ya + _sigmoid(mg[:, d:2 * d]) * yb
    x1 = x_ref[...] + _dot(mixed.astype(BF16), wo_ref[...])
    x1_ref[...] = x1
    h2 = x1 * lax.rsqrt(jnp.mean(x1 * x1, axis=-1, keepdims=True) + RMS_EPS) * nf_ref[...]
    h2_ref[...] = h2
    qry_ref[...] = _dot(h2.astype(BF16), wq_ref[...])


def _merge(xf, oa, ob, mg, w_a, w_b, w_o, norm_ffn, w_query, tm=256):
    t, d = xf.shape
    nq = w_query.shape[1]
    row = lambda w: pl.BlockSpec((tm, w), lambda i: (i, 0))
    full = lambda a: pl.BlockSpec(a.shape, lambda i: (0, 0))
    nf = norm_ffn.reshape(1, d)
    return pl.pallas_call(
        _merge_kernel,
        grid=(t // tm,),
        in_specs=[row(d), row(oa.shape[1]), row(ob.shape[1]), row(mg.shape[1]),
                  full(w_a), full(w_b), full(w_o), full(nf), full(w_query)],
        out_specs=[row(d), row(d), row(nq)],
        out_shape=[jax.ShapeDtypeStruct((t, d), F32), jax.ShapeDtypeStruct((t, d), F32),
                   jax.ShapeDtypeStruct((t, nq), F32)],
        compiler_params=pltpu.CompilerParams(dimension_semantics=("arbitrary",), vmem_limit_bytes=VMEM_LIMIT),
    )(xf, oa, ob, mg, w_a, w_b, w_o, nf, w_query)


def _top_rows(work, k, payload=None):
    n_rows = work.shape[0]
    rows = lax.broadcasted_iota(jnp.int32, work.shape, 0).astype(F32)
    vals, idxs = [], []
    for _ in range(k):
        mx = jnp.max(work, axis=0, keepdims=True)
        first = jnp.min(jnp.where(work == mx, rows, float(n_rows)), axis=0, keepdims=True)
        hit = rows == first
        vals.append(mx)
        if payload is None:
            idxs.append(first)
        else:
            idxs.append(jnp.max(jnp.where(hit, payload, -1.0), axis=0, keepdims=True))
        work = jnp.where(hit, -jnp.inf, work)
    return jnp.concatenate(vals, axis=0), jnp.concatenate(idxs, axis=0)


def _peersel_kernel(qry_ref, keys_ref, eidx_ref, gate_ref):
    kk = PEER_TOPK
    for h in range(PEER_HEADS):
        tops = []
        for p in range(2):
            c0 = (h * 2 + p) * PEER_DQH
            sc = _dot_nt(keys_ref[h, p], qry_ref[:, c0:c0 + PEER_DQH], HI)
            tops.append(_top_rows(sc, kk))
        (s1, i1), (s2, i2) = tops
        cand = jnp.concatenate([s1[a:a + 1, :] + s2 for a in range(kk)], axis=0)
        cidx = jnp.concatenate([i1[a:a + 1, :] * float(PEER_NKEYS) + i2 for a in range(kk)], axis=0)
        top, eidx = _top_rows(cand, kk, payload=cidx)
        e = jnp.exp(top - top[0:1, :])
        gate = e * (1.0 / jnp.sum(e, axis=0, keepdims=True))
        eidx_ref[h * kk:(h + 1) * kk, :] = eidx.astype(jnp.int32)
        gate_ref[h * kk:(h + 1) * kk, :] = gate


def _peer_select(qry, sub_keys, tt=128):
    t, nq = qry.shape
    npair = PEER_HEADS * PEER_TOPK
    return pl.pallas_call(
        _peersel_kernel,
        grid=(t // tt,),
        in_specs=[pl.BlockSpec((tt, nq), lambda i: (i, 0)),
                  pl.BlockSpec(sub_keys.shape, lambda i: (0, 0, 0, 0))],
        out_specs=[pl.BlockSpec((npair, tt), lambda i: (0, i)), pl.BlockSpec((npair, tt), lambda i: (0, i))],
        out_shape=[jax.ShapeDtypeStruct((npair, t), jnp.int32), jax.ShapeDtypeStruct((npair, t), F32)],
        compiler_params=pltpu.CompilerParams(dimension_semantics=("arbitrary",), vmem_limit_bytes=VMEM_LIMIT),
    )(qry, sub_keys)


def _gelu_tanh(x):
    return 0.5 * x * (1.0 + jnp.tanh(math.sqrt(2.0 / math.pi) * (x + 0.044715 * (x * x * x))))


def _peerffn_kernel(idx_cur_ref, idx_nxt_ref, h_ref, gate_ref, x1_ref, nw_ref, uv_hbm, o_ref, buf, sem):
    i = pl.program_id(0)
    n = pl.num_programs(0)
    tok, per = gate_ref.shape
    npairs = tok * per
    slot = i % 2
    nxt = 1 - slot

    def row_copy(idx_ref, sl, p):
        e = idx_ref[0, 0, p]
        return pltpu.make_async_copy(uv_hbm.at[pl.ds(e, 1), :], buf.at[sl, pl.ds(p, 1), :], sem.at[sl])

    @pl.when(i == 0)
    def _():
        def body(p, carry):
            row_copy(idx_cur_ref, 0, p).start()
            return carry
        lax.fori_loop(0, npairs, body, 0)

    pltpu.make_async_copy(buf.at[slot], buf.at[slot], sem.at[slot]).wait()

    hb = h_ref[...].astype(BF16)
    gate = gate_ref[...]
    trow = lax.broadcasted_iota(jnp.int32, (tok, per), 0)
    y = jnp.zeros(x1_ref.shape, F32)
    for tt in range(tok):
        for k in range(per):
            row_copy(idx_nxt_ref, nxt, tt * per + k).start()
        rows = buf[slot, tt * per:(tt + 1) * per, :]
        u = lax.bitcast_convert_type(rows << 16, F32).astype(BF16)
        v = lax.bitcast_convert_type(rows & jnp.uint32(0xFFFF0000), F32).astype(BF16)
        act = _dot_nt(hb, u)
        w = jnp.where(trow == tt, gate * _gelu_tanh(act), 0.0)
        y = y + _dot(w.astype(BF16), v)

    @pl.when(i == n - 1)
    def _():
        pltpu.make_async_copy(buf.at[nxt], buf.at[nxt], sem.at[nxt]).wait()

    x2 = x1_ref[...] + y
    o_ref[...] = x2 * lax.rsqrt(jnp.mean(x2 * x2, axis=-1, keepdims=True) + RMS_EPS) * nw_ref[...]


def _pack_tables(expert_u, expert_v):
    u16 = lax.bitcast_convert_type(expert_u.astype(BF16), jnp.uint16).astype(jnp.uint32)
    v16 = lax.bitcast_convert_type(expert_v.astype(BF16), jnp.uint16).astype(jnp.uint32)
    return u16 | (v16 << 16)


def _peer_ffn(eidx_tok, gate_tok, h2, x1, norm_final, expert_u, expert_v):
    t, d = h2.shape
    tok = PEER_TOK
    per = eidx_tok.shape[1]
    npairs = tok * per
    n = t // tok
    idx2 = eidx_tok.reshape(n, 1, npairs)
    uv = _pack_tables(expert_u, expert_v)
    return pl.pallas_call(
        _peerffn_kernel,
        grid=(n,),
        in_specs=[pl.BlockSpec((1, 1, npairs), lambda i: (i, 0, 0), memory_space=pltpu.SMEM),
                  pl.BlockSpec((1, 1, npairs), lambda i: (jnp.minimum(i + 1, n - 1), 0, 0),
                               memory_space=pltpu.SMEM),
                  pl.BlockSpec((tok, d), lambda i: (i, 0)),
                  pl.BlockSpec((tok, per), lambda i: (i, 0)),
                  pl.BlockSpec((tok, d), lambda i: (i, 0)),
                  pl.BlockSpec((1, d), lambda i: (0, 0)),
                  pl.BlockSpec(memory_space=pl.ANY)],
        out_specs=pl.BlockSpec((tok, d), lambda i: (i, 0)),
        out_shape=jax.ShapeDtypeStruct((t, d), F32),
        scratch_shapes=[pltpu.VMEM((2, npairs, d), jnp.uint32), pltpu.SemaphoreType.DMA((2,))],
        compiler_params=pltpu.CompilerParams(dimension_semantics=("arbitrary",), vmem_limit_bytes=VMEM_LIMIT),
    )(idx2, idx2, h2, gate_tok, x1, norm_final.reshape(1, d), uv)


def _layer(x, rel_bias, norm_mix, w_in, conv_w, a_log, dt_bias, gdn_norm, cmp_pe_k, cmp_pe_v, cmp_w_k, cmp_w_v,
           w_branch_a, w_branch_b, w_out, norm_ffn, w_query, sub_keys, expert_u, expert_v, norm_out):
    b, s, d = x.shape
    t = b * s
    xf = x.reshape(t, d)
    hk = GDN_HEADS * GDN_DK
    w_qkv = 2 * hk + GDN_HEADS * GDN_DV
    w_z = GDN_HEADS * GDN_DV
    w_nq = NSA_HEADS * NSA_DH
    w_nkv = NSA_BRANCHES * 2 * NSA_KV_GROUPS * NSA_DH
    w_gate = NSA_HEADS * NSA_BRANCHES
    w_small = 2 * GDN_HEADS + w_gate
    pad = (-w_small) % 128
    c0 = w_qkv + w_z
    c1 = c0 + 2 * GDN_HEADS
    c2 = c1 + w_nq
    c3 = c2 + w_nkv
    c4 = c3 + w_gate
    w_cat = jnp.concatenate([w_in[:, :c0], w_in[:, c1:c3], w_in[:, c4:], w_in[:, c0:c1], w_in[:, c3:c4],
                             jnp.zeros((d, pad), w_in.dtype)], axis=1).astype(BF16)
    widths = (w_qkv, w_z, w_nq, w_nkv, 2 * d, w_small + pad)
    qkv, z, nq, nkv, mg, small = _proj(xf, norm_mix, w_cat, widths)

    oa = _gdn(qkv, z, small, conv_w, a_log, dt_bias, gdn_norm, b, s)

    kv = nkv.reshape(b, s, NSA_BRANCHES, 2, NSA_KV_GROUPS, NSA_DH).transpose(2, 3, 0, 4, 1, 5)
    kcmp, vcmp = _compress(kv[0, 0], kv[0, 1], cmp_pe_k, cmp_pe_v, cmp_w_k, cmp_w_v)
    gates = small[:, 2 * GDN_HEADS:2 * GDN_HEADS + w_gate]
    ob = _nsa(nq, gates, kcmp, vcmp, kv[1, 0], kv[1, 1], kv[2, 0], kv[2, 1], rel_bias, b, s)

    x1, h2, qry = _merge(xf, oa, ob, mg, w_branch_a.astype(BF16), w_branch_b.astype(BF16), w_out.astype(BF16),
                         norm_ffn, w_query.astype(BF16))
    eidx, gate = _peer_select(qry, sub_keys)
    out = _peer_ffn(eidx.T, gate.T, h2, x1, norm_out, expert_u, expert_v)
    return out.reshape(b, s, d)


def kernel(x, rel_bias, norm_final, norm_mix, w_in, conv_w, a_log, dt_bias, gdn_norm, cmp_pe_k, cmp_pe_v, cmp_w_k,
           cmp_w_v, w_branch_a, w_branch_b, w_out, norm_ffn, w_query, sub_keys, expert_u, expert_v):
    assert norm_mix.shape[0] == 1, "single-layer block"
    return _layer(x, rel_bias, norm_mix[0], w_in[0], conv_w[0], a_log[0], dt_bias[0], gdn_norm[0], cmp_pe_k[0],
                  cmp_pe_v[0], cmp_w_k[0], cmp_w_v[0], w_branch_a[0], w_branch_b[0], w_out[0], norm_ffn[0],
                  w_query[0], sub_keys[0], expert_u[0], expert_v[0], norm_final)
```

```python
import functools
import math

import numpy as np
import jax
import jax.numpy as jnp
from jax import lax
from jax.experimental import pallas as pl
from jax.experimental.pallas import tpu as pltpu

F32 = jnp.float32
BF16 = jnp.bfloat16
HI = lax.Precision.HIGHEST

RMS_EPS = 1e-6
NEG = -1e30

GDN_HEADS = 8
GDN_DK = 64
GDN_DV = 64
GDN_CONV = 4
GDN_CHUNK = 64

NSA_HEADS = 8
NSA_KV_GROUPS = 2
NSA_HPG = NSA_HEADS // NSA_KV_GROUPS
NSA_DH = 64
NSA_BRANCHES = 3
CMP_BLOCK = 32
CMP_STRIDE = 16
SEL_BLOCK = 64
N_SELECT = 16
WINDOW = 512
FORCE_BONUS = 1e4
REL_BUCKETS = 32
REL_MAX_DIST = 128

PEER_HEADS = 8
PEER_NKEYS = 128
PEER_TOPK = 16
PEER_DQH = 128

NSA_TQ = 128
PEER_TOK = 8
VMEM_LIMIT = 56 * 1024 * 1024


def _t5_thresholds():
    d = np.arange(0, 4 * REL_MAX_DIST)
    max_exact = REL_BUCKETS // 2
    dd = np.maximum(d, 1).astype(np.float64)
    large = max_exact + (np.log(dd / max_exact) / math.log(REL_MAX_DIST / max_exact)
                         * (REL_BUCKETS - max_exact)).astype(np.int32)
    large = np.minimum(large, REL_BUCKETS - 1)
    b = np.where(d < max_exact, d, large)
    return [int(np.argmax(b >= k)) for k in range(1, REL_BUCKETS)]


T5_THETA = _t5_thresholds()


def _dot(a, b, precision=None):
    return jnp.dot(a, b, preferred_element_type=F32, precision=precision)


def _dot_nt(a, b, precision=None):
    return lax.dot_general(a, b, (((1,), (1,)), ((), ())), preferred_element_type=F32, precision=precision)


def _dot_tn(a, b, precision=None):
    return lax.dot_general(a, b, (((0,), (0,)), ((), ())), preferred_element_type=F32, precision=precision)


def _dot_split(x, w_bf16):
    hi = x.astype(BF16)
    lo = (x - hi.astype(F32)).astype(BF16)
    return _dot(hi, w_bf16) + _dot(lo, w_bf16)


def _sigmoid(x):
    return 1.0 / (1.0 + jnp.exp(-x))


def _silu(x):
    return x * _sigmoid(x)


def _softplus(x):
    return jnp.maximum(x, 0.0) + jnp.log(1.0 + jnp.exp(-jnp.abs(x)))


def _proj_kernel(x_ref, nw_ref, w_ref, *out_refs, widths):
    x = x_ref[...]
    h = x * lax.rsqrt(jnp.mean(x * x, axis=-1, keepdims=True) + RMS_EPS) * nw_ref[...]
    hb = h.astype(BF16)
    off = 0
    for o_ref, wd in zip(out_refs, widths):
        o_ref[...] = _dot(hb, w_ref[:, off:off + wd])
        off += wd


def _proj(xf, norm_w, w_cat, widths, tm=256):
    t, d = xf.shape
    n = w_cat.shape[1]
    return pl.pallas_call(
        functools.partial(_proj_kernel, widths=widths),
        grid=(t // tm,),
        in_specs=[pl.BlockSpec((tm, d), lambda i: (i, 0)),
                  pl.BlockSpec((1, d), lambda i: (0, 0)),
                  pl.BlockSpec((d, n), lambda i: (0, 0))],
        out_specs=[pl.BlockSpec((tm, wd), lambda i: (i, 0)) for wd in widths],
        out_shape=[jax.ShapeDtypeStruct((t, wd), F32) for wd in widths],
        compiler_params=pltpu.CompilerParams(dimension_semantics=("arbitrary",), vmem_limit_bytes=VMEM_LIMIT),
    )(xf, norm_w.reshape(1, d), w_cat)


def _gdn_kernel(qkv_ref, z_ref, bcol_ref, acol_ref, arow_ref, convw_ref, alog_c_ref, dtb_c_ref,
                alog_r_ref, dtb_r_ref, gnorm_ref, expand_ref, blk1_ref, o_ref, xbuf_ref, state_ref):
    c = GDN_CHUNK
    nh = GDN_HEADS
    dk = GDN_DK
    hd = nh * dk

    @pl.when(pl.program_id(1) == 0)
    def _():
        xbuf_ref[0:8, :] = jnp.zeros((8, xbuf_ref.shape[1]), F32)
        state_ref[...] = jnp.zeros(state_ref.shape, F32)

    xbuf_ref[8:8 + c, :] = qkv_ref[...]
    acc = xbuf_ref[pl.ds(8 - (GDN_CONV - 1), c), :] * convw_ref[0:1, :]
    for j in range(1, GDN_CONV):
        acc = acc + xbuf_ref[pl.ds(8 - (GDN_CONV - 1) + j, c), :] * convw_ref[j:j + 1, :]
    xbuf_ref[0:8, :] = xbuf_ref[c:c + 8, :]
    qkv = _silu(acc)

    row = lax.broadcasted_iota(jnp.int32, (c, c), 0)
    col = lax.broadcasted_iota(jnp.int32, (c, c), 1)
    lower = row >= col
    strict = row > col
    eye = (row == col).astype(F32)
    ltri = lower.astype(F32)
    utri = (row <= col).astype(F32)

    expand = expand_ref[...]
    blk1 = blk1_ref[...]
    q = qkv[:, 0:hd]
    k = qkv[:, hd:2 * hd]
    v = qkv[:, 2 * hd:3 * hd]
    q = q * lax.rsqrt(_dot_split(q * q, blk1) + 1e-6) * (dk ** -0.5)
    k = k * lax.rsqrt(_dot_split(k * k, blk1) + 1e-6)
    beta = _dot(_sigmoid(bcol_ref[...]), expand, HI)
    g_c = -jnp.exp(alog_c_ref[...]) * _softplus(acol_ref[...] + dtb_c_ref[...])
    g_r = -jnp.exp(alog_r_ref[...]) * _softplus(arow_ref[0, 0] + dtb_r_ref[...])
    gc = _dot(ltri, _dot(g_c, expand, HI), HI)
    gc_r = _dot(g_r, utri, HI)
    eg = jnp.exp(gc)
    g_last = gc[c - 1:c, :]
    gl = jnp.exp(g_last)
    kb = k * beta
    vb = v * beta
    kbe = kb * eg
    q_dec = q * eg
    k_dec = k * jnp.exp(g_last - gc)

    def hs(x, h):
        return x[:, h * dk:(h + 1) * dk]

    def bf(x):
        return x.astype(BF16)

    heads = range(nh)
    decay = [jnp.where(lower, jnp.exp(jnp.where(lower, hs(gc, h) - gc_r[h:h + 1, :], 0.0)), 0.0) for h in heads]
    k_b = [bf(hs(k, h)) for h in heads]
    a = [jnp.where(strict, _dot_nt(bf(hs(kb, h)), k_b[h]) * decay[h], 0.0) for h in heads]
    tinv = [eye - a[h] for h in heads]
    p_b = [bf(a[h]) for h in heads]
    p_b = [bf(_dot(p_b[h], p_b[h])) for h in heads]
    for s in range(5):
        tinv = [tinv[h] + _dot(bf(tinv[h]), p_b[h]) for h in heads]
        if s < 4:
            p_b = [bf(_dot(p_b[h], p_b[h])) for h in heads]
    t_b = [bf(tinv[h]) for h in heads]
    u = [_dot(t_b[h], bf(hs(vb, h))) for h in heads]
    w = [_dot(t_b[h], bf(hs(kbe, h))) for h in heads]
    attn = [_dot_nt(bf(hs(q, h)), k_b[h]) * decay[h] for h in heads]
    st = [state_ref[h] for h in heads]
    st_b = [bf(st[h]) for h in heads]
    v_new = [u[h] - _dot(bf(w[h]), st_b[h]) for h in heads]
    vn_b = [bf(v_new[h]) for h in heads]
    o = [_dot(bf(hs(q_dec, h)), st_b[h]) + _dot(bf(attn[h]), vn_b[h]) for h in heads]
    for h in heads:
        state_ref[h] = st[h] * hs(gl, h) + _dot_tn(bf(hs(k_dec, h)), vn_b[h])
    o_all = jnp.concatenate(o, axis=1)
    ms = _dot_split(o_all * o_all, blk1) * (1.0 / GDN_DV)
    o_ref[...] = o_all * lax.rsqrt(ms + RMS_EPS) * gnorm_ref[...] * _silu(z_ref[...])


def _gdn(qkv, z, small, conv_w, a_log, dt_bias, gdn_norm, b, s):
    c = GDN_CHUNK
    nc = s // c
    t = b * s
    hh = GDN_HEADS
    hd = hh * GDN_DK
    beta_in = small[:, 0:hh]
    a_in = small[:, hh:2 * hh]
    arow = a_in.reshape(b, nc, c, hh).transpose(0, 1, 3, 2)
    wq = qkv.shape[1]
    lane_head = np.arange(hd) // GDN_DK
    expand = jnp.asarray((np.arange(hh)[:, None] == lane_head[None, :]).astype(np.float32))
    blk1 = jnp.asarray((lane_head[:, None] == lane_head[None, :]).astype(np.float32)).astype(BF16)
    const = lambda a: pl.BlockSpec(a.shape, lambda i, j: (0,) * a.ndim)
    args = (conv_w, a_log.reshape(1, hh), dt_bias.reshape(1, hh), a_log.reshape(hh, 1), dt_bias.reshape(hh, 1),
            jnp.tile(gdn_norm.reshape(1, GDN_DV), (1, hh)), expand, blk1)
    return pl.pallas_call(
        _gdn_kernel,
        grid=(b, nc),
        in_specs=[pl.BlockSpec((c, wq), lambda i, j: (i * nc + j, 0)),
                  pl.BlockSpec((c, hh * GDN_DV), lambda i, j: (i * nc + j, 0)),
                  pl.BlockSpec((c, hh), lambda i, j: (i * nc + j, 0)),
                  pl.BlockSpec((c, hh), lambda i, j: (i * nc + j, 0)),
                  pl.BlockSpec((1, 1, hh, c), lambda i, j: (i, j, 0, 0))] + [const(a) for a in args],
        out_specs=pl.BlockSpec((c, hh * GDN_DV), lambda i, j: (i * nc + j, 0)),
        out_shape=jax.ShapeDtypeStruct((t, hh * GDN_DV), F32),
        scratch_shapes=[pltpu.VMEM((c + 8, wq), F32), pltpu.VMEM((hh, GDN_DK, GDN_DV), F32)],
        compiler_params=pltpu.CompilerParams(dimension_semantics=("arbitrary", "arbitrary"),
                                             vmem_limit_bytes=VMEM_LIMIT),
    )(qkv, z, beta_in, a_in, arow, *args)


def _cmp_kernel(kc_ref, vc_ref, pek_ref, pev_ref, wk_ref, wv_ref, ko_ref, vo_ref):
    for src, pe, w, dst in ((kc_ref, pek_ref, wk_ref, ko_ref), (vc_ref, pev_ref, wv_ref, vo_ref)):
        r = src[0, 0]
        y_lo = _dot(r + pe[0], w[0], HI)
        y_hi = _dot(r + pe[1], w[1], HI)
        n = y_hi.shape[0]
        dst[0, 0] = y_lo + pltpu.roll(y_hi, n - 1, 0)


def _compress(kc, vc, pe_k, pe_v, w_k, w_v):
    b, g, s, dh = kc.shape
    half = CMP_STRIDE * dh
    nr = s // CMP_STRIDE
    kc2 = kc.reshape(b, g, nr, half)
    vc2 = vc.reshape(b, g, nr, half)
    spec_in = pl.BlockSpec((1, 1, nr, half), lambda i, j: (i, j, 0, 0))
    spec_pe = pl.BlockSpec((2, 1, half), lambda i, j: (0, 0, 0))
    spec_w = pl.BlockSpec((2, half, dh), lambda i, j: (0, 0, 0))
    spec_o = pl.BlockSpec((1, 1, nr, dh), lambda i, j: (i, j, 0, 0))
    return pl.pallas_call(
        _cmp_kernel,
        grid=(b, g),
        in_specs=[spec_in, spec_in, spec_pe, spec_pe, spec_w, spec_w],
        out_specs=[spec_o, spec_o],
        out_shape=[jax.ShapeDtypeStruct((b, g, nr, dh), F32)] * 2,
        compiler_params=pltpu.CompilerParams(dimension_semantics=("arbitrary", "arbitrary"),
                                             vmem_limit_bytes=VMEM_LIMIT),
    )(kc2, vc2, pe_k.reshape(2, 1, half), pe_v.reshape(2, 1, half),
      w_k.reshape(2, half, dh), w_v.reshape(2, half, dh))


def _bias_chain(dist, rb_ref, heads):
    accs = [jnp.full(dist.shape, rb_ref[0, hd], F32) for hd in heads]
    for k in range(1, REL_BUCKETS):
        m = dist >= T5_THETA[k - 1]
        accs = [jnp.where(m, rb_ref[k, hd], a) for a, hd in zip(accs, heads)]
    return accs


CMP_BAND_LO = (CMP_BLOCK - 1 + REL_MAX_DIST - 1) // CMP_STRIDE
CMP_BAND_HI = (NSA_TQ - 1 - (CMP_BLOCK - 1)) // CMP_STRIDE
CMP_BAND = CMP_BAND_LO + CMP_BAND_HI + 1


def _nsa_kernel(rb_ref, q_ref, gate_ref, kcmp_ref, vcmp_ref, ks_ref, vs_ref, kw_ref, vw_ref, ovlt_ref, o_ref,
                btab_ref, cpatch_ref, mrun_ref, lrun_ref, acc_ref, *, group_axis):
    tq = NSA_TQ
    dh = NSA_DH
    hpg = NSA_HPG
    scale = dh ** -0.5
    qi = pl.program_id(2)
    t0 = qi * tq
    ncmp = kcmp_ref.shape[2]
    nsel = ovlt_ref.shape[0]
    blk_per_tile = tq // SEL_BLOCK
    nwin = WINDOW // tq

    ri = lax.broadcasted_iota(jnp.int32, (tq, tq), 0)
    ci = lax.broadcasted_iota(jnp.int32, (tq, tq), 1)
    dloc = ri - ci

    gsel = pl.program_id(group_axis)
    heads = [gsel * hpg + hh for hh in range(hpg)]

    def far_bias(hh):
        return rb_ref[REL_BUCKETS - 1, heads[hh]]

    @pl.when(qi == 0)
    def _():
        for dl in range(2):
            tabs = _bias_chain(dloc + dl * tq, rb_ref, heads)
            for hh in range(hpg):
                btab_ref[hh, dl] = tabs[hh]
        tabs = _bias_chain(ri - CMP_STRIDE * (ci - CMP_BAND_LO) - (CMP_BLOCK - 1), rb_ref, heads)
        for hh in range(hpg):
            cpatch_ref[hh] = tabs[hh] - far_bias(hh)

    q_all = q_ref[...] * scale
    qs = [q_all[:, hh * dh:(hh + 1) * dh] for hh in range(hpg)]
    qstk = jnp.concatenate(qs, axis=0).astype(BF16)
    gates = _sigmoid(gate_ref[...])

    trow = t0 + lax.broadcasted_iota(jnp.int32, (tq, ncmp), 0)
    ncol = lax.broadcasted_iota(jnp.int32, (tq, ncmp), 1)
    mask_c = trow - (ncol * CMP_STRIDE + CMP_BLOCK - 1) >= 0
    pr = lax.broadcasted_iota(jnp.int32, (tq, ncmp), 0)
    place = jnp.where((ncol == t0 // CMP_STRIDE - CMP_BAND_LO + pr) & (pr < CMP_BAND), 1.0, 0.0).astype(BF16)
    kcmp = kcmp_ref[0, 0]
    vcmp = vcmp_ref[0, 0].astype(BF16)
    o_c = []
    psum = jnp.zeros((tq, ncmp), F32)
    for hh in range(hpg):
        bias = far_bias(hh) + _dot_split(cpatch_ref[hh], place)
        s = jnp.where(mask_c, _dot_nt(qs[hh], kcmp, HI) + bias, NEG)
        mx = jnp.max(s, axis=-1, keepdims=True)
        e = jnp.where(mask_c, jnp.exp(s - mx), 0.0)
        l = jnp.sum(e, axis=-1, keepdims=True)
        p = e * (1.0 / jnp.where(l > 0.0, l, 1.0))
        o_c.append(_dot(p.astype(BF16), vcmp))
        psum = psum + p
    imp_t = _dot_nt(ovlt_ref[...], psum, HI)

    blk_t = lax.broadcasted_iota(jnp.int32, (nsel, tq), 0)
    cur_t = (t0 + lax.broadcasted_iota(jnp.int32, (nsel, tq), 1)) // SEL_BLOCK
    valid_t = blk_t <= cur_t
    forced_t = (blk_t == 0) | (blk_t == cur_t) | (blk_t == cur_t - 1)
    work = jnp.where(valid_t, imp_t + jnp.where(forced_t, FORCE_BONUS, 0.0), -1.0)
    rank = jnp.zeros((nsel, tq), F32)
    for i in range(nsel):
        xi = work[i:i + 1, :]
        rank = rank + jnp.where(blk_t > i, jnp.where(xi >= work, 1.0, 0.0), jnp.where(xi > work, 1.0, 0.0))
    sel_t = jnp.where((rank < float(min(N_SELECT, nsel))) & valid_t, 1.0, 0.0)

    mrun_ref[...] = jnp.full(mrun_ref.shape, NEG, F32)
    lrun_ref[...] = jnp.zeros(lrun_ref.shape, F32)
    acc_ref[...] = jnp.zeros(acc_ref.shape, F32)

    def scores(k_ref, j, dls, mask):
        width = len(dls) * tq
        start = pl.multiple_of(j * tq, tq)
        kt = k_ref[0, 0, pl.ds(start, width), :].astype(BF16)
        s = _dot_nt(qstk, kt)
        out = []
        for hh in range(hpg):
            parts = []
            for cblk, dl in enumerate(dls):
                bias = far_bias(hh) if (dl is None or dl >= 2) else btab_ref[hh, dl]
                parts.append(s[hh * tq:(hh + 1) * tq, cblk * tq:(cblk + 1) * tq] + bias)
            sh = parts[0] if len(parts) == 1 else jnp.concatenate(parts, axis=1)
            out.append(jnp.where(mask, sh, NEG))
        return out

    def fold(x, op):
        r = x[:, 0:tq]
        for cblk in range(1, x.shape[1] // tq):
            r = op(r, x[:, cblk * tq:(cblk + 1) * tq])
        return r

    def pass1(br, k_ref, v_ref, j, dls, mask):
        for hh, s in enumerate(scores(k_ref, j, dls, mask)):
            mrun_ref[br, hh] = jnp.maximum(mrun_ref[br, hh], fold(s, jnp.maximum))

    def pass2(br, k_ref, v_ref, j, dls, mask):
        start = pl.multiple_of(j * tq, tq)
        vt = v_ref[0, 0, pl.ds(start, len(dls) * tq), :].astype(BF16)
        ps = []
        for hh, s in enumerate(scores(k_ref, j, dls, mask)):
            m = mrun_ref[br, hh]
            p = jnp.exp(s - jnp.concatenate([m] * len(dls), axis=1))
            lrun_ref[br, hh] = lrun_ref[br, hh] + fold(p, jnp.add)
            ps.append(p.astype(BF16))
        acc_ref[br] = acc_ref[br] + _dot(jnp.concatenate(ps, axis=0), vt)

    def span_dist(dls):
        n = len(dls)
        r = lax.broadcasted_iota(jnp.int32, (tq, n * tq), 0)
        cidx = lax.broadcasted_iota(jnp.int32, (tq, n * tq), 1)
        return r - cidx + (dls[-1] + n - 1) * tq

    def sel_mask(j, dls):
        width = len(dls) * tq
        eb = lax.broadcasted_iota(jnp.int32, (nsel, width), 0)
        ek = lax.broadcasted_iota(jnp.int32, (nsel, width), 1) // SEL_BLOCK
        expand = jnp.where(eb == ek + j * blk_per_tile, 1.0, 0.0)
        mask = _dot_tn(sel_t, expand) > 0.5
        if dls[-1] == 0:
            mask = mask & (span_dist(dls) >= 0)
        return mask

    def win_mask(dls):
        dist = span_dist(dls)
        return (dist >= 0) & (dist < WINDOW)

    far_dls = [None] * 4
    win_dls = list(range(nwin, -1, -1))

    def sweep(fn):
        nfar = jnp.maximum(qi - 1, 0)
        nspan = nfar // len(far_dls)

        def span_body(c, carry):
            fn(0, ks_ref, vs_ref, c * len(far_dls), far_dls, sel_mask(c * len(far_dls), far_dls))
            return carry
        lax.fori_loop(0, nspan, span_body, 0)

        def far_body(j, carry):
            fn(0, ks_ref, vs_ref, j, [None], sel_mask(j, [None]))
            return carry
        lax.fori_loop(nspan * len(far_dls), nfar, far_body, 0)

        @pl.when(qi >= 1)
        def _():
            fn(0, ks_ref, vs_ref, qi - 1, [1, 0], sel_mask(qi - 1, [1, 0]))

        @pl.when(qi == 0)
        def _():
            fn(0, ks_ref, vs_ref, qi, [0], sel_mask(qi, [0]))

        @pl.when(qi >= nwin)
        def _():
            fn(1, kw_ref, vw_ref, qi - nwin, win_dls, win_mask(win_dls))

        for dl in range(nwin - 1, -1, -1):
            @pl.when((qi >= dl) & (qi < nwin))
            def _():
                fn(1, kw_ref, vw_ref, qi - dl, [dl], win_mask([dl]))

    sweep(pass1)
    for br in range(2):
        for hh in range(hpg):
            mrun_ref[br, hh] = jnp.broadcast_to(jnp.max(mrun_ref[br, hh], axis=-1, keepdims=True), (tq, tq))
    sweep(pass2)

    for hh in range(hpg):
        o_b = []
        for br in range(2):
            l = jnp.sum(lrun_ref[br, hh], axis=-1, keepdims=True)
            o_b.append(acc_ref[br, hh * tq:(hh + 1) * tq, :] * (1.0 / l))
        gc = hh * NSA_BRANCHES
        out = gates[:, gc:gc + 1] * o_c[hh] + gates[:, gc + 1:gc + 2] * o_b[0] + gates[:, gc + 2:gc + 3] * o_b[1]
        o_ref[:, hh * dh:(hh + 1) * dh] = out


def _overlap_matrix_t(s):
    n_rows = s // CMP_STRIDE
    n_cmp = (s - CMP_BLOCK) // CMP_STRIDE + 1
    n_sel = s // SEL_BLOCK
    cmp_start = np.arange(n_rows) * CMP_STRIDE
    cmp_end = cmp_start + CMP_BLOCK - 1
    sel_start = np.arange(n_sel) * SEL_BLOCK
    ov = np.clip(np.minimum(cmp_end[:, None] + 1, sel_start[None, :] + SEL_BLOCK)
                 - np.maximum(cmp_start[:, None], sel_start[None, :]), 0, None).astype(np.float32) / CMP_BLOCK
    ov[n_cmp:] = 0.0
    return jnp.asarray(ov.T)


def _nsa(q, gates, kcmp, vcmp, ks, vs, kw, vw, rel_bias, b, s):
    t = b * s
    tq = NSA_TQ
    nq = s // tq
    g = NSA_KV_GROUPS
    dh = NSA_DH
    gw = NSA_HPG * dh
    ngate = NSA_HPG * NSA_BRANCHES
    ncmp = kcmp.shape[2]
    nsel = s // SEL_BLOCK
    ovlt = _overlap_matrix_t(s)
    gates_g = gates.reshape(t, g, ngate).transpose(1, 0, 2)
    seq_spec = pl.BlockSpec((1, 1, s, dh), lambda i, j, k: (i, j, 0, 0))
    cmp_spec = pl.BlockSpec((1, 1, ncmp, dh), lambda i, j, k: (i, j, 0, 0))
    return pl.pallas_call(
        functools.partial(_nsa_kernel, group_axis=1),
        grid=(b, g, nq),
        in_specs=[pl.BlockSpec(memory_space=pltpu.SMEM),
                  pl.BlockSpec((tq, gw), lambda i, j, k: (i * nq + k, j)),
                  pl.BlockSpec((None, tq, ngate), lambda i, j, k: (j, i * nq + k, 0)),
                  cmp_spec, cmp_spec, seq_spec, seq_spec, seq_spec, seq_spec,
                  pl.BlockSpec((nsel, ncmp), lambda i, j, k: (0, 0))],
        out_specs=pl.BlockSpec((tq, gw), lambda i, j, k: (i * nq + k, j)),
        out_shape=jax.ShapeDtypeStruct((t, g * gw), F32),
        scratch_shapes=[pltpu.VMEM((NSA_HPG, 2, tq, tq), F32),
                        pltpu.VMEM((NSA_HPG, tq, tq), F32),
                        pltpu.VMEM((2, NSA_HPG, tq, tq), F32),
                        pltpu.VMEM((2, NSA_HPG, tq, tq), F32),
                        pltpu.VMEM((2, NSA_HPG * tq, dh), F32)],
        compiler_params=pltpu.CompilerParams(dimension_semantics=("arbitrary", "arbitrary", "arbitrary"),
                                             vmem_limit_bytes=VMEM_LIMIT),
    )(rel_bias, q, gates_g, kcmp, vcmp, ks, vs, kw, vw, ovlt)


def _merge_kernel(x_ref, oa_ref, ob_ref, mg_ref, wa_ref, wb_ref, wo_ref, nf_ref, wq_ref, x1_ref, h2_ref, qry_ref):
    d = x_ref.shape[1]
    ya = _dot(oa_ref[...].astype(BF16), wa_ref[...])
    yb = _dot(ob_ref[...].astype(BF16), wb_ref[...])
    mg = mg_ref[...]
    mixed = _sigmoid(mg[:, 0:d]) * ya + _sigmoid(mg[:, d:2 * d]) * yb
    x1 = x_ref[...] + _dot(mixed.astype(BF16), wo_ref[...])
    x1_ref[...] = x1
    h2 = x1 * lax.rsqrt(jnp.mean(x1 * x1, axis=-1, keepdims=True) + RMS_EPS) * nf_ref[...]
    h2_ref[...] = h2
    qry_ref[...] = _dot(h2.astype(BF16), wq_ref[...])


def _merge(xf, oa, ob, mg, w_a, w_b, w_o, norm_ffn, w_query, tm=256):
    t, d = xf.shape
    nq = w_query.shape[1]
    row = lambda w: pl.BlockSpec((tm, w), lambda i: (i, 0))
    full = lambda a: pl.BlockSpec(a.shape, lambda i: (0, 0))
    nf = norm_ffn.reshape(1, d)
    return pl.pallas_call(
        _merge_kernel,
        grid=(t // tm,),
        in_specs=[row(d), row(oa.shape[1]), row(ob.shape[1]), row(mg.shape[1]),
                  full(w_a), full(w_b), full(w_o), full(nf), full(w_query)],
        out_specs=[row(d), row(d), row(nq)],
        out_shape=[jax.ShapeDtypeStruct((t, d), F32), jax.ShapeDtypeStruct((t, d), F32),
                   jax.ShapeDtypeStruct((t, nq), F32)],
        compiler_params=pltpu.CompilerParams(dimension_semantics=("arbitrary",), vmem_limit_bytes=VMEM_LIMIT),
    )(xf, oa, ob, mg, w_a, w_b, w_o, nf, w_query)


def _top_rows(work, k, payload=None):
    n_rows = work.shape[0]
    rows = lax.broadcasted_iota(jnp.int32, work.shape, 0).astype(F32)
    vals, idxs = [], []
    for _ in range(k):
        mx = jnp.max(work, axis=0, keepdims=True)
        first = jnp.min(jnp.where(work == mx, rows, float(n_rows)), axis=0, keepdims=True)
        hit = rows == first
        vals.append(mx)
        if payload is None:
            idxs.append(first)
        else:
            idxs.append(jnp.max(jnp.where(hit, payload, -1.0), axis=0, keepdims=True))
        work = jnp.where(hit, -jnp.inf, work)
    return jnp.concatenate(vals, axis=0), jnp.concatenate(idxs, axis=0)


def _peersel_kernel(qry_ref, keys_ref, eidx_ref, gate_ref):
    kk = PEER_TOPK
    for h in range(PEER_HEADS):
        tops = []
        for p in range(2):
            c0 = (h * 2 + p) * PEER_DQH
            sc = _dot_nt(keys_ref[h, p], qry_ref[:, c0:c0 + PEER_DQH], HI)
            tops.append(_top_rows(sc, kk))
        (s1, i1), (s2, i2) = tops
        cand = jnp.concatenate([s1[a:a + 1, :] + s2 for a in range(kk)], axis=0)
        cidx = jnp.concatenate([i1[a:a + 1, :] * float(PEER_NKEYS) + i2 for a in range(kk)], axis=0)
        top, eidx = _top_rows(cand, kk, payload=cidx)
        e = jnp.exp(top - top[0:1, :])
        gate = e * (1.0 / jnp.sum(e, axis=0, keepdims=True))
        eidx_ref[h * kk:(h + 1) * kk, :] = eidx.astype(jnp.int32)
        gate_ref[h * kk:(h + 1) * kk, :] = gate


def _peer_select(qry, sub_keys, tt=128):
    t, nq = qry.shape
    npair = PEER_HEADS * PEER_TOPK
    return pl.pallas_call(
        _peersel_kernel,
        grid=(t // tt,),
        in_specs=[pl.BlockSpec((tt, nq), lambda i: (i, 0)),
                  pl.BlockSpec(sub_keys.shape, lambda i: (0, 0, 0, 0))],
        out_specs=[pl.BlockSpec((npair, tt), lambda i: (0, i)), pl.BlockSpec((npair, tt), lambda i: (0, i))],
        out_shape=[jax.ShapeDtypeStruct((npair, t), jnp.int32), jax.ShapeDtypeStruct((npair, t), F32)],
        compiler_params=pltpu.CompilerParams(dimension_semantics=("arbitrary",), vmem_limit_bytes=VMEM_LIMIT),
    )(qry, sub_keys)


def _gelu_tanh(x):
    return 0.5 * x * (1.0 + jnp.tanh(math.sqrt(2.0 / math.pi) * (x + 0.044715 * (x * x * x))))


PEER_ROW_SUB = 4
PEER_TILE_STRIDE = PEER_HEADS * PEER_TOPK + 1


def _pack_half(table):
    n, d = table.shape
    bits = lax.bitcast_convert_type(table.astype(BF16), jnp.uint16).astype(jnp.uint32)
    words = bits[:, :d // 2] | (bits[:, d // 2:] << 16)
    return words.reshape(n * PEER_ROW_SUB, d // 2 // PEER_ROW_SUB)


def _load_table_once(tab_hbm, tab, sem):
    @pl.when(pl.program_id(0) == 0)
    def _():
        cp = pltpu.make_async_copy(tab_hbm, tab, sem.at[0])
        cp.start()
        cp.wait()


def _gather_rows(idx_ref, tab, tile, tt, per):
    for k in range(per):
        e4 = pl.multiple_of(idx_ref[0, 0, tt * per + k], PEER_ROW_SUB)
        tile[tt, pl.ds(k, PEER_ROW_SUB, stride=PEER_TILE_STRIDE), :] = tab[pl.ds(e4, PEER_ROW_SUB), :]
    xs = [tile[tt, pl.ds(j * PEER_TILE_STRIDE, per), :] for j in range(PEER_ROW_SUB)]
    lo = jnp.concatenate([lax.bitcast_convert_type(x << 16, F32) for x in xs], axis=1).astype(BF16)
    hi = jnp.concatenate([lax.bitcast_convert_type(x & jnp.uint32(0xFFFF0000), F32) for x in xs], axis=1)
    return lo, hi.astype(BF16)


def _peer_act_kernel(idx_ref, h_ref, gate_ref, tab_hbm, w_ref, tab, tile, sem):
    _load_table_once(tab_hbm, tab, sem)
    tok, per = gate_ref.shape
    hb = h_ref[...].astype(BF16)
    half = hb.shape[1] // 2
    gate = gate_ref[...]
    trow = lax.broadcasted_iota(jnp.int32, (tok, per), 0)
    w = jnp.zeros((tok, per), F32)
    for tt in range(tok):
        u_lo, u_hi = _gather_rows(idx_ref, tab, tile, tt, per)
        act = _dot_nt(hb[:, :half], u_lo) + _dot_nt(hb[:, half:], u_hi)
        w = w + jnp.where(trow == tt, gate * _gelu_tanh(act), 0.0)
    w_ref[...] = w


def _peer_out_kernel(idx_ref, w_ref, x1_ref, nw_ref, tab_hbm, o_ref, tab, tile, sem):
    _load_table_once(tab_hbm, tab, sem)
    tok, per = w_ref.shape
    w = w_ref[...]
    trow = lax.broadcasted_iota(jnp.int32, (tok, per), 0)
    half = x1_ref.shape[1] // 2
    y_lo = jnp.zeros((tok, half), F32)
    y_hi = jnp.zeros((tok, half), F32)
    for tt in range(tok):
        v_lo, v_hi = _gather_rows(idx_ref, tab, tile, tt, per)
        wt = jnp.where(trow == tt, w, 0.0).astype(BF16)
        y_lo = y_lo + _dot(wt, v_lo)
        y_hi = y_hi + _dot(wt, v_hi)
    x2 = x1_ref[...] + jnp.concatenate([y_lo, y_hi], axis=1)
    o_ref[...] = x2 * lax.rsqrt(jnp.mean(x2 * x2, axis=-1, keepdims=True) + RMS_EPS) * nw_ref[...]


def _peer_ffn(eidx_tok, gate_tok, h2, x1, norm_final, expert_u, expert_v):
    t, d = h2.shape
    tok = PEER_TOK
    per = eidx_tok.shape[1]
    n = t // tok
    idx = (eidx_tok * PEER_ROW_SUB).reshape(n, 1, tok * per)
    tab_u = _pack_half(expert_u)
    tab_v = _pack_half(expert_v)
    idx_spec = pl.BlockSpec((1, 1, tok * per), lambda i: (i, 0, 0), memory_space=pltpu.SMEM)
    row = lambda wd: pl.BlockSpec((tok, wd), lambda i: (i, 0))
    scratch = [pltpu.VMEM(tab_u.shape, jnp.uint32),
               pltpu.VMEM((tok, PEER_ROW_SUB * PEER_TILE_STRIDE, 128), jnp.uint32),
               pltpu.SemaphoreType.DMA((1,))]
    params = pltpu.CompilerParams(dimension_semantics=("arbitrary",), vmem_limit_bytes=VMEM_LIMIT)
    w = pl.pallas_call(
        _peer_act_kernel,
        grid=(n,),
        in_specs=[idx_spec, row(d), row(per), pl.BlockSpec(memory_space=pl.ANY)],
        out_specs=row(per),
        out_shape=jax.ShapeDtypeStruct((t, per), F32),
        scratch_shapes=scratch,
        compiler_params=params,
    )(idx, h2, gate_tok, tab_u)
    return pl.pallas_call(
        _peer_out_kernel,
        grid=(n,),
        in_specs=[idx_spec, row(per), row(d), pl.BlockSpec((1, d), lambda i: (0, 0)),
                  pl.BlockSpec(memory_space=pl.ANY)],
        out_specs=row(d),
        out_shape=jax.ShapeDtypeStruct((t, d), F32),
        scratch_shapes=scratch,
        compiler_params=params,
    )(idx, w, x1, norm_final.reshape(1, d), tab_v)


def _layer(x, rel_bias, norm_mix, w_in, conv_w, a_log, dt_bias, gdn_norm, cmp_pe_k, cmp_pe_v, cmp_w_k, cmp_w_v,
           w_branch_a, w_branch_b, w_out, norm_ffn, w_query, sub_keys, expert_u, expert_v, norm_out):
    b, s, d = x.shape
    t = b * s
    xf = x.reshape(t, d)
    hk = GDN_HEADS * GDN_DK
    w_qkv = 2 * hk + GDN_HEADS * GDN_DV
    w_z = GDN_HEADS * GDN_DV
    w_nq = NSA_HEADS * NSA_DH
    w_nkv = NSA_BRANCHES * 2 * NSA_KV_GROUPS * NSA_DH
    w_gate = NSA_HEADS * NSA_BRANCHES
    w_small = 2 * GDN_HEADS + w_gate
    pad = (-w_small) % 128
    c0 = w_qkv + w_z
    c1 = c0 + 2 * GDN_HEADS
    c2 = c1 + w_nq
    c3 = c2 + w_nkv
    c4 = c3 + w_gate
    w_cat = jnp.concatenate([w_in[:, :c0], w_in[:, c1:c3], w_in[:, c4:], w_in[:, c0:c1], w_in[:, c3:c4],
                             jnp.zeros((d, pad), w_in.dtype)], axis=1).astype(BF16)
    widths = (w_qkv, w_z, w_nq, w_nkv, 2 * d, w_small + pad)
    qkv, z, nq, nkv, mg, small = _proj(xf, norm_mix, w_cat, widths)

    oa = _gdn(qkv, z, small, conv_w, a_log, dt_bias, gdn_norm, b, s)

    kv = nkv.reshape(b, s, NSA_BRANCHES, 2, NSA_KV_GROUPS, NSA_DH).transpose(2, 3, 0, 4, 1, 5)
    kcmp, vcmp = _compress(kv[0, 0], kv[0, 1], cmp_pe_k, cmp_pe_v, cmp_w_k, cmp_w_v)
    gates = small[:, 2 * GDN_HEADS:2 * GDN_HEADS + w_gate]
    ob = _nsa(nq, gates, kcmp, vcmp, kv[1, 0], kv[1, 1], kv[2, 0], kv[2, 1], rel_bias, b, s)

    x1, h2, qry = _merge(xf, oa, ob, mg, w_branch_a.astype(BF16), w_branch_b.astype(BF16), w_out.astype(BF16),
                         norm_ffn, w_query.astype(BF16))
    eidx, gate = _peer_select(qry, sub_keys)
    out = _peer_ffn(eidx.T, gate.T, h2, x1, norm_out, expert_u, expert_v)
    return out.reshape(b, s, d)


def kernel(x, rel_bias, norm_final, norm_mix, w_in, conv_w, a_log, dt_bias, gdn_norm, cmp_pe_k, cmp_pe_v, cmp_w_k,
           cmp_w_v, w_branch_a, w_branch_b, w_out, norm_ffn, w_query, sub_keys, expert_u, expert_v):
    assert norm_mix.shape[0] == 1, "single-layer block"
    return _layer(x, rel_bias, norm_mix[0], w_in[0], conv_w[0], a_log[0], dt_bias[0], gdn_norm[0], cmp_pe_k[0],
                  cmp_pe_v[0], cmp_w_k[0], cmp_w_v[0], w_branch_a[0], w_branch_b[0], w_out[0], norm_ffn[0],
                  w_query[0], sub_keys[0], expert_u[0], expert_v[0], norm_final)
```

```python
import functools
import math

import numpy as np
import jax
import jax.numpy as jnp
from jax import lax
from jax.experimental import pallas as pl
from jax.experimental.pallas import tpu as pltpu

F32 = jnp.float32
BF16 = jnp.bfloat16
HI = lax.Precision.HIGHEST

RMS_EPS = 1e-6
NEG = -1e30

GDN_HEADS = 8
GDN_DK = 64
GDN_DV = 64
GDN_CONV = 4
GDN_CHUNK = 64

NSA_HEADS = 8
NSA_KV_GROUPS = 2
NSA_HPG = NSA_HEADS // NSA_KV_GROUPS
NSA_DH = 64
NSA_BRANCHES = 3
CMP_BLOCK = 32
CMP_STRIDE = 16
SEL_BLOCK = 64
N_SELECT = 16
WINDOW = 512
FORCE_BONUS = 1e4
REL_BUCKETS = 32
REL_MAX_DIST = 128

PEER_HEADS = 8
PEER_NKEYS = 128
PEER_TOPK = 16
PEER_DQH = 128

NSA_TQ = 128
PEER_TOK = 32
VMEM_LIMIT = 56 * 1024 * 1024


def _t5_thresholds():
    d = np.arange(0, 4 * REL_MAX_DIST)
    max_exact = REL_BUCKETS // 2
    dd = np.maximum(d, 1).astype(np.float64)
    large = max_exact + (np.log(dd / max_exact) / math.log(REL_MAX_DIST / max_exact)
                         * (REL_BUCKETS - max_exact)).astype(np.int32)
    large = np.minimum(large, REL_BUCKETS - 1)
    b = np.where(d < max_exact, d, large)
    return [int(np.argmax(b >= k)) for k in range(1, REL_BUCKETS)]


T5_THETA = _t5_thresholds()


def _dot(a, b, precision=None):
    return jnp.dot(a, b, preferred_element_type=F32, precision=precision)


def _dot_nt(a, b, precision=None):
    return lax.dot_general(a, b, (((1,), (1,)), ((), ())), preferred_element_type=F32, precision=precision)


def _dot_tn(a, b, precision=None):
    return lax.dot_general(a, b, (((0,), (0,)), ((), ())), preferred_element_type=F32, precision=precision)


def _dot_split(x, w_bf16):
    hi = x.astype(BF16)
    lo = (x - hi.astype(F32)).astype(BF16)
    return _dot(hi, w_bf16) + _dot(lo, w_bf16)


def _sigmoid(x):
    return 1.0 / (1.0 + jnp.exp(-x))


def _silu(x):
    return x * _sigmoid(x)


def _softplus(x):
    return jnp.maximum(x, 0.0) + jnp.log(1.0 + jnp.exp(-jnp.abs(x)))


def _proj_kernel(x_ref, nw_ref, w_ref, *out_refs, widths):
    x = x_ref[...]
    h = x * lax.rsqrt(jnp.mean(x * x, axis=-1, keepdims=True) + RMS_EPS) * nw_ref[...]
    hb = h.astype(BF16)
    off = 0
    for o_ref, wd in zip(out_refs, widths):
        res = _dot(hb, w_ref[:, off:off + wd])
        if len(o_ref.shape) == 2:
            o_ref[...] = res
        else:
            pw = o_ref.shape[2]
            for c in range(o_ref.shape[0]):
                o_ref[c] = res[:, c * pw:(c + 1) * pw]
        off += wd


def _proj(xf, norm_w, w_cat, widths, split, tm=256):
    t, d = xf.shape
    n = w_cat.shape[1]
    out_specs, out_shape = [], []
    for k, wd in enumerate(widths):
        if k in split:
            pw = split[k]
            out_specs.append(pl.BlockSpec((wd // pw, tm, pw), lambda i: (0, i, 0)))
            out_shape.append(jax.ShapeDtypeStruct((wd // pw, t, pw), F32))
        else:
            out_specs.append(pl.BlockSpec((tm, wd), lambda i: (i, 0)))
            out_shape.append(jax.ShapeDtypeStruct((t, wd), F32))
    return pl.pallas_call(
        functools.partial(_proj_kernel, widths=widths),
        grid=(t // tm,),
        in_specs=[pl.BlockSpec((tm, d), lambda i: (i, 0)),
                  pl.BlockSpec((1, d), lambda i: (0, 0)),
                  pl.BlockSpec((d, n), lambda i: (0, 0))],
        out_specs=out_specs,
        out_shape=out_shape,
        compiler_params=pltpu.CompilerParams(dimension_semantics=("arbitrary",), vmem_limit_bytes=VMEM_LIMIT),
    )(xf, norm_w.reshape(1, d), w_cat)


def _gdn_kernel(qkv_ref, z_ref, small_ref, convw_ref, alog_ref, dtb_ref, gnorm_ref, expand_ref, blk1_ref,
                o_ref, xbuf_ref, state_ref):
    c = GDN_CHUNK
    nh = GDN_HEADS
    dk = GDN_DK
    hd = nh * dk

    @pl.when(pl.program_id(1) == 0)
    def _():
        xbuf_ref[0:8, :] = jnp.zeros((8, xbuf_ref.shape[1]), F32)
        state_ref[...] = jnp.zeros(state_ref.shape, F32)

    xbuf_ref[8:8 + c, :] = qkv_ref[...]
    acc = xbuf_ref[pl.ds(8 - (GDN_CONV - 1), c), :] * convw_ref[0:1, :]
    for j in range(1, GDN_CONV):
        acc = acc + xbuf_ref[pl.ds(8 - (GDN_CONV - 1) + j, c), :] * convw_ref[j:j + 1, :]
    xbuf_ref[0:8, :] = xbuf_ref[c:c + 8, :]
    qkv = _silu(acc)

    row = lax.broadcasted_iota(jnp.int32, (c, c), 0)
    col = lax.broadcasted_iota(jnp.int32, (c, c), 1)
    lower = row >= col
    strict = row > col
    eye = (row == col).astype(F32)
    ltri = lower.astype(F32)

    expand = expand_ref[...]
    blk1 = blk1_ref[...]
    q = qkv[:, 0:hd]
    k = qkv[:, hd:2 * hd]
    v = qkv[:, 2 * hd:3 * hd]
    q = q * lax.rsqrt(_dot_split(q * q, blk1) + 1e-6) * (dk ** -0.5)
    k = k * lax.rsqrt(_dot_split(k * k, blk1) + 1e-6)
    small = small_ref[...]
    beta = _dot(_sigmoid(small[:, 0:nh]), expand, HI)
    g_c = -jnp.exp(alog_ref[...]) * _softplus(small[:, nh:2 * nh] + dtb_ref[...])
    gc_c = _dot(ltri, g_c, HI)
    gc = _dot(gc_c, expand, HI)
    eye_h = (lax.broadcasted_iota(jnp.int32, (nh, nh), 0) == lax.broadcasted_iota(jnp.int32, (nh, nh), 1))
    gc_r = _dot_nt(eye_h.astype(F32), gc_c, HI)
    eg = jnp.exp(gc)
    g_last = gc[c - 1:c, :]
    gl = jnp.exp(g_last)
    kb = k * beta
    vb = v * beta
    kbe = kb * eg
    q_dec = q * eg
    k_dec = k * jnp.exp(g_last - gc)

    def hs(x, h):
        return x[:, h * dk:(h + 1) * dk]

    def bf(x):
        return x.astype(BF16)

    heads = range(nh)
    decay = [jnp.where(lower, jnp.exp(jnp.where(lower, hs(gc, h) - gc_r[h:h + 1, :], 0.0)), 0.0) for h in heads]
    k_b = [bf(hs(k, h)) for h in heads]
    a = [jnp.where(strict, _dot_nt(bf(hs(kb, h)), k_b[h]) * decay[h], 0.0) for h in heads]
    tinv = [eye - a[h] for h in heads]
    p_b = [bf(a[h]) for h in heads]
    p_b = [bf(_dot(p_b[h], p_b[h])) for h in heads]
    for s in range(5):
        tinv = [tinv[h] + _dot(bf(tinv[h]), p_b[h]) for h in heads]
        if s < 4:
            p_b = [bf(_dot(p_b[h], p_b[h])) for h in heads]
    t_b = [bf(tinv[h]) for h in heads]
    u = [_dot(t_b[h], bf(hs(vb, h))) for h in heads]
    w = [_dot(t_b[h], bf(hs(kbe, h))) for h in heads]
    attn = [_dot_nt(bf(hs(q, h)), k_b[h]) * decay[h] for h in heads]
    st = [state_ref[h] for h in heads]
    st_b = [bf(st[h]) for h in heads]
    v_new = [u[h] - _dot(bf(w[h]), st_b[h]) for h in heads]
    vn_b = [bf(v_new[h]) for h in heads]
    o = [_dot(bf(hs(q_dec, h)), st_b[h]) + _dot(bf(attn[h]), vn_b[h]) for h in heads]
    for h in heads:
        state_ref[h] = st[h] * hs(gl, h) + _dot_tn(bf(hs(k_dec, h)), vn_b[h])
    o_all = jnp.concatenate(o, axis=1)
    ms = _dot_split(o_all * o_all, blk1) * (1.0 / GDN_DV)
    o_ref[...] = o_all * lax.rsqrt(ms + RMS_EPS) * gnorm_ref[...] * _silu(z_ref[...])


def _gdn(qkv, z, small, conv_w, a_log, dt_bias, gdn_norm, b, s):
    c = GDN_CHUNK
    nc = s // c
    t = b * s
    hh = GDN_HEADS
    hd = hh * GDN_DK
    wq = qkv.shape[1]
    lane_head = np.arange(hd) // GDN_DK
    expand = jnp.asarray((np.arange(hh)[:, None] == lane_head[None, :]).astype(np.float32))
    blk1 = jnp.asarray((lane_head[:, None] == lane_head[None, :]).astype(np.float32)).astype(BF16)
    const = lambda a: pl.BlockSpec(a.shape, lambda i, j: (0,) * a.ndim)
    args = (conv_w, a_log.reshape(1, hh), dt_bias.reshape(1, hh), jnp.tile(gdn_norm.reshape(1, GDN_DV), (1, hh)),
            expand, blk1)
    return pl.pallas_call(
        _gdn_kernel,
        grid=(b, nc),
        in_specs=[pl.BlockSpec((c, wq), lambda i, j: (i * nc + j, 0)),
                  pl.BlockSpec((c, hh * GDN_DV), lambda i, j: (i * nc + j, 0)),
                  pl.BlockSpec((c, small.shape[1]), lambda i, j: (i * nc + j, 0))] + [const(a) for a in args],
        out_specs=pl.BlockSpec((c, hh * GDN_DV), lambda i, j: (i * nc + j, 0)),
        out_shape=jax.ShapeDtypeStruct((t, hh * GDN_DV), F32),
        scratch_shapes=[pltpu.VMEM((c + 8, wq), F32), pltpu.VMEM((hh, GDN_DK, GDN_DV), F32)],
        compiler_params=pltpu.CompilerParams(dimension_semantics=("arbitrary", "arbitrary"),
                                             vmem_limit_bytes=VMEM_LIMIT),
    )(qkv, z, small, *args)


def _cmp_kernel(kc_ref, vc_ref, pek_ref, pev_ref, wk_ref, wv_ref, ko_ref, vo_ref):
    for src, pe, w, dst in ((kc_ref, pek_ref, wk_ref, ko_ref), (vc_ref, pev_ref, wv_ref, vo_ref)):
        r = src[0, 0]
        y_lo = _dot(r + pe[0], w[0], HI)
        y_hi = _dot(r + pe[1], w[1], HI)
        n = y_hi.shape[0]
        dst[0, 0] = y_lo + pltpu.roll(y_hi, n - 1, 0)


def _compress(kc, vc, pe_k, pe_v, w_k, w_v):
    g, b, s, dh = kc.shape
    half = CMP_STRIDE * dh
    nr = s // CMP_STRIDE
    kc2 = kc.reshape(g, b, nr, half)
    vc2 = vc.reshape(g, b, nr, half)
    spec_in = pl.BlockSpec((1, 1, nr, half), lambda i, j: (j, i, 0, 0))
    spec_pe = pl.BlockSpec((2, 1, half), lambda i, j: (0, 0, 0))
    spec_w = pl.BlockSpec((2, half, dh), lambda i, j: (0, 0, 0))
    spec_o = pl.BlockSpec((1, 1, nr, dh), lambda i, j: (i, j, 0, 0))
    return pl.pallas_call(
        _cmp_kernel,
        grid=(b, g),
        in_specs=[spec_in, spec_in, spec_pe, spec_pe, spec_w, spec_w],
        out_specs=[spec_o, spec_o],
        out_shape=[jax.ShapeDtypeStruct((b, g, nr, dh), F32)] * 2,
        compiler_params=pltpu.CompilerParams(dimension_semantics=("arbitrary", "arbitrary"),
                                             vmem_limit_bytes=VMEM_LIMIT),
    )(kc2, vc2, pe_k.reshape(2, 1, half), pe_v.reshape(2, 1, half),
      w_k.reshape(2, half, dh), w_v.reshape(2, half, dh))


def _bias_chain(dist, rb_ref, heads):
    accs = [jnp.full(dist.shape, rb_ref[0, hd], F32) for hd in heads]
    for k in range(1, REL_BUCKETS):
        m = dist >= T5_THETA[k - 1]
        accs = [jnp.where(m, rb_ref[k, hd], a) for a, hd in zip(accs, heads)]
    return accs


CMP_BAND_LO = (CMP_BLOCK - 1 + REL_MAX_DIST - 1) // CMP_STRIDE
CMP_BAND_HI = (NSA_TQ - 1 - (CMP_BLOCK - 1)) // CMP_STRIDE
CMP_BAND = CMP_BAND_LO + CMP_BAND_HI + 1


def _nsa_kernel(rb_ref, q_ref, small_ref, kcmp_ref, vcmp_ref, ks_ref, vs_ref, kw_ref, vw_ref, ovlt_ref, o_ref,
                btab_ref, cpatch_ref, mrun_ref, lrun_ref, acc_ref, *, group_axis):
    tq = NSA_TQ
    dh = NSA_DH
    hpg = NSA_HPG
    scale = dh ** -0.5
    qi = pl.program_id(2)
    t0 = qi * tq
    ncmp = kcmp_ref.shape[2]
    nsel = ovlt_ref.shape[0]
    blk_per_tile = tq // SEL_BLOCK
    nwin = WINDOW // tq

    ri = lax.broadcasted_iota(jnp.int32, (tq, tq), 0)
    ci = lax.broadcasted_iota(jnp.int32, (tq, tq), 1)
    dloc = ri - ci

    gsel = pl.program_id(group_axis)
    heads = [gsel * hpg + hh for hh in range(hpg)]

    def far_bias(hh):
        return rb_ref[REL_BUCKETS - 1, heads[hh]]

    @pl.when(qi == 0)
    def _():
        for dl in range(2):
            tabs = _bias_chain(dloc + dl * tq, rb_ref, heads)
            for hh in range(hpg):
                btab_ref[hh, dl] = tabs[hh]
        tabs = _bias_chain(ri - CMP_STRIDE * (ci - CMP_BAND_LO) - (CMP_BLOCK - 1), rb_ref, heads)
        for hh in range(hpg):
            cpatch_ref[hh] = tabs[hh] - far_bias(hh)

    q_all = q_ref[...] * scale
    qs = [q_all[:, hh * dh:(hh + 1) * dh] for hh in range(hpg)]
    qstk = jnp.concatenate(qs, axis=0).astype(BF16)
    gates_all = _sigmoid(small_ref[...])
    goff = 2 * GDN_HEADS

    def gate_col(hh, br):
        cols = [gates_all[:, goff + ((g * hpg + hh) * NSA_BRANCHES + br):goff + ((g * hpg + hh) * NSA_BRANCHES + br) + 1]
                for g in range(NSA_KV_GROUPS)]
        out = cols[0]
        for g in range(1, NSA_KV_GROUPS):
            out = jnp.where(gsel == g, cols[g], out)
        return out

    trow = t0 + lax.broadcasted_iota(jnp.int32, (tq, ncmp), 0)
    ncol = lax.broadcasted_iota(jnp.int32, (tq, ncmp), 1)
    mask_c = trow - (ncol * CMP_STRIDE + CMP_BLOCK - 1) >= 0
    pr = lax.broadcasted_iota(jnp.int32, (tq, ncmp), 0)
    place = jnp.where((ncol == t0 // CMP_STRIDE - CMP_BAND_LO + pr) & (pr < CMP_BAND), 1.0, 0.0).astype(BF16)
    kcmp = kcmp_ref[0, 0]
    vcmp = vcmp_ref[0, 0].astype(BF16)
    o_c = []
    psum = jnp.zeros((tq, ncmp), F32)
    for hh in range(hpg):
        bias = far_bias(hh) + _dot_split(cpatch_ref[hh], place)
        s = jnp.where(mask_c, _dot_nt(qs[hh], kcmp, HI) + bias, NEG)
        mx = jnp.max(s, axis=-1, keepdims=True)
        e = jnp.where(mask_c, jnp.exp(s - mx), 0.0)
        l = jnp.sum(e, axis=-1, keepdims=True)
        p = e * (1.0 / jnp.where(l > 0.0, l, 1.0))
        o_c.append(_dot(p.astype(BF16), vcmp))
        psum = psum + p
    imp_t = _dot_nt(ovlt_ref[...], psum, HI)

    blk_t = lax.broadcasted_iota(jnp.int32, (nsel, tq), 0)
    cur_t = (t0 + lax.broadcasted_iota(jnp.int32, (nsel, tq), 1)) // SEL_BLOCK
    valid_t = blk_t <= cur_t
    forced_t = (blk_t == 0) | (blk_t == cur_t) | (blk_t == cur_t - 1)
    work = jnp.where(valid_t, imp_t + jnp.where(forced_t, FORCE_BONUS, 0.0), -1.0)
    rank = jnp.zeros((nsel, tq), F32)
    for i in range(nsel):
        xi = work[i:i + 1, :]
        rank = rank + jnp.where(blk_t > i, jnp.where(xi >= work, 1.0, 0.0), jnp.where(xi > work, 1.0, 0.0))
    sel_t = jnp.where((rank < float(min(N_SELECT, nsel))) & valid_t, 1.0, 0.0)

    mrun_ref[...] = jnp.full(mrun_ref.shape, NEG, F32)
    lrun_ref[...] = jnp.zeros(lrun_ref.shape, F32)
    acc_ref[...] = jnp.zeros(acc_ref.shape, F32)

    def scores(k_ref, j, dls, mask):
        width = len(dls) * tq
        start = pl.multiple_of(j * tq, tq)
        kt = k_ref[0, 0, pl.ds(start, width), :].astype(BF16)
        s = _dot_nt(qstk, kt)
        out = []
        for hh in range(hpg):
            parts = []
            for cblk, dl in enumerate(dls):
                bias = far_bias(hh) if (dl is None or dl >= 2) else btab_ref[hh, dl]
                parts.append(s[hh * tq:(hh + 1) * tq, cblk * tq:(cblk + 1) * tq] + bias)
            sh = parts[0] if len(parts) == 1 else jnp.concatenate(parts, axis=1)
            out.append(jnp.where(mask, sh, NEG))
        return out

    def fold(x, op):
        r = x[:, 0:tq]
        for cblk in range(1, x.shape[1] // tq):
            r = op(r, x[:, cblk * tq:(cblk + 1) * tq])
        return r

    def pass1(br, k_ref, v_ref, j, dls, mask):
        for hh, s in enumerate(scores(k_ref, j, dls, mask)):
            mrun_ref[br, hh] = jnp.maximum(mrun_ref[br, hh], fold(s, jnp.maximum))

    def pass2(br, k_ref, v_ref, j, dls, mask):
        start = pl.multiple_of(j * tq, tq)
        vt = v_ref[0, 0, pl.ds(start, len(dls) * tq), :].astype(BF16)
        ps = []
        for hh, s in enumerate(scores(k_ref, j, dls, mask)):
            m = mrun_ref[br, hh]
            p = jnp.exp(s - jnp.concatenate([m] * len(dls), axis=1))
            lrun_ref[br, hh] = lrun_ref[br, hh] + fold(p, jnp.add)
            ps.append(p.astype(BF16))
        acc_ref[br] = acc_ref[br] + _dot(jnp.concatenate(ps, axis=0), vt)

    def span_dist(dls):
        n = len(dls)
        r = lax.broadcasted_iota(jnp.int32, (tq, n * tq), 0)
        cidx = lax.broadcasted_iota(jnp.int32, (tq, n * tq), 1)
        return r - cidx + (dls[-1] + n - 1) * tq

    def sel_mask(j, dls):
        width = len(dls) * tq
        eb = lax.broadcasted_iota(jnp.int32, (nsel, width), 0)
        ek = lax.broadcasted_iota(jnp.int32, (nsel, width), 1) // SEL_BLOCK
        expand = jnp.where(eb == ek + j * blk_per_tile, 1.0, 0.0)
        mask = _dot_tn(sel_t, expand) > 0.5
        if dls[-1] == 0:
            mask = mask & (span_dist(dls) >= 0)
        return mask

    def win_mask(dls):
        dist = span_dist(dls)
        return (dist >= 0) & (dist < WINDOW)

    far_dls = [None] * 4
    win_dls = list(range(nwin, -1, -1))

    def sweep(fn):
        nfar = jnp.maximum(qi - 1, 0)
        nspan = nfar // len(far_dls)

        def span_body(c, carry):
            fn(0, ks_ref, vs_ref, c * len(far_dls), far_dls, sel_mask(c * len(far_dls), far_dls))
            return carry
        lax.fori_loop(0, nspan, span_body, 0)

        def far_body(j, carry):
            fn(0, ks_ref, vs_ref, j, [None], sel_mask(j, [None]))
            return carry
        lax.fori_loop(nspan * len(far_dls), nfar, far_body, 0)

        @pl.when(qi >= 1)
        def _():
            fn(0, ks_ref, vs_ref, qi - 1, [1, 0], sel_mask(qi - 1, [1, 0]))

        @pl.when(qi == 0)
        def _():
            fn(0, ks_ref, vs_ref, qi, [0], sel_mask(qi, [0]))

        @pl.when(qi >= nwin)
        def _():
            fn(1, kw_ref, vw_ref, qi - nwin, win_dls, win_mask(win_dls))

        for dl in range(nwin - 1, -1, -1):
            @pl.when((qi >= dl) & (qi < nwin))
            def _():
                fn(1, kw_ref, vw_ref, qi - dl, [dl], win_mask([dl]))

    sweep(pass1)
    for br in range(2):
        for hh in range(hpg):
            mrun_ref[br, hh] = jnp.broadcast_to(jnp.max(mrun_ref[br, hh], axis=-1, keepdims=True), (tq, tq))
    sweep(pass2)

    for hh in range(hpg):
        o_b = []
        for br in range(2):
            l = jnp.sum(lrun_ref[br, hh], axis=-1, keepdims=True)
            o_b.append(acc_ref[br, hh * tq:(hh + 1) * tq, :] * (1.0 / l))
        out = gate_col(hh, 0) * o_c[hh] + gate_col(hh, 1) * o_b[0] + gate_col(hh, 2) * o_b[1]
        o_ref[:, hh * dh:(hh + 1) * dh] = out


def _overlap_matrix_t(s):
    n_rows = s // CMP_STRIDE
    n_cmp = (s - CMP_BLOCK) // CMP_STRIDE + 1
    n_sel = s // SEL_BLOCK
    cmp_start = np.arange(n_rows) * CMP_STRIDE
    cmp_end = cmp_start + CMP_BLOCK - 1
    sel_start = np.arange(n_sel) * SEL_BLOCK
    ov = np.clip(np.minimum(cmp_end[:, None] + 1, sel_start[None, :] + SEL_BLOCK)
                 - np.maximum(cmp_start[:, None], sel_start[None, :]), 0, None).astype(np.float32) / CMP_BLOCK
    ov[n_cmp:] = 0.0
    return jnp.asarray(ov.T)


def _nsa(q, small, kcmp, vcmp, ks, vs, kw, vw, rel_bias, b, s):
    t = b * s
    tq = NSA_TQ
    nq = s // tq
    g = NSA_KV_GROUPS
    dh = NSA_DH
    gw = NSA_HPG * dh
    ncmp = kcmp.shape[2]
    nsel = s // SEL_BLOCK
    ovlt = _overlap_matrix_t(s)
    seq_spec = pl.BlockSpec((1, 1, s, dh), lambda i, j, k: (j, i, 0, 0))
    cmp_spec = pl.BlockSpec((1, 1, ncmp, dh), lambda i, j, k: (i, j, 0, 0))
    return pl.pallas_call(
        functools.partial(_nsa_kernel, group_axis=1),
        grid=(b, g, nq),
        in_specs=[pl.BlockSpec(memory_space=pltpu.SMEM),
                  pl.BlockSpec((tq, gw), lambda i, j, k: (i * nq + k, j)),
                  pl.BlockSpec((tq, small.shape[1]), lambda i, j, k: (i * nq + k, 0)),
                  cmp_spec, cmp_spec, seq_spec, seq_spec, seq_spec, seq_spec,
                  pl.BlockSpec((nsel, ncmp), lambda i, j, k: (0, 0))],
        out_specs=pl.BlockSpec((tq, gw), lambda i, j, k: (i * nq + k, j)),
        out_shape=jax.ShapeDtypeStruct((t, g * gw), F32),
        scratch_shapes=[pltpu.VMEM((NSA_HPG, 2, tq, tq), F32),
                        pltpu.VMEM((NSA_HPG, tq, tq), F32),
                        pltpu.VMEM((2, NSA_HPG, tq, tq), F32),
                        pltpu.VMEM((2, NSA_HPG, tq, tq), F32),
                        pltpu.VMEM((2, NSA_HPG * tq, dh), F32)],
        compiler_params=pltpu.CompilerParams(dimension_semantics=("arbitrary", "arbitrary", "arbitrary"),
                                             vmem_limit_bytes=VMEM_LIMIT),
    )(rel_bias, q, small, kcmp, vcmp, ks, vs, kw, vw, ovlt)


def _merge_kernel(x_ref, oa_ref, ob_ref, mg_ref, wa_ref, wb_ref, wo_ref, nf_ref, wq_ref, x1_ref, h2_ref, qry_ref):
    d = x_ref.shape[1]
    ya = _dot(oa_ref[...].astype(BF16), wa_ref[...])
    yb = _dot(ob_ref[...].astype(BF16), wb_ref[...])
    mg = mg_ref[...]
    mixed = _sigmoid(mg[:, 0:d]) * ya + _sigmoid(mg[:, d:2 * d]) * yb
    x1 = x_ref[...] + _dot(mixed.astype(BF16), wo_ref[...])
    x1_ref[...] = x1
    h2 = x1 * lax.rsqrt(jnp.mean(x1 * x1, axis=-1, keepdims=True) + RMS_EPS) * nf_ref[...]
    h2_ref[...] = h2
    qry_ref[...] = _dot(h2.astype(BF16), wq_ref[...])


def _merge(xf, oa, ob, mg, w_a, w_b, w_o, norm_ffn, w_query, tm=256):
    t, d = xf.shape
    nq = w_query.shape[1]
    row = lambda w: pl.BlockSpec((tm, w), lambda i: (i, 0))
    full = lambda a: pl.BlockSpec(a.shape, lambda i: (0, 0))
    nf = norm_ffn.reshape(1, d)
    return pl.pallas_call(
        _merge_kernel,
        grid=(t // tm,),
        in_specs=[row(d), row(oa.shape[1]), row(ob.shape[1]), row(mg.shape[1]),
                  full(w_a), full(w_b), full(w_o), full(nf), full(w_query)],
        out_specs=[row(d), row(d), row(nq)],
        out_shape=[jax.ShapeDtypeStruct((t, d), F32), jax.ShapeDtypeStruct((t, d), F32),
                   jax.ShapeDtypeStruct((t, nq), F32)],
        compiler_params=pltpu.CompilerParams(dimension_semantics=("arbitrary",), vmem_limit_bytes=VMEM_LIMIT),
    )(xf, oa, ob, mg, w_a, w_b, w_o, nf, w_query)


def _top_rows(work, k, payload=None):
    n_rows = work.shape[0]
    rows = lax.broadcasted_iota(jnp.int32, work.shape, 0).astype(F32)
    vals, idxs = [], []
    for _ in range(k):
        mx = jnp.max(work, axis=0, keepdims=True)
        first = jnp.min(jnp.where(work == mx, rows, float(n_rows)), axis=0, keepdims=True)
        hit = rows == first
        vals.append(mx)
        if payload is None:
            idxs.append(first)
        else:
            idxs.append(jnp.max(jnp.where(hit, payload, -1.0), axis=0, keepdims=True))
        work = jnp.where(hit, -jnp.inf, work)
    return jnp.concatenate(vals, axis=0), jnp.concatenate(idxs, axis=0)


def _peersel_kernel(qry_ref, keys_ref, eidx_ref, gate_ref):
    kk = PEER_TOPK
    for h in range(PEER_HEADS):
        tops = []
        for p in range(2):
            c0 = (h * 2 + p) * PEER_DQH
            sc = _dot_nt(keys_ref[h, p], qry_ref[:, c0:c0 + PEER_DQH], HI)
            tops.append(_top_rows(sc, kk))
        (s1, i1), (s2, i2) = tops
        cand_p, cidx_p = [], []
        a = 0
        while a < kk:
            nb = kk // (a + 1)
            if nb == 1:
                cand_p.append(s1[a:kk, :] + s2[0:1, :])
                cidx_p.append(i1[a:kk, :] * float(PEER_NKEYS) + i2[0:1, :])
                break
            nbp = -(-nb // 8) * 8
            v = s1[a:a + 1, :] + s2[0:nbp, :]
            if nbp != nb:
                v = jnp.where(lax.broadcasted_iota(jnp.int32, v.shape, 0) < nb, v, -jnp.inf)
            cand_p.append(v)
            cidx_p.append(i1[a:a + 1, :] * float(PEER_NKEYS) + i2[0:nbp, :])
            a += 1
        cand = jnp.concatenate(cand_p, axis=0)
        cidx = jnp.concatenate(cidx_p, axis=0)
        top, eidx = _top_rows(cand, kk, payload=cidx)
        e = jnp.exp(top - top[0:1, :])
        gate = e * (1.0 / jnp.sum(e, axis=0, keepdims=True))
        eidx_ref[h * kk:(h + 1) * kk, :] = eidx.astype(jnp.int32)
        gate_ref[h * kk:(h + 1) * kk, :] = gate


def _peer_select(qry, sub_keys, tt=128):
    t, nq = qry.shape
    npair = PEER_HEADS * PEER_TOPK
    return pl.pallas_call(
        _peersel_kernel,
        grid=(t // tt,),
        in_specs=[pl.BlockSpec((tt, nq), lambda i: (i, 0)),
                  pl.BlockSpec(sub_keys.shape, lambda i: (0, 0, 0, 0))],
        out_specs=[pl.BlockSpec((npair, tt), lambda i: (0, i)), pl.BlockSpec((npair, tt), lambda i: (0, i))],
        out_shape=[jax.ShapeDtypeStruct((npair, t), jnp.int32), jax.ShapeDtypeStruct((npair, t), F32)],
        compiler_params=pltpu.CompilerParams(dimension_semantics=("arbitrary",), vmem_limit_bytes=VMEM_LIMIT),
    )(qry, sub_keys)


def _gelu_tanh(x):
    return 0.5 * x * (1.0 + jnp.tanh(math.sqrt(2.0 / math.pi) * (x + 0.044715 * (x * x * x))))


PEER_ROW_SUB = 4
PEER_TILE_STRIDE = PEER_HEADS * PEER_TOPK + 1


def _pack_half(table):
    n, d = table.shape
    bits = lax.bitcast_convert_type(table.astype(BF16), jnp.uint16).astype(jnp.uint32)
    words = bits[:, :d // 2] | (bits[:, d // 2:] << 16)
    return words.reshape(n * PEER_ROW_SUB, d // 2 // PEER_ROW_SUB)


def _load_table_once(tab_hbm, tab, sem):
    @pl.when(pl.program_id(0) == 0)
    def _():
        cp = pltpu.make_async_copy(tab_hbm, tab, sem.at[0])
        cp.start()
        cp.wait()


def _gather_rows(idx_ref, tab, tile, tt, per):
    for k in range(per):
        e4 = pl.multiple_of(idx_ref[0, 0, tt * per + k], PEER_ROW_SUB)
        tile[tt, pl.ds(k, PEER_ROW_SUB, stride=PEER_TILE_STRIDE), :] = tab[pl.ds(e4, PEER_ROW_SUB), :]
    xs = [tile[tt, pl.ds(j * PEER_TILE_STRIDE, per), :] for j in range(PEER_ROW_SUB)]
    lo = jnp.concatenate([lax.bitcast_convert_type(x << 16, F32) for x in xs], axis=1).astype(BF16)
    hi = jnp.concatenate([lax.bitcast_convert_type(x & jnp.uint32(0xFFFF0000), F32) for x in xs], axis=1)
    return lo, hi.astype(BF16)


def _peer_act_kernel(idx_ref, h_ref, gate_ref, tab_hbm, w_ref, tab, tile, sem):
    _load_table_once(tab_hbm, tab, sem)
    tok, per = gate_ref.shape
    hb = h_ref[...].astype(BF16)
    half = hb.shape[1] // 2
    gate = gate_ref[...]
    trow = lax.broadcasted_iota(jnp.int32, (tok, per), 0)
    w = jnp.zeros((tok, per), F32)
    for tt in range(tok):
        u_lo, u_hi = _gather_rows(idx_ref, tab, tile, tt, per)
        act = _dot_nt(hb[:, :half], u_lo) + _dot_nt(hb[:, half:], u_hi)
        w = w + jnp.where(trow == tt, gate * _gelu_tanh(act), 0.0)
    w_ref[...] = w


def _peer_out_kernel(idx_ref, w_ref, x1_ref, nw_ref, tab_hbm, o_ref, tab, tile, sem):
    _load_table_once(tab_hbm, tab, sem)
    tok, per = w_ref.shape
    w = w_ref[...]
    trow = lax.broadcasted_iota(jnp.int32, (tok, per), 0)
    half = x1_ref.shape[1] // 2
    y_lo = jnp.zeros((tok, half), F32)
    y_hi = jnp.zeros((tok, half), F32)
    for tt in range(tok):
        v_lo, v_hi = _gather_rows(idx_ref, tab, tile, tt, per)
        wt = jnp.where(trow == tt, w, 0.0).astype(BF16)
        y_lo = y_lo + _dot(wt, v_lo)
        y_hi = y_hi + _dot(wt, v_hi)
    x2 = x1_ref[...] + jnp.concatenate([y_lo, y_hi], axis=1)
    o_ref[...] = x2 * lax.rsqrt(jnp.mean(x2 * x2, axis=-1, keepdims=True) + RMS_EPS) * nw_ref[...]


def _peer_ffn(eidx_tok, gate_tok, h2, x1, norm_final, expert_u, expert_v):
    t, d = h2.shape
    tok = PEER_TOK
    per = eidx_tok.shape[1]
    n = t // tok
    idx = (eidx_tok * PEER_ROW_SUB).reshape(n, 1, tok * per)
    tab_u = _pack_half(expert_u)
    tab_v = _pack_half(expert_v)
    idx_spec = pl.BlockSpec((1, 1, tok * per), lambda i: (i, 0, 0), memory_space=pltpu.SMEM)
    row = lambda wd: pl.BlockSpec((tok, wd), lambda i: (i, 0))
    scratch = [pltpu.VMEM(tab_u.shape, jnp.uint32),
               pltpu.VMEM((tok, PEER_ROW_SUB * PEER_TILE_STRIDE, 128), jnp.uint32),
               pltpu.SemaphoreType.DMA((1,))]
    params = pltpu.CompilerParams(dimension_semantics=("arbitrary",), vmem_limit_bytes=VMEM_LIMIT)
    w = pl.pallas_call(
        _peer_act_kernel,
        grid=(n,),
        in_specs=[idx_spec, row(d), row(per), pl.BlockSpec(memory_space=pl.ANY)],
        out_specs=row(per),
        out_shape=jax.ShapeDtypeStruct((t, per), F32),
        scratch_shapes=scratch,
        compiler_params=params,
    )(idx, h2, gate_tok, tab_u)
    return pl.pallas_call(
        _peer_out_kernel,
        grid=(n,),
        in_specs=[idx_spec, row(per), row(d), pl.BlockSpec((1, d), lambda i: (0, 0)),
                  pl.BlockSpec(memory_space=pl.ANY)],
        out_specs=row(d),
        out_shape=jax.ShapeDtypeStruct((t, d), F32),
        scratch_shapes=scratch,
        compiler_params=params,
    )(idx, w, x1, norm_final.reshape(1, d), tab_v)


def _layer(x, rel_bias, norm_mix, w_in, conv_w, a_log, dt_bias, gdn_norm, cmp_pe_k, cmp_pe_v, cmp_w_k, cmp_w_v,
           w_branch_a, w_branch_b, w_out, norm_ffn, w_query, sub_keys, expert_u, expert_v, norm_out):
    b, s, d = x.shape
    t = b * s
    xf = x.reshape(t, d)
    hk = GDN_HEADS * GDN_DK
    w_qkv = 2 * hk + GDN_HEADS * GDN_DV
    w_z = GDN_HEADS * GDN_DV
    w_nq = NSA_HEADS * NSA_DH
    w_nkv = NSA_BRANCHES * 2 * NSA_KV_GROUPS * NSA_DH
    w_gate = NSA_HEADS * NSA_BRANCHES
    w_small = 2 * GDN_HEADS + w_gate
    pad = (-w_small) % 128
    c0 = w_qkv + w_z
    c1 = c0 + 2 * GDN_HEADS
    c2 = c1 + w_nq
    c3 = c2 + w_nkv
    c4 = c3 + w_gate
    w_cat = jnp.concatenate([w_in[:, :c0], w_in[:, c1:c3], w_in[:, c4:], w_in[:, c0:c1], w_in[:, c3:c4],
                             jnp.zeros((d, pad), w_in.dtype)], axis=1).astype(BF16)
    widths = (w_qkv, w_z, w_nq, w_nkv, 2 * d, w_small + pad)
    qkv, z, nq, kvg, mg, small = _proj(xf, norm_mix, w_cat, widths, {3: NSA_DH})

    oa = _gdn(qkv, z, small, conv_w, a_log, dt_bias, gdn_norm, b, s)

    kv = kvg.reshape(NSA_BRANCHES, 2, NSA_KV_GROUPS, b, s, NSA_DH)
    kcmp, vcmp = _compress(kv[0, 0], kv[0, 1], cmp_pe_k, cmp_pe_v, cmp_w_k, cmp_w_v)
    ob = _nsa(nq, small, kcmp, vcmp, kv[1, 0], kv[1, 1], kv[2, 0], kv[2, 1], rel_bias, b, s)

    x1, h2, qry = _merge(xf, oa, ob, mg, w_branch_a.astype(BF16), w_branch_b.astype(BF16), w_out.astype(BF16),
                         norm_ffn, w_query.astype(BF16))
    eidx, gate = _peer_select(qry, sub_keys)
    out = _peer_ffn(eidx.T, gate.T, h2, x1, norm_out, expert_u, expert_v)
    return out.reshape(b, s, d)


def kernel(x, rel_bias, norm_final, norm_mix, w_in, conv_w, a_log, dt_bias, gdn_norm, cmp_pe_k, cmp_pe_v, cmp_w_k,
           cmp_w_v, w_branch_a, w_branch_b, w_out, norm_ffn, w_query, sub_keys, expert_u, expert_v):
    assert norm_mix.shape[0] == 1, "single-layer block"
    return _layer(x, rel_bias, norm_mix[0], w_in[0], conv_w[0], a_log[0], dt_bias[0], gdn_norm[0], cmp_pe_k[0],
                  cmp_pe_v[0], cmp_w_k[0], cmp_w_v[0], w_branch_a[0], w_branch_b[0], w_out[0], norm_ffn[0],
                  w_query[0], sub_keys[0], expert_u[0], expert_v[0], norm_final)
```

```python
import functools
import math

import numpy as np
import jax
import jax.numpy as jnp
from jax import lax
from jax.experimental import pallas as pl
from jax.experimental.pallas import tpu as pltpu

F32 = jnp.float32
BF16 = jnp.bfloat16
HI = lax.Precision.HIGHEST

RMS_EPS = 1e-6
NEG = -1e30

GDN_HEADS = 8
GDN_DK = 64
GDN_DV = 64
GDN_CONV = 4
GDN_CHUNK = 64

NSA_HEADS = 8
NSA_KV_GROUPS = 2
NSA_HPG = NSA_HEADS // NSA_KV_GROUPS
NSA_DH = 64
NSA_BRANCHES = 3
CMP_BLOCK = 32
CMP_STRIDE = 16
SEL_BLOCK = 64
N_SELECT = 16
WINDOW = 512
FORCE_BONUS = 1e4
REL_BUCKETS = 32
REL_MAX_DIST = 128

PEER_HEADS = 8
PEER_NKEYS = 128
PEER_TOPK = 16
PEER_DQH = 128

NSA_TQ = 128
NSA_FAR_TILES = 4
PEER_TOK = 32
VMEM_LIMIT = 56 * 1024 * 1024


def _t5_thresholds():
    d = np.arange(0, 4 * REL_MAX_DIST)
    max_exact = REL_BUCKETS // 2
    dd = np.maximum(d, 1).astype(np.float64)
    large = max_exact + (np.log(dd / max_exact) / math.log(REL_MAX_DIST / max_exact)
                         * (REL_BUCKETS - max_exact)).astype(np.int32)
    large = np.minimum(large, REL_BUCKETS - 1)
    b = np.where(d < max_exact, d, large)
    return [int(np.argmax(b >= k)) for k in range(1, REL_BUCKETS)]


T5_THETA = _t5_thresholds()


def _dot(a, b, precision=None):
    return jnp.dot(a, b, preferred_element_type=F32, precision=precision)


def _dot_nt(a, b, precision=None):
    return lax.dot_general(a, b, (((1,), (1,)), ((), ())), preferred_element_type=F32, precision=precision)


def _dot_tn(a, b, precision=None):
    return lax.dot_general(a, b, (((0,), (0,)), ((), ())), preferred_element_type=F32, precision=precision)


def _dot_split(x, w_bf16):
    hi = x.astype(BF16)
    lo = (x - hi.astype(F32)).astype(BF16)
    return _dot(hi, w_bf16) + _dot(lo, w_bf16)


def _sigmoid(x):
    return 1.0 / (1.0 + jnp.exp(-x))


def _silu(x):
    return x * _sigmoid(x)


def _softplus(x):
    return jnp.maximum(x, 0.0) + jnp.log(1.0 + jnp.exp(-jnp.abs(x)))


def _proj_kernel(x_ref, nw_ref, w_ref, *out_refs, widths):
    x = x_ref[...]
    h = x * lax.rsqrt(jnp.mean(x * x, axis=-1, keepdims=True) + RMS_EPS) * nw_ref[...]
    hb = h.astype(BF16)
    off = 0
    for o_ref, wd in zip(out_refs, widths):
        res = _dot(hb, w_ref[:, off:off + wd])
        if len(o_ref.shape) == 2:
            o_ref[...] = res
        else:
            pw = o_ref.shape[2]
            for c in range(o_ref.shape[0]):
                o_ref[c] = res[:, c * pw:(c + 1) * pw]
        off += wd


def _proj(xf, norm_w, w_cat, widths, split, tm=256):
    t, d = xf.shape
    n = w_cat.shape[1]
    out_specs, out_shape = [], []
    for k, wd in enumerate(widths):
        if k in split:
            pw = split[k]
            out_specs.append(pl.BlockSpec((wd // pw, tm, pw), lambda i: (0, i, 0)))
            out_shape.append(jax.ShapeDtypeStruct((wd // pw, t, pw), F32))
        else:
            out_specs.append(pl.BlockSpec((tm, wd), lambda i: (i, 0)))
            out_shape.append(jax.ShapeDtypeStruct((t, wd), F32))
    return pl.pallas_call(
        functools.partial(_proj_kernel, widths=widths),
        grid=(t // tm,),
        in_specs=[pl.BlockSpec((tm, d), lambda i: (i, 0)),
                  pl.BlockSpec((1, d), lambda i: (0, 0)),
                  pl.BlockSpec((d, n), lambda i: (0, 0))],
        out_specs=out_specs,
        out_shape=out_shape,
        compiler_params=pltpu.CompilerParams(dimension_semantics=("arbitrary",), vmem_limit_bytes=VMEM_LIMIT),
    )(xf, norm_w.reshape(1, d), w_cat)


def _gdn_kernel(qkv_ref, z_ref, small_ref, convw_ref, alog_ref, dtb_ref, gnorm_ref, expand_ref, blk1_ref,
                o_ref, xbuf_ref, state_ref):
    c = GDN_CHUNK
    nh = GDN_HEADS
    dk = GDN_DK
    hd = nh * dk

    @pl.when(pl.program_id(1) == 0)
    def _():
        xbuf_ref[0:8, :] = jnp.zeros((8, xbuf_ref.shape[1]), F32)
        state_ref[...] = jnp.zeros(state_ref.shape, F32)

    xbuf_ref[8:8 + c, :] = qkv_ref[...]
    acc = xbuf_ref[pl.ds(8 - (GDN_CONV - 1), c), :] * convw_ref[0:1, :]
    for j in range(1, GDN_CONV):
        acc = acc + xbuf_ref[pl.ds(8 - (GDN_CONV - 1) + j, c), :] * convw_ref[j:j + 1, :]
    xbuf_ref[0:8, :] = xbuf_ref[c:c + 8, :]
    qkv = _silu(acc)

    row = lax.broadcasted_iota(jnp.int32, (c, c), 0)
    col = lax.broadcasted_iota(jnp.int32, (c, c), 1)
    lower = row >= col
    strict = row > col
    eye = (row == col).astype(F32)
    ltri = lower.astype(F32)

    expand = expand_ref[...]
    blk1 = blk1_ref[...]
    q = qkv[:, 0:hd]
    k = qkv[:, hd:2 * hd]
    v = qkv[:, 2 * hd:3 * hd]
    q = q * lax.rsqrt(_dot_split(q * q, blk1) + 1e-6) * (dk ** -0.5)
    k = k * lax.rsqrt(_dot_split(k * k, blk1) + 1e-6)
    small = small_ref[...]
    beta = _dot(_sigmoid(small[:, 0:nh]), expand, HI)
    g_c = -jnp.exp(alog_ref[...]) * _softplus(small[:, nh:2 * nh] + dtb_ref[...])
    gc_c = _dot(ltri, g_c, HI)
    gc = _dot(gc_c, expand, HI)
    eye_h = (lax.broadcasted_iota(jnp.int32, (nh, nh), 0) == lax.broadcasted_iota(jnp.int32, (nh, nh), 1))
    gc_r = _dot_nt(eye_h.astype(F32), gc_c, HI)
    eg = jnp.exp(gc)
    g_last = gc[c - 1:c, :]
    gl = jnp.exp(g_last)
    kb = k * beta
    vb = v * beta
    kbe = kb * eg
    q_dec = q * eg
    k_dec = k * jnp.exp(g_last - gc)

    def hs(x, h):
        return x[:, h * dk:(h + 1) * dk]

    def bf(x):
        return x.astype(BF16)

    heads = range(nh)
    decay = [jnp.where(lower, jnp.exp(jnp.where(lower, hs(gc, h) - gc_r[h:h + 1, :], 0.0)), 0.0) for h in heads]
    k_b = [bf(hs(k, h)) for h in heads]
    a = [jnp.where(strict, _dot_nt(bf(hs(kb, h)), k_b[h]) * decay[h], 0.0) for h in heads]
    tinv = [eye - a[h] for h in heads]
    p_b = [bf(a[h]) for h in heads]
    p_b = [bf(_dot(p_b[h], p_b[h])) for h in heads]
    for s in range(5):
        tinv = [tinv[h] + _dot(bf(tinv[h]), p_b[h]) for h in heads]
        if s < 4:
            p_b = [bf(_dot(p_b[h], p_b[h])) for h in heads]
    t_b = [bf(tinv[h]) for h in heads]
    u = [_dot(t_b[h], bf(hs(vb, h))) for h in heads]
    w = [_dot(t_b[h], bf(hs(kbe, h))) for h in heads]
    attn = [_dot_nt(bf(hs(q, h)), k_b[h]) * decay[h] for h in heads]
    st = [state_ref[h] for h in heads]
    st_b = [bf(st[h]) for h in heads]
    v_new = [u[h] - _dot(bf(w[h]), st_b[h]) for h in heads]
    vn_b = [bf(v_new[h]) for h in heads]
    o = [_dot(bf(hs(q_dec, h)), st_b[h]) + _dot(bf(attn[h]), vn_b[h]) for h in heads]
    for h in heads:
        state_ref[h] = st[h] * hs(gl, h) + _dot_tn(bf(hs(k_dec, h)), vn_b[h])
    o_all = jnp.concatenate(o, axis=1)
    ms = _dot_split(o_all * o_all, blk1) * (1.0 / GDN_DV)
    o_ref[...] = o_all * lax.rsqrt(ms + RMS_EPS) * gnorm_ref[...] * _silu(z_ref[...])


def _gdn(qkv, z, small, conv_w, a_log, dt_bias, gdn_norm, b, s):
    c = GDN_CHUNK
    nc = s // c
    t = b * s
    hh = GDN_HEADS
    hd = hh * GDN_DK
    wq = qkv.shape[1]
    lane_head = np.arange(hd) // GDN_DK
    expand = jnp.asarray((np.arange(hh)[:, None] == lane_head[None, :]).astype(np.float32))
    blk1 = jnp.asarray((lane_head[:, None] == lane_head[None, :]).astype(np.float32)).astype(BF16)
    const = lambda a: pl.BlockSpec(a.shape, lambda i, j: (0,) * a.ndim)
    args = (conv_w, a_log.reshape(1, hh), dt_bias.reshape(1, hh), jnp.tile(gdn_norm.reshape(1, GDN_DV), (1, hh)),
            expand, blk1)
    return pl.pallas_call(
        _gdn_kernel,
        grid=(b, nc),
        in_specs=[pl.BlockSpec((c, wq), lambda i, j: (i * nc + j, 0)),
                  pl.BlockSpec((c, hh * GDN_DV), lambda i, j: (i * nc + j, 0)),
                  pl.BlockSpec((c, small.shape[1]), lambda i, j: (i * nc + j, 0))] + [const(a) for a in args],
        out_specs=pl.BlockSpec((c, hh * GDN_DV), lambda i, j: (i * nc + j, 0)),
        out_shape=jax.ShapeDtypeStruct((t, hh * GDN_DV), F32),
        scratch_shapes=[pltpu.VMEM((c + 8, wq), F32), pltpu.VMEM((hh, GDN_DK, GDN_DV), F32)],
        compiler_params=pltpu.CompilerParams(dimension_semantics=("arbitrary", "arbitrary"),
                                             vmem_limit_bytes=VMEM_LIMIT),
    )(qkv, z, small, *args)


def _cmp_kernel(kc_ref, vc_ref, pek_ref, pev_ref, wk_ref, wv_ref, ko_ref, vo_ref):
    for src, pe, w, dst in ((kc_ref, pek_ref, wk_ref, ko_ref), (vc_ref, pev_ref, wv_ref, vo_ref)):
        r = src[0, 0]
        y_lo = _dot(r + pe[0], w[0], HI)
        y_hi = _dot(r + pe[1], w[1], HI)
        n = y_hi.shape[0]
        dst[0, 0] = y_lo + pltpu.roll(y_hi, n - 1, 0)


def _compress(kc, vc, pe_k, pe_v, w_k, w_v):
    g, b, s, dh = kc.shape
    half = CMP_STRIDE * dh
    nr = s // CMP_STRIDE
    kc2 = kc.reshape(g, b, nr, half)
    vc2 = vc.reshape(g, b, nr, half)
    spec_in = pl.BlockSpec((1, 1, nr, half), lambda i, j: (j, i, 0, 0))
    spec_pe = pl.BlockSpec((2, 1, half), lambda i, j: (0, 0, 0))
    spec_w = pl.BlockSpec((2, half, dh), lambda i, j: (0, 0, 0))
    spec_o = pl.BlockSpec((1, 1, nr, dh), lambda i, j: (i, j, 0, 0))
    return pl.pallas_call(
        _cmp_kernel,
        grid=(b, g),
        in_specs=[spec_in, spec_in, spec_pe, spec_pe, spec_w, spec_w],
        out_specs=[spec_o, spec_o],
        out_shape=[jax.ShapeDtypeStruct((b, g, nr, dh), F32)] * 2,
        compiler_params=pltpu.CompilerParams(dimension_semantics=("arbitrary", "arbitrary"),
                                             vmem_limit_bytes=VMEM_LIMIT),
    )(kc2, vc2, pe_k.reshape(2, 1, half), pe_v.reshape(2, 1, half),
      w_k.reshape(2, half, dh), w_v.reshape(2, half, dh))


def _bias_chain(dist, rb_ref, heads):
    accs = [jnp.full(dist.shape, rb_ref[0, hd], F32) for hd in heads]
    for k in range(1, REL_BUCKETS):
        m = dist >= T5_THETA[k - 1]
        accs = [jnp.where(m, rb_ref[k, hd], a) for a, hd in zip(accs, heads)]
    return accs


CMP_BAND_LO = (CMP_BLOCK - 1 + REL_MAX_DIST - 1) // CMP_STRIDE
CMP_BAND_HI = (NSA_TQ - 1 - (CMP_BLOCK - 1)) // CMP_STRIDE
CMP_BAND = CMP_BAND_LO + CMP_BAND_HI + 1


def _nsa_kernel(rb_ref, q_ref, small_ref, kcmp_ref, vcmp_ref, ks_ref, vs_ref, kw_ref, vw_ref, ovlt_ref, o_ref,
                btab_ref, cpatch_ref, mrun_ref, lrun_ref, acc_ref, sfar_ref, *, group_axis):
    tq = NSA_TQ
    dh = NSA_DH
    hpg = NSA_HPG
    scale = dh ** -0.5
    qi = pl.program_id(2)
    t0 = qi * tq
    ncmp = kcmp_ref.shape[2]
    nsel = ovlt_ref.shape[0]
    blk_per_tile = tq // SEL_BLOCK
    nwin = WINDOW // tq

    ri = lax.broadcasted_iota(jnp.int32, (tq, tq), 0)
    ci = lax.broadcasted_iota(jnp.int32, (tq, tq), 1)
    dloc = ri - ci

    gsel = pl.program_id(group_axis)
    heads = [gsel * hpg + hh for hh in range(hpg)]

    def far_bias(hh):
        return rb_ref[REL_BUCKETS - 1, heads[hh]]

    @pl.when(qi == 0)
    def _():
        for dl in range(2):
            tabs = _bias_chain(dloc + dl * tq, rb_ref, heads)
            for hh in range(hpg):
                btab_ref[hh, dl] = tabs[hh]
        tabs = _bias_chain(ri - CMP_STRIDE * (ci - CMP_BAND_LO) - (CMP_BLOCK - 1), rb_ref, heads)
        for hh in range(hpg):
            cpatch_ref[hh * tq:(hh + 1) * tq, :] = jnp.where(ci == CMP_BAND, far_bias(hh), tabs[hh] - far_bias(hh))

    q_all = q_ref[...] * scale
    qs = [q_all[:, hh * dh:(hh + 1) * dh] for hh in range(hpg)]
    qstk = jnp.concatenate(qs, axis=0).astype(BF16)
    gates_all = _sigmoid(small_ref[...])
    goff = 2 * GDN_HEADS

    def gate_col(hh, br):
        cols = [gates_all[:, goff + ((g * hpg + hh) * NSA_BRANCHES + br):goff + ((g * hpg + hh) * NSA_BRANCHES + br) + 1]
                for g in range(NSA_KV_GROUPS)]
        out = cols[0]
        for g in range(1, NSA_KV_GROUPS):
            out = jnp.where(gsel == g, cols[g], out)
        return out

    nrow = hpg * tq
    trow = t0 + lax.broadcasted_iota(jnp.int32, (nrow, ncmp), 0) % tq
    ncol = lax.broadcasted_iota(jnp.int32, (nrow, ncmp), 1)
    mask_c = trow - (ncol * CMP_STRIDE + CMP_BLOCK - 1) >= 0
    pr = lax.broadcasted_iota(jnp.int32, (tq, ncmp), 0)
    pc = lax.broadcasted_iota(jnp.int32, (tq, ncmp), 1)
    in_band = (pc == t0 // CMP_STRIDE - CMP_BAND_LO + pr) & (pr < CMP_BAND)
    place = jnp.where(in_band | (pr == CMP_BAND), 1.0, 0.0).astype(BF16)
    vcmp = vcmp_ref[0, 0].astype(BF16)
    bias = _dot_split(cpatch_ref[...], place)
    s = jnp.where(mask_c, _dot_nt(jnp.concatenate(qs, axis=0), kcmp_ref[0, 0], HI) + bias, NEG)
    mx = jnp.max(s, axis=-1, keepdims=True)
    e = jnp.where(mask_c, jnp.exp(s - mx), 0.0)
    l = jnp.sum(e, axis=-1, keepdims=True)
    p = e * (1.0 / jnp.where(l > 0.0, l, 1.0))
    o_c_all = _dot(p.astype(BF16), vcmp)
    o_c = [o_c_all[hh * tq:(hh + 1) * tq] for hh in range(hpg)]
    psum = p[0:tq]
    for hh in range(1, hpg):
        psum = psum + p[hh * tq:(hh + 1) * tq]
    imp_t = _dot_nt(ovlt_ref[...], psum, HI)

    blk_t = lax.broadcasted_iota(jnp.int32, (nsel, tq), 0)
    cur_t = (t0 + lax.broadcasted_iota(jnp.int32, (nsel, tq), 1)) // SEL_BLOCK
    valid_t = blk_t <= cur_t
    forced_t = (blk_t == 0) | (blk_t == cur_t) | (blk_t == cur_t - 1)
    work = jnp.where(valid_t, imp_t + jnp.where(forced_t, FORCE_BONUS, 0.0), -1.0)
    rank = jnp.zeros((nsel, tq), F32)
    for i in range(nsel):
        xi = work[i:i + 1, :]
        rank = rank + jnp.where(blk_t > i, jnp.where(xi >= work, 1.0, 0.0), jnp.where(xi > work, 1.0, 0.0))
    sel_t = jnp.where((rank < float(min(N_SELECT, nsel))) & valid_t, 1.0, 0.0)

    mrun_ref[...] = jnp.full(mrun_ref.shape, NEG, F32)
    lrun_ref[...] = jnp.zeros(lrun_ref.shape, F32)
    acc_ref[...] = jnp.zeros(acc_ref.shape, F32)

    def scores(k_ref, j, dls, mask_fn):
        mask = mask_fn()
        width = len(dls) * tq
        start = pl.multiple_of(j * tq, tq)
        kt = k_ref[0, 0, pl.ds(start, width), :].astype(BF16)
        s = _dot_nt(qstk, kt)
        out = []
        for hh in range(hpg):
            parts = []
            for cblk, dl in enumerate(dls):
                bias = far_bias(hh) if (dl is None or dl >= 2) else btab_ref[hh, dl]
                parts.append(s[hh * tq:(hh + 1) * tq, cblk * tq:(cblk + 1) * tq] + bias)
            sh = parts[0] if len(parts) == 1 else jnp.concatenate(parts, axis=1)
            out.append(jnp.where(mask, sh, NEG))
        return out

    def fold(x, op):
        r = x[:, 0:tq]
        for cblk in range(1, x.shape[1] // tq):
            r = op(r, x[:, cblk * tq:(cblk + 1) * tq])
        return r

    def pass1(br, k_ref, v_ref, j, dls, mask_fn, slot=None):
        for hh, s in enumerate(scores(k_ref, j, dls, mask_fn)):
            if slot is not None:
                sfar_ref[slot, hh] = s
            mrun_ref[br, hh] = jnp.maximum(mrun_ref[br, hh], fold(s, jnp.maximum))

    def pass2(br, k_ref, v_ref, j, dls, mask_fn, slot=None):
        start = pl.multiple_of(j * tq, tq)
        vt = v_ref[0, 0, pl.ds(start, len(dls) * tq), :].astype(BF16)
        ps = []
        masked = scores(k_ref, j, dls, mask_fn) if slot is None else [sfar_ref[slot, hh] for hh in range(hpg)]
        for hh, s in enumerate(masked):
            m = mrun_ref[br, hh]
            p = jnp.exp(s - jnp.concatenate([m] * len(dls), axis=1))
            lrun_ref[br, hh] = lrun_ref[br, hh] + fold(p, jnp.add)
            ps.append(p.astype(BF16))
        acc_ref[br] = acc_ref[br] + _dot(jnp.concatenate(ps, axis=0), vt)

    def span_dist(dls):
        n = len(dls)
        r = lax.broadcasted_iota(jnp.int32, (tq, n * tq), 0)
        cidx = lax.broadcasted_iota(jnp.int32, (tq, n * tq), 1)
        return r - cidx + (dls[-1] + n - 1) * tq

    def sel_mask(j, dls):
        width = len(dls) * tq
        eb = lax.broadcasted_iota(jnp.int32, (nsel, width), 0)
        ek = lax.broadcasted_iota(jnp.int32, (nsel, width), 1) // SEL_BLOCK
        expand = jnp.where(eb == ek + j * blk_per_tile, 1.0, 0.0)
        mask = _dot_tn(sel_t, expand) > 0.5
        if dls[-1] == 0:
            mask = mask & (span_dist(dls) >= 0)
        return mask

    def win_mask(dls):
        dist = span_dist(dls)
        return (dist >= 0) & (dist < WINDOW)

    far_dls = [None] * NSA_FAR_TILES
    win_dls = list(range(nwin, -1, -1))

    def sweep(fn):
        nfar = jnp.maximum(qi - 1, 0)
        nspan = nfar // len(far_dls)

        def span_body(c, carry):
            fn(0, ks_ref, vs_ref, c * len(far_dls), far_dls, lambda: sel_mask(c * len(far_dls), far_dls), c)
            return carry
        lax.fori_loop(0, nspan, span_body, 0)

        def far_body(j, carry):
            fn(0, ks_ref, vs_ref, j, [None], lambda: sel_mask(j, [None]))
            return carry
        lax.fori_loop(nspan * len(far_dls), nfar, far_body, 0)

        @pl.when(qi >= 1)
        def _():
            fn(0, ks_ref, vs_ref, qi - 1, [1, 0], lambda: sel_mask(qi - 1, [1, 0]))

        @pl.when(qi == 0)
        def _():
            fn(0, ks_ref, vs_ref, qi, [0], lambda: sel_mask(qi, [0]))

        @pl.when(qi >= nwin)
        def _():
            fn(1, kw_ref, vw_ref, qi - nwin, win_dls, lambda: win_mask(win_dls))

        for dl in range(nwin - 1, -1, -1):
            @pl.when((qi >= dl) & (qi < nwin))
            def _():
                fn(1, kw_ref, vw_ref, qi - dl, [dl], lambda dl=dl: win_mask([dl]))

    sweep(pass1)
    for br in range(2):
        for hh in range(hpg):
            mrun_ref[br, hh] = jnp.broadcast_to(jnp.max(mrun_ref[br, hh], axis=-1, keepdims=True), (tq, tq))
    sweep(pass2)

    for hh in range(hpg):
        o_b = []
        for br in range(2):
            l = jnp.sum(lrun_ref[br, hh], axis=-1, keepdims=True)
            o_b.append(acc_ref[br, hh * tq:(hh + 1) * tq, :] * (1.0 / l))
        out = gate_col(hh, 0) * o_c[hh] + gate_col(hh, 1) * o_b[0] + gate_col(hh, 2) * o_b[1]
        o_ref[:, hh * dh:(hh + 1) * dh] = out


def _overlap_matrix_t(s):
    n_rows = s // CMP_STRIDE
    n_cmp = (s - CMP_BLOCK) // CMP_STRIDE + 1
    n_sel = s // SEL_BLOCK
    cmp_start = np.arange(n_rows) * CMP_STRIDE
    cmp_end = cmp_start + CMP_BLOCK - 1
    sel_start = np.arange(n_sel) * SEL_BLOCK
    ov = np.clip(np.minimum(cmp_end[:, None] + 1, sel_start[None, :] + SEL_BLOCK)
                 - np.maximum(cmp_start[:, None], sel_start[None, :]), 0, None).astype(np.float32) / CMP_BLOCK
    ov[n_cmp:] = 0.0
    return jnp.asarray(ov.T)


def _nsa(q, small, kcmp, vcmp, ks, vs, kw, vw, rel_bias, b, s):
    t = b * s
    tq = NSA_TQ
    nq = s // tq
    g = NSA_KV_GROUPS
    dh = NSA_DH
    gw = NSA_HPG * dh
    ncmp = kcmp.shape[2]
    nsel = s // SEL_BLOCK
    ovlt = _overlap_matrix_t(s)
    seq_spec = pl.BlockSpec((1, 1, s, dh), lambda i, j, k: (j, i, 0, 0))
    cmp_spec = pl.BlockSpec((1, 1, ncmp, dh), lambda i, j, k: (i, j, 0, 0))
    return pl.pallas_call(
        functools.partial(_nsa_kernel, group_axis=1),
        grid=(b, g, nq),
        in_specs=[pl.BlockSpec(memory_space=pltpu.SMEM),
                  pl.BlockSpec((tq, gw), lambda i, j, k: (i * nq + k, j)),
                  pl.BlockSpec((tq, small.shape[1]), lambda i, j, k: (i * nq + k, 0)),
                  cmp_spec, cmp_spec, seq_spec, seq_spec, seq_spec, seq_spec,
                  pl.BlockSpec((nsel, ncmp), lambda i, j, k: (0, 0))],
        out_specs=pl.BlockSpec((tq, gw), lambda i, j, k: (i * nq + k, j)),
        out_shape=jax.ShapeDtypeStruct((t, g * gw), F32),
        scratch_shapes=[pltpu.VMEM((NSA_HPG, 2, tq, tq), F32),
                        pltpu.VMEM((NSA_HPG * tq, tq), F32),
                        pltpu.VMEM((2, NSA_HPG, tq, tq), F32),
                        pltpu.VMEM((2, NSA_HPG, tq, tq), F32),
                        pltpu.VMEM((2, NSA_HPG * tq, dh), F32),
                        pltpu.VMEM((max(nq // NSA_FAR_TILES, 1), NSA_HPG, tq, NSA_FAR_TILES * tq), F32)],
        compiler_params=pltpu.CompilerParams(dimension_semantics=("arbitrary", "arbitrary", "arbitrary"),
                                             vmem_limit_bytes=VMEM_LIMIT),
    )(rel_bias, q, small, kcmp, vcmp, ks, vs, kw, vw, ovlt)


def _merge_kernel(x_ref, oa_ref, ob_ref, mg_ref, wa_ref, wb_ref, wo_ref, nf_ref, wq_ref, x1_ref, h2_ref, qry_ref):
    d = x_ref.shape[1]
    ya = _dot(oa_ref[...].astype(BF16), wa_ref[...])
    yb = _dot(ob_ref[...].astype(BF16), wb_ref[...])
    mg = mg_ref[...]
    mixed = _sigmoid(mg[:, 0:d]) * ya + _sigmoid(mg[:, d:2 * d]) * yb
    x1 = x_ref[...] + _dot(mixed.astype(BF16), wo_ref[...])
    x1_ref[...] = x1
    h2 = x1 * lax.rsqrt(jnp.mean(x1 * x1, axis=-1, keepdims=True) + RMS_EPS) * nf_ref[...]
    h2_ref[...] = h2
    qry_ref[...] = _dot(h2.astype(BF16), wq_ref[...])


def _merge(xf, oa, ob, mg, w_a, w_b, w_o, norm_ffn, w_query, tm=256):
    t, d = xf.shape
    nq = w_query.shape[1]
    row = lambda w: pl.BlockSpec((tm, w), lambda i: (i, 0))
    full = lambda a: pl.BlockSpec(a.shape, lambda i: (0, 0))
    nf = norm_ffn.reshape(1, d)
    return pl.pallas_call(
        _merge_kernel,
        grid=(t // tm,),
        in_specs=[row(d), row(oa.shape[1]), row(ob.shape[1]), row(mg.shape[1]),
                  full(w_a), full(w_b), full(w_o), full(nf), full(w_query)],
        out_specs=[row(d), row(d), row(nq)],
        out_shape=[jax.ShapeDtypeStruct((t, d), F32), jax.ShapeDtypeStruct((t, d), F32),
                   jax.ShapeDtypeStruct((t, nq), F32)],
        compiler_params=pltpu.CompilerParams(dimension_semantics=("arbitrary",), vmem_limit_bytes=VMEM_LIMIT),
    )(xf, oa, ob, mg, w_a, w_b, w_o, nf, w_query)


def _top_rows(work, k, payload=None):
    n_rows = work.shape[0]
    rows = lax.broadcasted_iota(jnp.int32, work.shape, 0).astype(F32)
    vals, idxs = [], []
    for _ in range(k):
        mx = jnp.max(work, axis=0, keepdims=True)
        first = jnp.min(jnp.where(work == mx, rows, float(n_rows)), axis=0, keepdims=True)
        hit = rows == first
        vals.append(mx)
        if payload is None:
            idxs.append(first)
        else:
            idxs.append(jnp.max(jnp.where(hit, payload, -1.0), axis=0, keepdims=True))
        work = jnp.where(hit, -jnp.inf, work)
    return jnp.concatenate(vals, axis=0), jnp.concatenate(idxs, axis=0)


def _peersel_kernel(qry_ref, keys_ref, eidx_ref, gate_ref):
    kk = PEER_TOPK
    for h in range(PEER_HEADS):
        tops = []
        for p in range(2):
            c0 = (h * 2 + p) * PEER_DQH
            sc = _dot_nt(keys_ref[h, p], qry_ref[:, c0:c0 + PEER_DQH], HI)
            tops.append(_top_rows(sc, kk))
        (s1, i1), (s2, i2) = tops
        cand_p, cidx_p = [], []
        a = 0
        while a < kk:
            nb = kk // (a + 1)
            if nb == 1:
                cand_p.append(s1[a:kk, :] + s2[0:1, :])
                cidx_p.append(i1[a:kk, :] * float(PEER_NKEYS) + i2[0:1, :])
                break
            nbp = -(-nb // 8) * 8
            v = s1[a:a + 1, :] + s2[0:nbp, :]
            if nbp != nb:
                v = jnp.where(lax.broadcasted_iota(jnp.int32, v.shape, 0) < nb, v, -jnp.inf)
            cand_p.append(v)
            cidx_p.append(i1[a:a + 1, :] * float(PEER_NKEYS) + i2[0:nbp, :])
            a += 1
        cand = jnp.concatenate(cand_p, axis=0)
        cidx = jnp.concatenate(cidx_p, axis=0)
        top, eidx = _top_rows(cand, kk, payload=cidx)
        e = jnp.exp(top - top[0:1, :])
        gate = e * (1.0 / jnp.sum(e, axis=0, keepdims=True))
        eidx_ref[h * kk:(h + 1) * kk, :] = eidx.astype(jnp.int32)
        gate_ref[h * kk:(h + 1) * kk, :] = gate


def _peer_select(qry, sub_keys, tt=128):
    t, nq = qry.shape
    npair = PEER_HEADS * PEER_TOPK
    return pl.pallas_call(
        _peersel_kernel,
        grid=(t // tt,),
        in_specs=[pl.BlockSpec((tt, nq), lambda i: (i, 0)),
                  pl.BlockSpec(sub_keys.shape, lambda i: (0, 0, 0, 0))],
        out_specs=[pl.BlockSpec((npair, tt), lambda i: (0, i)), pl.BlockSpec((npair, tt), lambda i: (0, i))],
        out_shape=[jax.ShapeDtypeStruct((npair, t), jnp.int32), jax.ShapeDtypeStruct((npair, t), F32)],
        compiler_params=pltpu.CompilerParams(dimension_semantics=("arbitrary",), vmem_limit_bytes=VMEM_LIMIT),
    )(qry, sub_keys)


def _gelu_tanh(x):
    return 0.5 * x * (1.0 + jnp.tanh(math.sqrt(2.0 / math.pi) * (x + 0.044715 * (x * x * x))))


PEER_ROW_SUB = 4
PEER_TILE_STRIDE = PEER_HEADS * PEER_TOPK + 1


def _pack_half(table):
    n, d = table.shape
    bits = lax.bitcast_convert_type(table.astype(BF16), jnp.uint16).astype(jnp.uint32)
    words = bits[:, :d // 2] | (bits[:, d // 2:] << 16)
    return words.reshape(n * PEER_ROW_SUB, d // 2 // PEER_ROW_SUB)


def _load_table_once(tab_hbm, tab, sem):
    @pl.when(pl.program_id(0) == 0)
    def _():
        cp = pltpu.make_async_copy(tab_hbm, tab, sem.at[0])
        cp.start()
        cp.wait()


def _gather_rows(idx_ref, tab, tile, tt, per):
    for k in range(per):
        e4 = pl.multiple_of(idx_ref[0, 0, tt * per + k], PEER_ROW_SUB)
        tile[tt, pl.ds(k, PEER_ROW_SUB, stride=PEER_TILE_STRIDE), :] = tab[pl.ds(e4, PEER_ROW_SUB), :]
    xs = [tile[tt, pl.ds(j * PEER_TILE_STRIDE, per), :] for j in range(PEER_ROW_SUB)]
    lo = jnp.concatenate([lax.bitcast_convert_type(x << 16, F32) for x in xs], axis=1).astype(BF16)
    hi = jnp.concatenate([lax.bitcast_convert_type(x & jnp.uint32(0xFFFF0000), F32) for x in xs], axis=1)
    return lo, hi.astype(BF16)


def _peer_act_kernel(idx_ref, h_ref, gate_ref, tab_hbm, w_ref, tab, tile, sem):
    _load_table_once(tab_hbm, tab, sem)
    tok, per = gate_ref.shape
    hb = h_ref[...].astype(BF16)
    half = hb.shape[1] // 2
    gate = gate_ref[...]
    trow = lax.broadcasted_iota(jnp.int32, (tok, per), 0)
    w = jnp.zeros((tok, per), F32)
    for tt in range(tok):
        u_lo, u_hi = _gather_rows(idx_ref, tab, tile, tt, per)
        act = _dot_nt(hb[:, :half], u_lo) + _dot_nt(hb[:, half:], u_hi)
        w = w + jnp.where(trow == tt, gate * _gelu_tanh(act), 0.0)
    w_ref[...] = w


def _peer_out_kernel(idx_ref, w_ref, x1_ref, nw_ref, tab_hbm, o_ref, tab, tile, sem):
    _load_table_once(tab_hbm, tab, sem)
    tok, per = w_ref.shape
    w = w_ref[...]
    trow = lax.broadcasted_iota(jnp.int32, (tok, per), 0)
    half = x1_ref.shape[1] // 2
    y_lo = jnp.zeros((tok, half), F32)
    y_hi = jnp.zeros((tok, half), F32)
    for tt in range(tok):
        v_lo, v_hi = _gather_rows(idx_ref, tab, tile, tt, per)
        wt = jnp.where(trow == tt, w, 0.0).astype(BF16)
        y_lo = y_lo + _dot(wt, v_lo)
        y_hi = y_hi + _dot(wt, v_hi)
    x2 = x1_ref[...] + jnp.concatenate([y_lo, y_hi], axis=1)
    o_ref[...] = x2 * lax.rsqrt(jnp.mean(x2 * x2, axis=-1, keepdims=True) + RMS_EPS) * nw_ref[...]


def _peer_ffn(eidx_tok, gate_tok, h2, x1, norm_final, expert_u, expert_v):
    t, d = h2.shape
    tok = PEER_TOK
    per = eidx_tok.shape[1]
    n = t // tok
    idx = (eidx_tok * PEER_ROW_SUB).reshape(n, 1, tok * per)
    tab_u = _pack_half(expert_u)
    tab_v = _pack_half(expert_v)
    idx_spec = pl.BlockSpec((1, 1, tok * per), lambda i: (i, 0, 0), memory_space=pltpu.SMEM)
    row = lambda wd: pl.BlockSpec((tok, wd), lambda i: (i, 0))
    scratch = [pltpu.VMEM(tab_u.shape, jnp.uint32),
               pltpu.VMEM((tok, PEER_ROW_SUB * PEER_TILE_STRIDE, 128), jnp.uint32),
               pltpu.SemaphoreType.DMA((1,))]
    params = pltpu.CompilerParams(dimension_semantics=("arbitrary",), vmem_limit_bytes=VMEM_LIMIT)
    w = pl.pallas_call(
        _peer_act_kernel,
        grid=(n,),
        in_specs=[idx_spec, row(d), row(per), pl.BlockSpec(memory_space=pl.ANY)],
        out_specs=row(per),
        out_shape=jax.ShapeDtypeStruct((t, per), F32),
        scratch_shapes=scratch,
        compiler_params=params,
    )(idx, h2, gate_tok, tab_u)
    return pl.pallas_call(
        _peer_out_kernel,
        grid=(n,),
        in_specs=[idx_spec, row(per), row(d), pl.BlockSpec((1, d), lambda i: (0, 0)),
                  pl.BlockSpec(memory_space=pl.ANY)],
        out_specs=row(d),
        out_shape=jax.ShapeDtypeStruct((t, d), F32),
        scratch_shapes=scratch,
        compiler_params=params,
    )(idx, w, x1, norm_final.reshape(1, d), tab_v)


def _layer(x, rel_bias, norm_mix, w_in, conv_w, a_log, dt_bias, gdn_norm, cmp_pe_k, cmp_pe_v, cmp_w_k, cmp_w_v,
           w_branch_a, w_branch_b, w_out, norm_ffn, w_query, sub_keys, expert_u, expert_v, norm_out):
    b, s, d = x.shape
    t = b * s
    xf = x.reshape(t, d)
    hk = GDN_HEADS * GDN_DK
    w_qkv = 2 * hk + GDN_HEADS * GDN_DV
    w_z = GDN_HEADS * GDN_DV
    w_nq = NSA_HEADS * NSA_DH
    w_nkv = NSA_BRANCHES * 2 * NSA_KV_GROUPS * NSA_DH
    w_gate = NSA_HEADS * NSA_BRANCHES
    w_small = 2 * GDN_HEADS + w_gate
    pad = (-w_small) % 128
    c0 = w_qkv + w_z
    c1 = c0 + 2 * GDN_HEADS
    c2 = c1 + w_nq
    c3 = c2 + w_nkv
    c4 = c3 + w_gate
    w_cat = jnp.concatenate([w_in[:, :c0], w_in[:, c1:c3], w_in[:, c4:], w_in[:, c0:c1], w_in[:, c3:c4],
                             jnp.zeros((d, pad), w_in.dtype)], axis=1).astype(BF16)
    widths = (w_qkv, w_z, w_nq, w_nkv, 2 * d, w_small + pad)
    qkv, z, nq, kvg, mg, small = _proj(xf, norm_mix, w_cat, widths, {3: NSA_DH})

    oa = _gdn(qkv, z, small, conv_w, a_log, dt_bias, gdn_norm, b, s)

    kv = kvg.reshape(NSA_BRANCHES, 2, NSA_KV_GROUPS, b, s, NSA_DH)
    kcmp, vcmp = _compress(kv[0, 0], kv[0, 1], cmp_pe_k, cmp_pe_v, cmp_w_k, cmp_w_v)
    ob = _nsa(nq, small, kcmp, vcmp, kv[1, 0], kv[1, 1], kv[2, 0], kv[2, 1], rel_bias, b, s)

    x1, h2, qry = _merge(xf, oa, ob, mg, w_branch_a.astype(BF16), w_branch_b.astype(BF16), w_out.astype(BF16),
                         norm_ffn, w_query.astype(BF16))
    eidx, gate = _peer_select(qry, sub_keys)
    out = _peer_ffn(eidx.T, gate.T, h2, x1, norm_out, expert_u, expert_v)
    return out.reshape(b, s, d)


def kernel(x, rel_bias, norm_final, norm_mix, w_in, conv_w, a_log, dt_bias, gdn_norm, cmp_pe_k, cmp_pe_v, cmp_w_k,
           cmp_w_v, w_branch_a, w_branch_b, w_out, norm_ffn, w_query, sub_keys, expert_u, expert_v):
    assert norm_mix.shape[0] == 1, "single-layer block"
    return _layer(x, rel_bias, norm_mix[0], w_in[0], conv_w[0], a_log[0], dt_bias[0], gdn_norm[0], cmp_pe_k[0],
                  cmp_pe_v[0], cmp_w_k[0], cmp_w_v[0], w_branch_a[0], w_branch_b[0], w_out[0], norm_ffn[0],
                  w_query[0], sub_keys[0], expert_u[0], expert_v[0], norm_final)
```

```python
import functools
import math

import numpy as np
import jax
import jax.numpy as jnp
from jax import lax
from jax.experimental import pallas as pl
from jax.experimental.pallas import tpu as pltpu

F32 = jnp.float32
BF16 = jnp.bfloat16
HI = lax.Precision.HIGHEST

RMS_EPS = 1e-6
NEG = -1e30

GDN_HEADS = 8
GDN_DK = 64
GDN_DV = 64
GDN_CONV = 4
GDN_CHUNK = 64

NSA_HEADS = 8
NSA_KV_GROUPS = 2
NSA_HPG = NSA_HEADS // NSA_KV_GROUPS
NSA_DH = 64
NSA_BRANCHES = 3
CMP_BLOCK = 32
CMP_STRIDE = 16
SEL_BLOCK = 64
N_SELECT = 16
WINDOW = 512
FORCE_BONUS = 1e4
REL_BUCKETS = 32
REL_MAX_DIST = 128

PEER_HEADS = 8
PEER_NKEYS = 128
PEER_TOPK = 16
PEER_DQH = 128

NSA_TQ = 128
NSA_FAR_TILES = 4
PEER_TOK = 32
VMEM_LIMIT = 56 * 1024 * 1024


def _t5_thresholds():
    d = np.arange(0, 4 * REL_MAX_DIST)
    max_exact = REL_BUCKETS // 2
    dd = np.maximum(d, 1).astype(np.float64)
    large = max_exact + (np.log(dd / max_exact) / math.log(REL_MAX_DIST / max_exact)
                         * (REL_BUCKETS - max_exact)).astype(np.int32)
    large = np.minimum(large, REL_BUCKETS - 1)
    b = np.where(d < max_exact, d, large)
    return [int(np.argmax(b >= k)) for k in range(1, REL_BUCKETS)]


T5_THETA = _t5_thresholds()


def _dot(a, b, precision=None):
    return jnp.dot(a, b, preferred_element_type=F32, precision=precision)


def _dot_nt(a, b, precision=None):
    return lax.dot_general(a, b, (((1,), (1,)), ((), ())), preferred_element_type=F32, precision=precision)


def _dot_tn(a, b, precision=None):
    return lax.dot_general(a, b, (((0,), (0,)), ((), ())), preferred_element_type=F32, precision=precision)


def _dot_split(x, w_bf16):
    hi = x.astype(BF16)
    lo = (x - hi.astype(F32)).astype(BF16)
    return _dot(hi, w_bf16) + _dot(lo, w_bf16)


def _sigmoid(x):
    return 1.0 / (1.0 + jnp.exp(-x))


def _silu(x):
    return x * _sigmoid(x)


def _softplus(x):
    return jnp.maximum(x, 0.0) + jnp.log(1.0 + jnp.exp(-jnp.abs(x)))


def _proj_kernel(x_ref, nw_ref, w_ref, *out_refs, widths):
    x = x_ref[...]
    h = x * lax.rsqrt(jnp.mean(x * x, axis=-1, keepdims=True) + RMS_EPS) * nw_ref[...]
    hb = h.astype(BF16)
    off = 0
    for o_ref, wd in zip(out_refs, widths):
        res = _dot(hb, w_ref[:, off:off + wd])
        if len(o_ref.shape) == 2:
            o_ref[...] = res
        else:
            pw = o_ref.shape[2]
            for c in range(o_ref.shape[0]):
                o_ref[c] = res[:, c * pw:(c + 1) * pw]
        off += wd


def _proj(xf, norm_w, w_cat, widths, split, tm=256):
    t, d = xf.shape
    n = w_cat.shape[1]
    out_specs, out_shape = [], []
    for k, wd in enumerate(widths):
        if k in split:
            pw = split[k]
            out_specs.append(pl.BlockSpec((wd // pw, tm, pw), lambda i: (0, i, 0)))
            out_shape.append(jax.ShapeDtypeStruct((wd // pw, t, pw), F32))
        else:
            out_specs.append(pl.BlockSpec((tm, wd), lambda i: (i, 0)))
            out_shape.append(jax.ShapeDtypeStruct((t, wd), F32))
    return pl.pallas_call(
        functools.partial(_proj_kernel, widths=widths),
        grid=(t // tm,),
        in_specs=[pl.BlockSpec((tm, d), lambda i: (i, 0)),
                  pl.BlockSpec((1, d), lambda i: (0, 0)),
                  pl.BlockSpec((d, n), lambda i: (0, 0))],
        out_specs=out_specs,
        out_shape=out_shape,
        compiler_params=pltpu.CompilerParams(dimension_semantics=("arbitrary",), vmem_limit_bytes=VMEM_LIMIT),
    )(xf, norm_w.reshape(1, d), w_cat)


def _gdn_kernel(qkv_ref, z_ref, small_ref, convw_ref, alog_ref, dtb_ref, gnorm_ref, expand_ref, blk1_ref,
                o_ref, xbuf_ref, state_ref):
    c = GDN_CHUNK
    nh = GDN_HEADS
    dk = GDN_DK
    hd = nh * dk

    @pl.when(pl.program_id(1) == 0)
    def _():
        xbuf_ref[0:8, :] = jnp.zeros((8, xbuf_ref.shape[1]), F32)
        state_ref[...] = jnp.zeros(state_ref.shape, F32)

    xbuf_ref[8:8 + c, :] = qkv_ref[...]
    acc = xbuf_ref[pl.ds(8 - (GDN_CONV - 1), c), :] * convw_ref[0:1, :]
    for j in range(1, GDN_CONV):
        acc = acc + xbuf_ref[pl.ds(8 - (GDN_CONV - 1) + j, c), :] * convw_ref[j:j + 1, :]
    xbuf_ref[0:8, :] = xbuf_ref[c:c + 8, :]
    qkv = _silu(acc)

    row = lax.broadcasted_iota(jnp.int32, (c, c), 0)
    col = lax.broadcasted_iota(jnp.int32, (c, c), 1)
    lower = row >= col
    strict = row > col
    eye = (row == col).astype(F32)
    ltri = lower.astype(F32)

    expand = expand_ref[...]
    blk1 = blk1_ref[...]
    q = qkv[:, 0:hd]
    k = qkv[:, hd:2 * hd]
    v = qkv[:, 2 * hd:3 * hd]
    q = q * lax.rsqrt(_dot_split(q * q, blk1) + 1e-6) * (dk ** -0.5)
    k = k * lax.rsqrt(_dot_split(k * k, blk1) + 1e-6)
    small = small_ref[...]
    beta = _dot(_sigmoid(small[:, 0:nh]), expand, HI)
    g_c = -jnp.exp(alog_ref[...]) * _softplus(small[:, nh:2 * nh] + dtb_ref[...])
    gc_c = _dot(ltri, g_c, HI)
    gc = _dot(gc_c, expand, HI)
    eye_h = (lax.broadcasted_iota(jnp.int32, (nh, nh), 0) == lax.broadcasted_iota(jnp.int32, (nh, nh), 1))
    gc_r = _dot_nt(eye_h.astype(F32), gc_c, HI)
    eg = jnp.exp(gc)
    g_last = gc[c - 1:c, :]
    gl = jnp.exp(g_last)
    kb = k * beta
    vb = v * beta
    kbe = kb * eg
    q_dec = q * eg
    k_dec = k * jnp.exp(g_last - gc)

    def hs(x, h):
        return x[:, h * dk:(h + 1) * dk]

    def bf(x):
        return x.astype(BF16)

    heads = range(nh)
    decay = [jnp.where(lower, jnp.exp(jnp.where(lower, hs(gc, h) - gc_r[h:h + 1, :], 0.0)), 0.0) for h in heads]
    k_b = [bf(hs(k, h)) for h in heads]
    a = [jnp.where(strict, _dot_nt(bf(hs(kb, h)), k_b[h]) * decay[h], 0.0) for h in heads]
    tinv = [eye - a[h] for h in heads]
    p_b = [bf(a[h]) for h in heads]
    p_b = [bf(_dot(p_b[h], p_b[h])) for h in heads]
    for s in range(5):
        tinv = [tinv[h] + _dot(bf(tinv[h]), p_b[h]) for h in heads]
        if s < 4:
            p_b = [bf(_dot(p_b[h], p_b[h])) for h in heads]
    t_b = [bf(tinv[h]) for h in heads]
    u = [_dot(t_b[h], bf(hs(vb, h))) for h in heads]
    w = [_dot(t_b[h], bf(hs(kbe, h))) for h in heads]
    attn = [_dot_nt(bf(hs(q, h)), k_b[h]) * decay[h] for h in heads]
    st = [state_ref[h] for h in heads]
    st_b = [bf(st[h]) for h in heads]
    v_new = [u[h] - _dot(bf(w[h]), st_b[h]) for h in heads]
    vn_b = [bf(v_new[h]) for h in heads]
    o = [_dot(bf(hs(q_dec, h)), st_b[h]) + _dot(bf(attn[h]), vn_b[h]) for h in heads]
    for h in heads:
        state_ref[h] = st[h] * hs(gl, h) + _dot_tn(bf(hs(k_dec, h)), vn_b[h])
    o_all = jnp.concatenate(o, axis=1)
    ms = _dot_split(o_all * o_all, blk1) * (1.0 / GDN_DV)
    o_ref[...] = o_all * lax.rsqrt(ms + RMS_EPS) * gnorm_ref[...] * _silu(z_ref[...])


def _gdn(qkv, z, small, conv_w, a_log, dt_bias, gdn_norm, b, s):
    c = GDN_CHUNK
    nc = s // c
    t = b * s
    hh = GDN_HEADS
    hd = hh * GDN_DK
    wq = qkv.shape[1]
    lane_head = np.arange(hd) // GDN_DK
    expand = jnp.asarray((np.arange(hh)[:, None] == lane_head[None, :]).astype(np.float32))
    blk1 = jnp.asarray((lane_head[:, None] == lane_head[None, :]).astype(np.float32)).astype(BF16)
    const = lambda a: pl.BlockSpec(a.shape, lambda i, j: (0,) * a.ndim)
    args = (conv_w, a_log.reshape(1, hh), dt_bias.reshape(1, hh), jnp.tile(gdn_norm.reshape(1, GDN_DV), (1, hh)),
            expand, blk1)
    return pl.pallas_call(
        _gdn_kernel,
        grid=(b, nc),
        in_specs=[pl.BlockSpec((c, wq), lambda i, j: (i * nc + j, 0)),
                  pl.BlockSpec((c, hh * GDN_DV), lambda i, j: (i * nc + j, 0)),
                  pl.BlockSpec((c, small.shape[1]), lambda i, j: (i * nc + j, 0))] + [const(a) for a in args],
        out_specs=pl.BlockSpec((c, hh * GDN_DV), lambda i, j: (i * nc + j, 0)),
        out_shape=jax.ShapeDtypeStruct((t, hh * GDN_DV), F32),
        scratch_shapes=[pltpu.VMEM((c + 8, wq), F32), pltpu.VMEM((hh, GDN_DK, GDN_DV), F32)],
        compiler_params=pltpu.CompilerParams(dimension_semantics=("arbitrary", "arbitrary"),
                                             vmem_limit_bytes=VMEM_LIMIT),
    )(qkv, z, small, *args)


def _cmp_kernel(kc_ref, vc_ref, pek_ref, pev_ref, wk_ref, wv_ref, ko_ref, vo_ref):
    for src, pe, w, dst in ((kc_ref, pek_ref, wk_ref, ko_ref), (vc_ref, pev_ref, wv_ref, vo_ref)):
        r = src[0, 0]
        y_lo = _dot(r + pe[0], w[0], HI)
        y_hi = _dot(r + pe[1], w[1], HI)
        n = y_hi.shape[0]
        dst[0, 0] = y_lo + pltpu.roll(y_hi, n - 1, 0)


def _compress(kc, vc, pe_k, pe_v, w_k, w_v):
    g, b, s, dh = kc.shape
    half = CMP_STRIDE * dh
    nr = s // CMP_STRIDE
    kc2 = kc.reshape(g, b, nr, half)
    vc2 = vc.reshape(g, b, nr, half)
    spec_in = pl.BlockSpec((1, 1, nr, half), lambda i, j: (j, i, 0, 0))
    spec_pe = pl.BlockSpec((2, 1, half), lambda i, j: (0, 0, 0))
    spec_w = pl.BlockSpec((2, half, dh), lambda i, j: (0, 0, 0))
    spec_o = pl.BlockSpec((1, 1, nr, dh), lambda i, j: (i, j, 0, 0))
    return pl.pallas_call(
        _cmp_kernel,
        grid=(b, g),
        in_specs=[spec_in, spec_in, spec_pe, spec_pe, spec_w, spec_w],
        out_specs=[spec_o, spec_o],
        out_shape=[jax.ShapeDtypeStruct((b, g, nr, dh), F32)] * 2,
        compiler_params=pltpu.CompilerParams(dimension_semantics=("arbitrary", "arbitrary"),
                                             vmem_limit_bytes=VMEM_LIMIT),
    )(kc2, vc2, pe_k.reshape(2, 1, half), pe_v.reshape(2, 1, half),
      w_k.reshape(2, half, dh), w_v.reshape(2, half, dh))


def _bias_chain(dist, rb_ref, heads):
    accs = [jnp.full(dist.shape, rb_ref[0, hd], F32) for hd in heads]
    for k in range(1, REL_BUCKETS):
        m = dist >= T5_THETA[k - 1]
        accs = [jnp.where(m, rb_ref[k, hd], a) for a, hd in zip(accs, heads)]
    return accs


CMP_BAND_LO = (CMP_BLOCK - 1 + REL_MAX_DIST - 1) // CMP_STRIDE
CMP_BAND_HI = (NSA_TQ - 1 - (CMP_BLOCK - 1)) // CMP_STRIDE
CMP_BAND = CMP_BAND_LO + CMP_BAND_HI + 1


def _nsa_kernel(rb_ref, q_ref, small_ref, kcmp_ref, vcmp_ref, ks_ref, vs_ref, kw_ref, vw_ref, ovlt_ref, o_ref,
                btab_ref, cpatch_ref, mrun_ref, lrun_ref, acc_ref, sfar_ref, snear_ref, swin_ref, *,
                group_axis):
    tq = NSA_TQ
    dh = NSA_DH
    hpg = NSA_HPG
    scale = dh ** -0.5
    qi = pl.program_id(2)
    t0 = qi * tq
    ncmp = kcmp_ref.shape[2]
    nsel = ovlt_ref.shape[0]
    blk_per_tile = tq // SEL_BLOCK
    nwin = WINDOW // tq

    ri = lax.broadcasted_iota(jnp.int32, (tq, tq), 0)
    ci = lax.broadcasted_iota(jnp.int32, (tq, tq), 1)
    dloc = ri - ci

    gsel = pl.program_id(group_axis)
    heads = [gsel * hpg + hh for hh in range(hpg)]

    def far_bias(hh):
        return rb_ref[REL_BUCKETS - 1, heads[hh]]

    @pl.when(qi == 0)
    def _():
        for dl in range(2):
            tabs = _bias_chain(dloc + dl * tq, rb_ref, heads)
            for hh in range(hpg):
                btab_ref[hh, dl] = tabs[hh]
        tabs = _bias_chain(ri - CMP_STRIDE * (ci - CMP_BAND_LO) - (CMP_BLOCK - 1), rb_ref, heads)
        for hh in range(hpg):
            cpatch_ref[hh * tq:(hh + 1) * tq, :] = jnp.where(ci == CMP_BAND, far_bias(hh), tabs[hh] - far_bias(hh))

    q_all = q_ref[...] * scale
    qs = [q_all[:, hh * dh:(hh + 1) * dh] for hh in range(hpg)]
    qstk = jnp.concatenate(qs, axis=0).astype(BF16)
    gates_all = _sigmoid(small_ref[...])
    goff = 2 * GDN_HEADS

    def gate_col(hh, br):
        cols = [gates_all[:, goff + ((g * hpg + hh) * NSA_BRANCHES + br):goff + ((g * hpg + hh) * NSA_BRANCHES + br) + 1]
                for g in range(NSA_KV_GROUPS)]
        out = cols[0]
        for g in range(1, NSA_KV_GROUPS):
            out = jnp.where(gsel == g, cols[g], out)
        return out

    nrow = hpg * tq
    trow = t0 + lax.broadcasted_iota(jnp.int32, (nrow, ncmp), 0) % tq
    ncol = lax.broadcasted_iota(jnp.int32, (nrow, ncmp), 1)
    mask_c = trow - (ncol * CMP_STRIDE + CMP_BLOCK - 1) >= 0
    pr = lax.broadcasted_iota(jnp.int32, (tq, ncmp), 0)
    pc = lax.broadcasted_iota(jnp.int32, (tq, ncmp), 1)
    in_band = (pc == t0 // CMP_STRIDE - CMP_BAND_LO + pr) & (pr < CMP_BAND)
    place = jnp.where(in_band | (pr == CMP_BAND), 1.0, 0.0).astype(BF16)
    vcmp = vcmp_ref[0, 0].astype(BF16)
    bias = _dot_split(cpatch_ref[...], place)
    s = jnp.where(mask_c, _dot_nt(jnp.concatenate(qs, axis=0), kcmp_ref[0, 0], HI) + bias, NEG)
    mx = jnp.max(s, axis=-1, keepdims=True)
    e = jnp.where(mask_c, jnp.exp(s - mx), 0.0)
    l = jnp.sum(e, axis=-1, keepdims=True)
    p = e * (1.0 / jnp.where(l > 0.0, l, 1.0))
    o_c_all = _dot(p.astype(BF16), vcmp)
    o_c = [o_c_all[hh * tq:(hh + 1) * tq] for hh in range(hpg)]
    psum = p[0:tq]
    for hh in range(1, hpg):
        psum = psum + p[hh * tq:(hh + 1) * tq]
    imp_t = _dot_nt(ovlt_ref[...], psum, HI)

    blk_t = lax.broadcasted_iota(jnp.int32, (nsel, tq), 0)
    cur_t = (t0 + lax.broadcasted_iota(jnp.int32, (nsel, tq), 1)) // SEL_BLOCK
    valid_t = blk_t <= cur_t
    forced_t = (blk_t == 0) | (blk_t == cur_t) | (blk_t == cur_t - 1)
    work = jnp.where(valid_t, imp_t + jnp.where(forced_t, FORCE_BONUS, 0.0), -1.0)
    rank = jnp.zeros((nsel, tq), F32)
    for i in range(nsel):
        xi = work[i:i + 1, :]
        rank = rank + jnp.where(blk_t > i, jnp.where(xi >= work, 1.0, 0.0), jnp.where(xi > work, 1.0, 0.0))
    sel_t = jnp.where((rank < float(min(N_SELECT, nsel))) & valid_t, 1.0, 0.0)

    mrun_ref[...] = jnp.full(mrun_ref.shape, NEG, F32)
    lrun_ref[...] = jnp.zeros(lrun_ref.shape, F32)
    acc_ref[...] = jnp.zeros(acc_ref.shape, F32)

    def scores(k_ref, j, dls, mask_fn):
        mask = mask_fn()
        width = len(dls) * tq
        start = pl.multiple_of(j * tq, tq)
        kt = k_ref[0, 0, pl.ds(start, width), :].astype(BF16)
        s = _dot_nt(qstk, kt)
        out = []
        for hh in range(hpg):
            parts = []
            for cblk, dl in enumerate(dls):
                bias = far_bias(hh) if (dl is None or dl >= 2) else btab_ref[hh, dl]
                parts.append(s[hh * tq:(hh + 1) * tq, cblk * tq:(cblk + 1) * tq] + bias)
            sh = parts[0] if len(parts) == 1 else jnp.concatenate(parts, axis=1)
            out.append(jnp.where(mask, sh, NEG))
        return out

    def fold(x, op):
        r = x[:, 0:tq]
        for cblk in range(1, x.shape[1] // tq):
            r = op(r, x[:, cblk * tq:(cblk + 1) * tq])
        return r

    def pass1(br, k_ref, v_ref, j, dls, mask_fn, cache=None):
        for hh, s in enumerate(scores(k_ref, j, dls, mask_fn)):
            if cache is not None:
                cache[0][cache[1] + (hh,)] = s
            mrun_ref[br, hh] = jnp.maximum(mrun_ref[br, hh], fold(s, jnp.maximum))

    def pass2(br, k_ref, v_ref, j, dls, mask_fn, cache=None):
        start = pl.multiple_of(j * tq, tq)
        vt = v_ref[0, 0, pl.ds(start, len(dls) * tq), :].astype(BF16)
        ps = []
        if cache is None:
            masked = scores(k_ref, j, dls, mask_fn)
        else:
            masked = [cache[0][cache[1] + (hh,)] for hh in range(hpg)]
        for hh, s in enumerate(masked):
            m = mrun_ref[br, hh]
            p = jnp.exp(s - jnp.concatenate([m] * len(dls), axis=1))
            lrun_ref[br, hh] = lrun_ref[br, hh] + fold(p, jnp.add)
            ps.append(p.astype(BF16))
        acc_ref[br] = acc_ref[br] + _dot(jnp.concatenate(ps, axis=0), vt)

    def span_dist(dls):
        n = len(dls)
        r = lax.broadcasted_iota(jnp.int32, (tq, n * tq), 0)
        cidx = lax.broadcasted_iota(jnp.int32, (tq, n * tq), 1)
        return r - cidx + (dls[-1] + n - 1) * tq

    def sel_mask(j, dls):
        width = len(dls) * tq
        eb = lax.broadcasted_iota(jnp.int32, (nsel, width), 0)
        ek = lax.broadcasted_iota(jnp.int32, (nsel, width), 1) // SEL_BLOCK
        expand = jnp.where(eb == ek + j * blk_per_tile, 1.0, 0.0)
        mask = _dot_tn(sel_t, expand) > 0.5
        if dls[-1] == 0:
            mask = mask & (span_dist(dls) >= 0)
        return mask

    def win_mask(dls):
        dist = span_dist(dls)
        return (dist >= 0) & (dist < WINDOW)

    far_dls = [None] * NSA_FAR_TILES
    win_dls = list(range(nwin, -1, -1))

    def sweep(fn):
        nfar = jnp.maximum(qi - 1, 0)
        nspan = nfar // len(far_dls)

        def span_body(c, carry):
            fn(0, ks_ref, vs_ref, c * len(far_dls), far_dls, lambda: sel_mask(c * len(far_dls), far_dls),
               (sfar_ref, (c,)))
            return carry
        lax.fori_loop(0, nspan, span_body, 0)

        def far_body(j, carry):
            fn(0, ks_ref, vs_ref, j, [None], lambda: sel_mask(j, [None]))
            return carry
        lax.fori_loop(nspan * len(far_dls), nfar, far_body, 0)

        @pl.when(qi >= 1)
        def _():
            fn(0, ks_ref, vs_ref, qi - 1, [1, 0], lambda: sel_mask(qi - 1, [1, 0]), (snear_ref, ()))

        @pl.when(qi == 0)
        def _():
            fn(0, ks_ref, vs_ref, qi, [0], lambda: sel_mask(qi, [0]))

        @pl.when(qi >= nwin)
        def _():
            fn(1, kw_ref, vw_ref, qi - nwin, win_dls, lambda: win_mask(win_dls), (swin_ref, ()))

        for dl in range(nwin - 1, -1, -1):
            @pl.when((qi >= dl) & (qi < nwin))
            def _():
                fn(1, kw_ref, vw_ref, qi - dl, [dl], lambda dl=dl: win_mask([dl]))

    sweep(pass1)
    for br in range(2):
        for hh in range(hpg):
            mrun_ref[br, hh] = jnp.broadcast_to(jnp.max(mrun_ref[br, hh], axis=-1, keepdims=True), (tq, tq))
    sweep(pass2)

    for hh in range(hpg):
        o_b = []
        for br in range(2):
            l = jnp.sum(lrun_ref[br, hh], axis=-1, keepdims=True)
            o_b.append(acc_ref[br, hh * tq:(hh + 1) * tq, :] * (1.0 / l))
        out = gate_col(hh, 0) * o_c[hh] + gate_col(hh, 1) * o_b[0] + gate_col(hh, 2) * o_b[1]
        o_ref[:, hh * dh:(hh + 1) * dh] = out


def _overlap_matrix_t(s):
    n_rows = s // CMP_STRIDE
    n_cmp = (s - CMP_BLOCK) // CMP_STRIDE + 1
    n_sel = s // SEL_BLOCK
    cmp_start = np.arange(n_rows) * CMP_STRIDE
    cmp_end = cmp_start + CMP_BLOCK - 1
    sel_start = np.arange(n_sel) * SEL_BLOCK
    ov = np.clip(np.minimum(cmp_end[:, None] + 1, sel_start[None, :] + SEL_BLOCK)
                 - np.maximum(cmp_start[:, None], sel_start[None, :]), 0, None).astype(np.float32) / CMP_BLOCK
    ov[n_cmp:] = 0.0
    return jnp.asarray(ov.T)


def _nsa(q, small, kcmp, vcmp, kv, rel_bias, b, s):
    t = b * s
    tq = NSA_TQ
    nq = s // tq
    g = NSA_KV_GROUPS
    dh = NSA_DH
    gw = NSA_HPG * dh
    ncmp = kcmp.shape[2]
    nsel = s // SEL_BLOCK
    ovlt = _overlap_matrix_t(s)
    def seq_spec(br, kvi):
        return pl.BlockSpec((None, None, 1, 1, s, dh), lambda i, j, k: (br, kvi, j, i, 0, 0))
    cmp_spec = pl.BlockSpec((1, 1, ncmp, dh), lambda i, j, k: (i, j, 0, 0))
    return pl.pallas_call(
        functools.partial(_nsa_kernel, group_axis=1),
        grid=(b, g, nq),
        in_specs=[pl.BlockSpec(memory_space=pltpu.SMEM),
                  pl.BlockSpec((tq, gw), lambda i, j, k: (i * nq + k, j)),
                  pl.BlockSpec((tq, small.shape[1]), lambda i, j, k: (i * nq + k, 0)),
                  cmp_spec, cmp_spec, seq_spec(1, 0), seq_spec(1, 1), seq_spec(2, 0), seq_spec(2, 1),
                  pl.BlockSpec((nsel, ncmp), lambda i, j, k: (0, 0))],
        out_specs=pl.BlockSpec((tq, gw), lambda i, j, k: (i * nq + k, j)),
        out_shape=jax.ShapeDtypeStruct((t, g * gw), F32),
        scratch_shapes=[pltpu.VMEM((NSA_HPG, 2, tq, tq), F32),
                        pltpu.VMEM((NSA_HPG * tq, tq), F32),
                        pltpu.VMEM((2, NSA_HPG, tq, tq), F32),
                        pltpu.VMEM((2, NSA_HPG, tq, tq), F32),
                        pltpu.VMEM((2, NSA_HPG * tq, dh), F32),
                        pltpu.VMEM((max(nq // NSA_FAR_TILES, 1), NSA_HPG, tq, NSA_FAR_TILES * tq), F32),
                        pltpu.VMEM((NSA_HPG, tq, 2 * tq), F32),
                        pltpu.VMEM((NSA_HPG, tq, (WINDOW // tq + 1) * tq), F32)],
        compiler_params=pltpu.CompilerParams(dimension_semantics=("arbitrary", "arbitrary", "arbitrary"),
                                             vmem_limit_bytes=VMEM_LIMIT),
    )(rel_bias, q, small, kcmp, vcmp, kv, kv, kv, kv, ovlt)


def _merge_kernel(x_ref, oa_ref, ob_ref, mg_ref, wa_ref, wb_ref, wo_ref, nf_ref, wq_ref, x1_ref, h2_ref, qry_ref):
    d = x_ref.shape[1]
    ya = _dot(oa_ref[...].astype(BF16), wa_ref[...])
    yb = _dot(ob_ref[...].astype(BF16), wb_ref[...])
    mg = mg_ref[...]
    mixed = _sigmoid(mg[:, 0:d]) * ya + _sigmoid(mg[:, d:2 * d]) * yb
    x1 = x_ref[...] + _dot(mixed.astype(BF16), wo_ref[...])
    x1_ref[...] = x1
    h2 = x1 * lax.rsqrt(jnp.mean(x1 * x1, axis=-1, keepdims=True) + RMS_EPS) * nf_ref[...]
    h2_ref[...] = h2
    qry_ref[...] = _dot(h2.astype(BF16), wq_ref[...])


def _merge(xf, oa, ob, mg, w_a, w_b, w_o, norm_ffn, w_query, tm=256):
    t, d = xf.shape
    nq = w_query.shape[1]
    row = lambda w: pl.BlockSpec((tm, w), lambda i: (i, 0))
    full = lambda a: pl.BlockSpec(a.shape, lambda i: (0, 0))
    nf = norm_ffn.reshape(1, d)
    return pl.pallas_call(
        _merge_kernel,
        grid=(t // tm,),
        in_specs=[row(d), row(oa.shape[1]), row(ob.shape[1]), row(mg.shape[1]),
                  full(w_a), full(w_b), full(w_o), full(nf), full(w_query)],
        out_specs=[row(d), row(d), row(nq)],
        out_shape=[jax.ShapeDtypeStruct((t, d), F32), jax.ShapeDtypeStruct((t, d), F32),
                   jax.ShapeDtypeStruct((t, nq), F32)],
        compiler_params=pltpu.CompilerParams(dimension_semantics=("arbitrary",), vmem_limit_bytes=VMEM_LIMIT),
    )(xf, oa, ob, mg, w_a, w_b, w_o, nf, w_query)


def _top_rows(work, k, payload=None):
    n_rows = work.shape[0]
    rows = lax.broadcasted_iota(jnp.int32, work.shape, 0).astype(F32)
    vals, idxs = [], []
    for _ in range(k):
        mx = jnp.max(work, axis=0, keepdims=True)
        first = jnp.min(jnp.where(work == mx, rows, float(n_rows)), axis=0, keepdims=True)
        hit = rows == first
        vals.append(mx)
        if payload is None:
            idxs.append(first)
        else:
            idxs.append(jnp.max(jnp.where(hit, payload, -1.0), axis=0, keepdims=True))
        work = jnp.where(hit, -jnp.inf, work)
    return jnp.concatenate(vals, axis=0), jnp.concatenate(idxs, axis=0)


def _peersel_kernel(qry_ref, keys_ref, eidx_ref, gate_ref):
    kk = PEER_TOPK
    for h in range(PEER_HEADS):
        tops = []
        for p in range(2):
            c0 = (h * 2 + p) * PEER_DQH
            sc = _dot_nt(keys_ref[h, p], qry_ref[:, c0:c0 + PEER_DQH], HI)
            tops.append(_top_rows(sc, kk))
        (s1, i1), (s2, i2) = tops
        cand_p, cidx_p = [], []
        a = 0
        while a < kk:
            nb = kk // (a + 1)
            if nb == 1:
                cand_p.append(s1[a:kk, :] + s2[0:1, :])
                cidx_p.append(i1[a:kk, :] * float(PEER_NKEYS) + i2[0:1, :])
                break
            nbp = -(-nb // 8) * 8
            v = s1[a:a + 1, :] + s2[0:nbp, :]
            if nbp != nb:
                v = jnp.where(lax.broadcasted_iota(jnp.int32, v.shape, 0) < nb, v, -jnp.inf)
            cand_p.append(v)
            cidx_p.append(i1[a:a + 1, :] * float(PEER_NKEYS) + i2[0:nbp, :])
            a += 1
        cand = jnp.concatenate(cand_p, axis=0)
        cidx = jnp.concatenate(cidx_p, axis=0)
        top, eidx = _top_rows(cand, kk, payload=cidx)
        e = jnp.exp(top - top[0:1, :])
        gate = e * (1.0 / jnp.sum(e, axis=0, keepdims=True))
        eidx_ref[h * kk:(h + 1) * kk, :] = eidx.astype(jnp.int32)
        gate_ref[h * kk:(h + 1) * kk, :] = gate


def _peer_select(qry, sub_keys, tt=128):
    t, nq = qry.shape
    npair = PEER_HEADS * PEER_TOPK
    return pl.pallas_call(
        _peersel_kernel,
        grid=(t // tt,),
        in_specs=[pl.BlockSpec((tt, nq), lambda i: (i, 0)),
                  pl.BlockSpec(sub_keys.shape, lambda i: (0, 0, 0, 0))],
        out_specs=[pl.BlockSpec((npair, tt), lambda i: (0, i)), pl.BlockSpec((npair, tt), lambda i: (0, i))],
        out_shape=[jax.ShapeDtypeStruct((npair, t), jnp.int32), jax.ShapeDtypeStruct((npair, t), F32)],
        compiler_params=pltpu.CompilerParams(dimension_semantics=("arbitrary",), vmem_limit_bytes=VMEM_LIMIT),
    )(qry, sub_keys)


def _gelu_tanh(x):
    return 0.5 * x * (1.0 + jnp.tanh(math.sqrt(2.0 / math.pi) * (x + 0.044715 * (x * x * x))))


PEER_ROW_SUB = 4
PEER_TILE_STRIDE = PEER_HEADS * PEER_TOPK + 1


def _pack_half(table):
    n, d = table.shape
    bits = lax.bitcast_convert_type(table.astype(BF16), jnp.uint16).astype(jnp.uint32)
    words = bits[:, :d // 2] | (bits[:, d // 2:] << 16)
    return words.reshape(n * PEER_ROW_SUB, d // 2 // PEER_ROW_SUB)


def _load_table_once(tab_hbm, tab, sem):
    @pl.when(pl.program_id(0) == 0)
    def _():
        cp = pltpu.make_async_copy(tab_hbm, tab, sem.at[0])
        cp.start()
        cp.wait()


def _gather_rows(idx_ref, tab, tile, tt, per):
    for k in range(per):
        e4 = pl.multiple_of(idx_ref[0, 0, tt * per + k], PEER_ROW_SUB)
        tile[tt, pl.ds(k, PEER_ROW_SUB, stride=PEER_TILE_STRIDE), :] = tab[pl.ds(e4, PEER_ROW_SUB), :]
    xs = [tile[tt, pl.ds(j * PEER_TILE_STRIDE, per), :] for j in range(PEER_ROW_SUB)]
    lo = jnp.concatenate([lax.bitcast_convert_type(x << 16, F32) for x in xs], axis=1).astype(BF16)
    hi = jnp.concatenate([lax.bitcast_convert_type(x & jnp.uint32(0xFFFF0000), F32) for x in xs], axis=1)
    return lo, hi.astype(BF16)


def _peer_act_kernel(idx_ref, h_ref, gate_ref, tab_hbm, w_ref, tab, tile, sem):
    _load_table_once(tab_hbm, tab, sem)
    tok, per = gate_ref.shape
    hb = h_ref[...].astype(BF16)
    half = hb.shape[1] // 2
    gate = gate_ref[...]
    trow = lax.broadcasted_iota(jnp.int32, (tok, per), 0)
    w = jnp.zeros((tok, per), F32)
    for tt in range(tok):
        u_lo, u_hi = _gather_rows(idx_ref, tab, tile, tt, per)
        act = _dot_nt(hb[:, :half], u_lo) + _dot_nt(hb[:, half:], u_hi)
        w = w + jnp.where(trow == tt, gate * _gelu_tanh(act), 0.0)
    w_ref[...] = w


def _peer_out_kernel(idx_ref, w_ref, x1_ref, nw_ref, tab_hbm, o_ref, tab, tile, sem):
    _load_table_once(tab_hbm, tab, sem)
    tok, per = w_ref.shape
    w = w_ref[...]
    trow = lax.broadcasted_iota(jnp.int32, (tok, per), 0)
    half = x1_ref.shape[1] // 2
    y_lo = jnp.zeros((tok, half), F32)
    y_hi = jnp.zeros((tok, half), F32)
    for tt in range(tok):
        v_lo, v_hi = _gather_rows(idx_ref, tab, tile, tt, per)
        wt = jnp.where(trow == tt, w, 0.0).astype(BF16)
        y_lo = y_lo + _dot(wt, v_lo)
        y_hi = y_hi + _dot(wt, v_hi)
    x2 = x1_ref[...] + jnp.concatenate([y_lo, y_hi], axis=1)
    o_ref[...] = x2 * lax.rsqrt(jnp.mean(x2 * x2, axis=-1, keepdims=True) + RMS_EPS) * nw_ref[...]


def _peer_ffn(eidx_tok, gate_tok, h2, x1, norm_final, expert_u, expert_v):
    t, d = h2.shape
    tok = PEER_TOK
    per = eidx_tok.shape[1]
    n = t // tok
    idx = (eidx_tok * PEER_ROW_SUB).reshape(n, 1, tok * per)
    tab_u = _pack_half(expert_u)
    tab_v = _pack_half(expert_v)
    idx_spec = pl.BlockSpec((1, 1, tok * per), lambda i: (i, 0, 0), memory_space=pltpu.SMEM)
    row = lambda wd: pl.BlockSpec((tok, wd), lambda i: (i, 0))
    scratch = [pltpu.VMEM(tab_u.shape, jnp.uint32),
               pltpu.VMEM((tok, PEER_ROW_SUB * PEER_TILE_STRIDE, 128), jnp.uint32),
               pltpu.SemaphoreType.DMA((1,))]
    params = pltpu.CompilerParams(dimension_semantics=("arbitrary",), vmem_limit_bytes=VMEM_LIMIT)
    w = pl.pallas_call(
        _peer_act_kernel,
        grid=(n,),
        in_specs=[idx_spec, row(d), row(per), pl.BlockSpec(memory_space=pl.ANY)],
        out_specs=row(per),
        out_shape=jax.ShapeDtypeStruct((t, per), F32),
        scratch_shapes=scratch,
        compiler_params=params,
    )(idx, h2, gate_tok, tab_u)
    return pl.pallas_call(
        _peer_out_kernel,
        grid=(n,),
        in_specs=[idx_spec, row(per), row(d), pl.BlockSpec((1, d), lambda i: (0, 0)),
                  pl.BlockSpec(memory_space=pl.ANY)],
        out_specs=row(d),
        out_shape=jax.ShapeDtypeStruct((t, d), F32),
        scratch_shapes=scratch,
        compiler_params=params,
    )(idx, w, x1, norm_final.reshape(1, d), tab_v)


def _layer(x, rel_bias, norm_mix, w_in, conv_w, a_log, dt_bias, gdn_norm, cmp_pe_k, cmp_pe_v, cmp_w_k, cmp_w_v,
           w_branch_a, w_branch_b, w_out, norm_ffn, w_query, sub_keys, expert_u, expert_v, norm_out):
    b, s, d = x.shape
    t = b * s
    xf = x.reshape(t, d)
    hk = GDN_HEADS * GDN_DK
    w_qkv = 2 * hk + GDN_HEADS * GDN_DV
    w_z = GDN_HEADS * GDN_DV
    w_nq = NSA_HEADS * NSA_DH
    w_nkv = NSA_BRANCHES * 2 * NSA_KV_GROUPS * NSA_DH
    w_gate = NSA_HEADS * NSA_BRANCHES
    w_small = 2 * GDN_HEADS + w_gate
    pad = (-w_small) % 128
    c0 = w_qkv + w_z
    c1 = c0 + 2 * GDN_HEADS
    c2 = c1 + w_nq
    c3 = c2 + w_nkv
    c4 = c3 + w_gate
    w_cat = jnp.concatenate([w_in[:, :c0], w_in[:, c1:c3], w_in[:, c4:], w_in[:, c0:c1], w_in[:, c3:c4],
                             jnp.zeros((d, pad), w_in.dtype)], axis=1).astype(BF16)
    widths = (w_qkv, w_z, w_nq, w_nkv, 2 * d, w_small + pad)
    qkv, z, nq, kvg, mg, small = _proj(xf, norm_mix, w_cat, widths, {3: NSA_DH})

    oa = _gdn(qkv, z, small, conv_w, a_log, dt_bias, gdn_norm, b, s)

    kv = kvg.reshape(NSA_BRANCHES, 2, NSA_KV_GROUPS, b, s, NSA_DH)
    kcmp, vcmp = _compress(kv[0, 0], kv[0, 1], cmp_pe_k, cmp_pe_v, cmp_w_k, cmp_w_v)
    ob = _nsa(nq, small, kcmp, vcmp, kv, rel_bias, b, s)

    x1, h2, qry = _merge(xf, oa, ob, mg, w_branch_a.astype(BF16), w_branch_b.astype(BF16), w_out.astype(BF16),
                         norm_ffn, w_query.astype(BF16))
    eidx, gate = _peer_select(qry, sub_keys)
    out = _peer_ffn(eidx.T, gate.T, h2, x1, norm_out, expert_u, expert_v)
    return out.reshape(b, s, d)


def kernel(x, rel_bias, norm_final, norm_mix, w_in, conv_w, a_log, dt_bias, gdn_norm, cmp_pe_k, cmp_pe_v, cmp_w_k,
           cmp_w_v, w_branch_a, w_branch_b, w_out, norm_ffn, w_query, sub_keys, expert_u, expert_v):
    assert norm_mix.shape[0] == 1, "single-layer block"
    return _layer(x, rel_bias, norm_mix[0], w_in[0], conv_w[0], a_log[0], dt_bias[0], gdn_norm[0], cmp_pe_k[0],
                  cmp_pe_v[0], cmp_w_k[0], cmp_w_v[0], w_branch_a[0], w_branch_b[0], w_out[0], norm_ffn[0],
                  w_query[0], sub_keys[0], expert_u[0], expert_v[0], norm_final)
```

```python
import functools
import math

import numpy as np
import jax
import jax.numpy as jnp
from jax import lax
from jax.experimental import pallas as pl
from jax.experimental.pallas import tpu as pltpu

F32 = jnp.float32
BF16 = jnp.bfloat16
HI = lax.Precision.HIGHEST

RMS_EPS = 1e-6
NEG = -1e30

GDN_HEADS = 8
GDN_DK = 64
GDN_DV = 64
GDN_CONV = 4
GDN_CHUNK = 64

NSA_HEADS = 8
NSA_KV_GROUPS = 2
NSA_HPG = NSA_HEADS // NSA_KV_GROUPS
NSA_DH = 64
NSA_BRANCHES = 3
CMP_BLOCK = 32
CMP_STRIDE = 16
SEL_BLOCK = 64
N_SELECT = 16
WINDOW = 512
FORCE_BONUS = 1e4
REL_BUCKETS = 32
REL_MAX_DIST = 128

PEER_HEADS = 8
PEER_NKEYS = 128
PEER_TOPK = 16
PEER_DQH = 128

NSA_TQ = 128
NSA_FAR_TILES = 4
PEER_TOK = 32
VMEM_LIMIT = 56 * 1024 * 1024


def _t5_thresholds():
    d = np.arange(0, 4 * REL_MAX_DIST)
    max_exact = REL_BUCKETS // 2
    dd = np.maximum(d, 1).astype(np.float64)
    large = max_exact + (np.log(dd / max_exact) / math.log(REL_MAX_DIST / max_exact)
                         * (REL_BUCKETS - max_exact)).astype(np.int32)
    large = np.minimum(large, REL_BUCKETS - 1)
    b = np.where(d < max_exact, d, large)
    return [int(np.argmax(b >= k)) for k in range(1, REL_BUCKETS)]


T5_THETA = _t5_thresholds()


def _dot(a, b, precision=None):
    return jnp.dot(a, b, preferred_element_type=F32, precision=precision)


def _dot_nt(a, b, precision=None):
    return lax.dot_general(a, b, (((1,), (1,)), ((), ())), preferred_element_type=F32, precision=precision)


def _dot_tn(a, b, precision=None):
    return lax.dot_general(a, b, (((0,), (0,)), ((), ())), preferred_element_type=F32, precision=precision)


def _dot_split(x, w_bf16):
    hi = x.astype(BF16)
    lo = (x - hi.astype(F32)).astype(BF16)
    return _dot(hi, w_bf16) + _dot(lo, w_bf16)


def _sigmoid(x):
    return 1.0 / (1.0 + jnp.exp(-x))


def _silu(x):
    return x * _sigmoid(x)


def _softplus(x):
    return jnp.maximum(x, 0.0) + jnp.log(1.0 + jnp.exp(-jnp.abs(x)))


def _proj_kernel(x_ref, nw_ref, w_ref, *out_refs, widths):
    x = x_ref[...]
    h = x * lax.rsqrt(jnp.mean(x * x, axis=-1, keepdims=True) + RMS_EPS) * nw_ref[...]
    hb = h.astype(BF16)
    off = 0
    for o_ref, wd in zip(out_refs, widths):
        res = _dot(hb, w_ref[:, off:off + wd])
        if len(o_ref.shape) == 2:
            o_ref[...] = res
        else:
            pw = o_ref.shape[2]
            for c in range(o_ref.shape[0]):
                o_ref[c] = res[:, c * pw:(c + 1) * pw]
        off += wd


def _proj(xf, norm_w, w_cat, widths, split, tm=256):
    t, d = xf.shape
    n = w_cat.shape[1]
    out_specs, out_shape = [], []
    for k, wd in enumerate(widths):
        if k in split:
            pw = split[k]
            out_specs.append(pl.BlockSpec((wd // pw, tm, pw), lambda i: (0, i, 0)))
            out_shape.append(jax.ShapeDtypeStruct((wd // pw, t, pw), F32))
        else:
            out_specs.append(pl.BlockSpec((tm, wd), lambda i: (i, 0)))
            out_shape.append(jax.ShapeDtypeStruct((t, wd), F32))
    return pl.pallas_call(
        functools.partial(_proj_kernel, widths=widths),
        grid=(t // tm,),
        in_specs=[pl.BlockSpec((tm, d), lambda i: (i, 0)),
                  pl.BlockSpec((1, d), lambda i: (0, 0)),
                  pl.BlockSpec((d, n), lambda i: (0, 0))],
        out_specs=out_specs,
        out_shape=out_shape,
        compiler_params=pltpu.CompilerParams(dimension_semantics=("arbitrary",), vmem_limit_bytes=VMEM_LIMIT),
    )(xf, norm_w.reshape(1, d), w_cat)


def _gdn_kernel(qkv_ref, z_ref, small_ref, convw_ref, alog_ref, dtb_ref, gnorm_ref, expand_ref, blk1_ref,
                o_ref, xbuf_ref, state_ref):
    c = GDN_CHUNK
    nh = GDN_HEADS
    dk = GDN_DK
    hd = nh * dk

    @pl.when(pl.program_id(1) == 0)
    def _():
        xbuf_ref[0:8, :] = jnp.zeros((8, xbuf_ref.shape[1]), F32)
        state_ref[...] = jnp.zeros(state_ref.shape, F32)

    xbuf_ref[8:8 + c, :] = qkv_ref[...]
    acc = xbuf_ref[pl.ds(8 - (GDN_CONV - 1), c), :] * convw_ref[0:1, :]
    for j in range(1, GDN_CONV):
        acc = acc + xbuf_ref[pl.ds(8 - (GDN_CONV - 1) + j, c), :] * convw_ref[j:j + 1, :]
    xbuf_ref[0:8, :] = xbuf_ref[c:c + 8, :]
    qkv = _silu(acc)

    row = lax.broadcasted_iota(jnp.int32, (c, c), 0)
    col = lax.broadcasted_iota(jnp.int32, (c, c), 1)
    lower = row >= col
    strict = row > col
    eye = (row == col).astype(F32)
    ltri = lower.astype(F32)

    expand = expand_ref[...]
    blk1 = blk1_ref[...]
    q = qkv[:, 0:hd]
    k = qkv[:, hd:2 * hd]
    v = qkv[:, 2 * hd:3 * hd]
    q = q * lax.rsqrt(_dot_split(q * q, blk1) + 1e-6) * (dk ** -0.5)
    k = k * lax.rsqrt(_dot_split(k * k, blk1) + 1e-6)
    small = small_ref[...]
    beta = _dot(_sigmoid(small[:, 0:nh]), expand, HI)
    g_c = -jnp.exp(alog_ref[...]) * _softplus(small[:, nh:2 * nh] + dtb_ref[...])
    gc_c = _dot(ltri, g_c, HI)
    gc = _dot(gc_c, expand, HI)
    eye_h = (lax.broadcasted_iota(jnp.int32, (nh, nh), 0) == lax.broadcasted_iota(jnp.int32, (nh, nh), 1))
    gc_r = _dot_nt(eye_h.astype(F32), gc_c, HI)
    eg = jnp.exp(gc)
    g_last = gc[c - 1:c, :]
    gl = jnp.exp(g_last)
    kb = k * beta
    vb = v * beta
    kbe = kb * eg
    q_dec = q * eg
    k_dec = k * jnp.exp(g_last - gc)

    def hs(x, h):
        return x[:, h * dk:(h + 1) * dk]

    def bf(x):
        return x.astype(BF16)

    heads = range(nh)
    decay = [jnp.where(lower, jnp.exp(jnp.where(lower, hs(gc, h) - gc_r[h:h + 1, :], 0.0)), 0.0) for h in heads]
    k_b = [bf(hs(k, h)) for h in heads]
    a = [jnp.where(strict, _dot_nt(bf(hs(kb, h)), k_b[h]) * decay[h], 0.0) for h in heads]
    tinv = [eye - a[h] for h in heads]
    p_b = [bf(a[h]) for h in heads]
    p_b = [bf(_dot(p_b[h], p_b[h])) for h in heads]
    for s in range(5):
        tinv = [tinv[h] + _dot(bf(tinv[h]), p_b[h]) for h in heads]
        if s < 4:
            p_b = [bf(_dot(p_b[h], p_b[h])) for h in heads]
    t_b = [bf(tinv[h]) for h in heads]
    u = [_dot(t_b[h], bf(hs(vb, h))) for h in heads]
    w = [_dot(t_b[h], bf(hs(kbe, h))) for h in heads]
    attn = [_dot_nt(bf(hs(q, h)), k_b[h]) * decay[h] for h in heads]
    st = [state_ref[h] for h in heads]
    st_b = [bf(st[h]) for h in heads]
    v_new = [u[h] - _dot(bf(w[h]), st_b[h]) for h in heads]
    vn_b = [bf(v_new[h]) for h in heads]
    o = [_dot(bf(hs(q_dec, h)), st_b[h]) + _dot(bf(attn[h]), vn_b[h]) for h in heads]
    for h in heads:
        state_ref[h] = st[h] * hs(gl, h) + _dot_tn(bf(hs(k_dec, h)), vn_b[h])
    o_all = jnp.concatenate(o, axis=1)
    ms = _dot_split(o_all * o_all, blk1) * (1.0 / GDN_DV)
    o_ref[...] = o_all * lax.rsqrt(ms + RMS_EPS) * gnorm_ref[...] * _silu(z_ref[...])


def _gdn(qkv, z, small, conv_w, a_log, dt_bias, gdn_norm, b, s):
    c = GDN_CHUNK
    nc = s // c
    t = b * s
    hh = GDN_HEADS
    hd = hh * GDN_DK
    wq = qkv.shape[1]
    lane_head = np.arange(hd) // GDN_DK
    expand = jnp.asarray((np.arange(hh)[:, None] == lane_head[None, :]).astype(np.float32))
    blk1 = jnp.asarray((lane_head[:, None] == lane_head[None, :]).astype(np.float32)).astype(BF16)
    const = lambda a: pl.BlockSpec(a.shape, lambda i, j: (0,) * a.ndim)
    args = (conv_w, a_log.reshape(1, hh), dt_bias.reshape(1, hh), jnp.tile(gdn_norm.reshape(1, GDN_DV), (1, hh)),
            expand, blk1)
    return pl.pallas_call(
        _gdn_kernel,
        grid=(b, nc),
        in_specs=[pl.BlockSpec((c, wq), lambda i, j: (i * nc + j, 0)),
                  pl.BlockSpec((c, hh * GDN_DV), lambda i, j: (i * nc + j, 0)),
                  pl.BlockSpec((c, small.shape[1]), lambda i, j: (i * nc + j, 0))] + [const(a) for a in args],
        out_specs=pl.BlockSpec((c, hh * GDN_DV), lambda i, j: (i * nc + j, 0)),
        out_shape=jax.ShapeDtypeStruct((t, hh * GDN_DV), F32),
        scratch_shapes=[pltpu.VMEM((c + 8, wq), F32), pltpu.VMEM((hh, GDN_DK, GDN_DV), F32)],
        compiler_params=pltpu.CompilerParams(dimension_semantics=("arbitrary", "arbitrary"),
                                             vmem_limit_bytes=VMEM_LIMIT),
    )(qkv, z, small, *args)


def _cmp_kernel(kc_ref, vc_ref, pek_ref, pev_ref, wk_ref, wv_ref, ko_ref, vo_ref):
    for src, pe, w, dst in ((kc_ref, pek_ref, wk_ref, ko_ref), (vc_ref, pev_ref, wv_ref, vo_ref)):
        r = src[0, 0]
        y_lo = _dot(r + pe[0], w[0], HI)
        y_hi = _dot(r + pe[1], w[1], HI)
        n = y_hi.shape[0]
        dst[0, 0] = y_lo + pltpu.roll(y_hi, n - 1, 0)


def _compress(kc, vc, pe_k, pe_v, w_k, w_v):
    g, b, s, dh = kc.shape
    half = CMP_STRIDE * dh
    nr = s // CMP_STRIDE
    kc2 = kc.reshape(g, b, nr, half)
    vc2 = vc.reshape(g, b, nr, half)
    spec_in = pl.BlockSpec((1, 1, nr, half), lambda i, j: (j, i, 0, 0))
    spec_pe = pl.BlockSpec((2, 1, half), lambda i, j: (0, 0, 0))
    spec_w = pl.BlockSpec((2, half, dh), lambda i, j: (0, 0, 0))
    spec_o = pl.BlockSpec((1, 1, nr, dh), lambda i, j: (i, j, 0, 0))
    return pl.pallas_call(
        _cmp_kernel,
        grid=(b, g),
        in_specs=[spec_in, spec_in, spec_pe, spec_pe, spec_w, spec_w],
        out_specs=[spec_o, spec_o],
        out_shape=[jax.ShapeDtypeStruct((b, g, nr, dh), F32)] * 2,
        compiler_params=pltpu.CompilerParams(dimension_semantics=("arbitrary", "arbitrary"),
                                             vmem_limit_bytes=VMEM_LIMIT),
    )(kc2, vc2, pe_k.reshape(2, 1, half), pe_v.reshape(2, 1, half),
      w_k.reshape(2, half, dh), w_v.reshape(2, half, dh))


def _bias_chain(dist, rb_ref, heads):
    accs = [jnp.full(dist.shape, rb_ref[0, hd], F32) for hd in heads]
    for k in range(1, REL_BUCKETS):
        m = dist >= T5_THETA[k - 1]
        accs = [jnp.where(m, rb_ref[k, hd], a) for a, hd in zip(accs, heads)]
    return accs


CMP_BAND_LO = (CMP_BLOCK - 1 + REL_MAX_DIST - 1) // CMP_STRIDE
CMP_BAND_HI = (NSA_TQ - 1 - (CMP_BLOCK - 1)) // CMP_STRIDE
CMP_BAND = CMP_BAND_LO + CMP_BAND_HI + 1


def _nsa_kernel(rb_ref, q_ref, small_ref, kcmp_ref, vcmp_ref, ks_ref, vs_ref, kw_ref, vw_ref, ovlt_ref, o_ref,
                btab_ref, cpatch_ref, mrun_ref, lrun_ref, acc_ref, sfar_ref, snear_ref, swin_ref, *,
                group_axis):
    tq = NSA_TQ
    dh = NSA_DH
    hpg = NSA_HPG
    scale = dh ** -0.5
    qi = pl.program_id(2)
    t0 = qi * tq
    ncmp = kcmp_ref.shape[2]
    nsel = ovlt_ref.shape[0]
    blk_per_tile = tq // SEL_BLOCK
    nwin = WINDOW // tq

    ri = lax.broadcasted_iota(jnp.int32, (tq, tq), 0)
    ci = lax.broadcasted_iota(jnp.int32, (tq, tq), 1)
    dloc = ri - ci

    gsel = pl.program_id(group_axis)
    heads = [gsel * hpg + hh for hh in range(hpg)]

    def far_bias(hh):
        return rb_ref[REL_BUCKETS - 1, heads[hh]]

    @pl.when(qi == 0)
    def _():
        for dl in range(2):
            tabs = _bias_chain(dloc + dl * tq, rb_ref, heads)
            for hh in range(hpg):
                btab_ref[hh, dl] = tabs[hh]
        tabs = _bias_chain(ri - CMP_STRIDE * (ci - CMP_BAND_LO) - (CMP_BLOCK - 1), rb_ref, heads)
        for hh in range(hpg):
            cpatch_ref[hh * tq:(hh + 1) * tq, :] = jnp.where(ci == CMP_BAND, far_bias(hh), tabs[hh] - far_bias(hh))

    q_all = q_ref[...] * scale
    qs = [q_all[:, hh * dh:(hh + 1) * dh] for hh in range(hpg)]
    qstk = jnp.concatenate(qs, axis=0).astype(BF16)
    gates_all = _sigmoid(small_ref[...])
    goff = 2 * GDN_HEADS

    def gate_col(hh, br):
        cols = [gates_all[:, goff + ((g * hpg + hh) * NSA_BRANCHES + br):goff + ((g * hpg + hh) * NSA_BRANCHES + br) + 1]
                for g in range(NSA_KV_GROUPS)]
        out = cols[0]
        for g in range(1, NSA_KV_GROUPS):
            out = jnp.where(gsel == g, cols[g], out)
        return out

    nrow = hpg * tq
    trow = t0 + lax.broadcasted_iota(jnp.int32, (nrow, ncmp), 0) % tq
    ncol = lax.broadcasted_iota(jnp.int32, (nrow, ncmp), 1)
    mask_c = trow - (ncol * CMP_STRIDE + CMP_BLOCK - 1) >= 0
    pr = lax.broadcasted_iota(jnp.int32, (tq, ncmp), 0)
    pc = lax.broadcasted_iota(jnp.int32, (tq, ncmp), 1)
    in_band = (pc == t0 // CMP_STRIDE - CMP_BAND_LO + pr) & (pr < CMP_BAND)
    place = jnp.where(in_band | (pr == CMP_BAND), 1.0, 0.0).astype(BF16)
    vcmp = vcmp_ref[0, 0].astype(BF16)
    bias = _dot_split(cpatch_ref[...], place)
    s = jnp.where(mask_c, _dot_nt(jnp.concatenate(qs, axis=0), kcmp_ref[0, 0], HI) + bias, NEG)
    mx = jnp.max(s, axis=-1, keepdims=True)
    e = jnp.where(mask_c, jnp.exp(s - mx), 0.0)
    l = jnp.sum(e, axis=-1, keepdims=True)
    p = e * (1.0 / jnp.where(l > 0.0, l, 1.0))
    o_c_all = _dot(p.astype(BF16), vcmp)
    o_c = [o_c_all[hh * tq:(hh + 1) * tq] for hh in range(hpg)]
    psum = p[0:tq]
    for hh in range(1, hpg):
        psum = psum + p[hh * tq:(hh + 1) * tq]
    imp_t = _dot_nt(ovlt_ref[...], psum, HI)

    blk_t = lax.broadcasted_iota(jnp.int32, (nsel, tq), 0)
    cur_t = (t0 + lax.broadcasted_iota(jnp.int32, (nsel, tq), 1)) // SEL_BLOCK
    valid_t = blk_t <= cur_t
    forced_t = (blk_t == 0) | (blk_t == cur_t) | (blk_t == cur_t - 1)
    work = jnp.where(valid_t, imp_t + jnp.where(forced_t, FORCE_BONUS, 0.0), -1.0)
    rank = jnp.zeros((nsel, tq), F32)
    for i in range(nsel):
        xi = work[i:i + 1, :]
        rank = rank + jnp.where(blk_t > i, jnp.where(xi >= work, 1.0, 0.0), jnp.where(xi > work, 1.0, 0.0))
    sel_t = jnp.where((rank < float(min(N_SELECT, nsel))) & valid_t, 1.0, 0.0)

    mrun_ref[...] = jnp.full(mrun_ref.shape, NEG, F32)
    lrun_ref[...] = jnp.zeros(lrun_ref.shape, F32)
    acc_ref[...] = jnp.zeros(acc_ref.shape, F32)

    def scores(k_ref, j, dls, mask_fn):
        mask = mask_fn()
        width = len(dls) * tq
        start = pl.multiple_of(j * tq, tq)
        kt = k_ref[0, 0, pl.ds(start, width), :].astype(BF16)
        s = _dot_nt(qstk, kt)
        out = []
        for hh in range(hpg):
            parts = []
            for cblk, dl in enumerate(dls):
                bias = far_bias(hh) if (dl is None or dl >= 2) else btab_ref[hh, dl]
                parts.append(s[hh * tq:(hh + 1) * tq, cblk * tq:(cblk + 1) * tq] + bias)
            sh = parts[0] if len(parts) == 1 else jnp.concatenate(parts, axis=1)
            out.append(jnp.where(mask, sh, NEG))
        return out

    def fold(x, op):
        r = x[:, 0:tq]
        for cblk in range(1, x.shape[1] // tq):
            r = op(r, x[:, cblk * tq:(cblk + 1) * tq])
        return r

    def pass1(br, k_ref, v_ref, j, dls, mask_fn, cache=None):
        for hh, s in enumerate(scores(k_ref, j, dls, mask_fn)):
            if cache is not None:
                cache[0][cache[1] + (hh,)] = s
            mrun_ref[br, hh] = jnp.maximum(mrun_ref[br, hh], fold(s, jnp.maximum))

    def pass2(br, k_ref, v_ref, j, dls, mask_fn, cache=None):
        start = pl.multiple_of(j * tq, tq)
        vt = v_ref[0, 0, pl.ds(start, len(dls) * tq), :].astype(BF16)
        ps = []
        if cache is None:
            masked = scores(k_ref, j, dls, mask_fn)
        else:
            masked = [cache[0][cache[1] + (hh,)] for hh in range(hpg)]
        for hh, s in enumerate(masked):
            m = mrun_ref[br, hh]
            p = jnp.exp(s - jnp.concatenate([m] * len(dls), axis=1))
            lrun_ref[br, hh] = lrun_ref[br, hh] + fold(p, jnp.add)
            ps.append(p.astype(BF16))
        acc_ref[br] = acc_ref[br] + _dot(jnp.concatenate(ps, axis=0), vt)

    def span_dist(dls):
        n = len(dls)
        r = lax.broadcasted_iota(jnp.int32, (tq, n * tq), 0)
        cidx = lax.broadcasted_iota(jnp.int32, (tq, n * tq), 1)
        return r - cidx + (dls[-1] + n - 1) * tq

    def sel_mask(j, dls):
        width = len(dls) * tq
        eb = lax.broadcasted_iota(jnp.int32, (nsel, width), 0)
        ek = lax.broadcasted_iota(jnp.int32, (nsel, width), 1) // SEL_BLOCK
        expand = jnp.where(eb == ek + j * blk_per_tile, 1.0, 0.0)
        mask = _dot_tn(sel_t, expand) > 0.5
        if dls[-1] == 0:
            mask = mask & (span_dist(dls) >= 0)
        return mask

    def win_mask(dls):
        dist = span_dist(dls)
        return (dist >= 0) & (dist < WINDOW)

    far_dls = [None] * NSA_FAR_TILES
    win_dls = list(range(nwin, -1, -1))

    def sweep(fn):
        nfar = jnp.maximum(qi - 1, 0)
        nspan = nfar // len(far_dls)

        def span_body(c, carry):
            fn(0, ks_ref, vs_ref, c * len(far_dls), far_dls, lambda: sel_mask(c * len(far_dls), far_dls),
               (sfar_ref, (c,)))
            return carry
        lax.fori_loop(0, nspan, span_body, 0)

        def far_body(j, carry):
            fn(0, ks_ref, vs_ref, j, [None], lambda: sel_mask(j, [None]))
            return carry
        lax.fori_loop(nspan * len(far_dls), nfar, far_body, 0)

        @pl.when(qi >= 1)
        def _():
            fn(0, ks_ref, vs_ref, qi - 1, [1, 0], lambda: sel_mask(qi - 1, [1, 0]), (snear_ref, ()))

        @pl.when(qi == 0)
        def _():
            fn(0, ks_ref, vs_ref, qi, [0], lambda: sel_mask(qi, [0]))

        @pl.when(qi >= nwin)
        def _():
            fn(1, kw_ref, vw_ref, qi - nwin, win_dls, lambda: win_mask(win_dls), (swin_ref, ()))

        for dl in range(nwin - 1, -1, -1):
            @pl.when((qi >= dl) & (qi < nwin))
            def _():
                fn(1, kw_ref, vw_ref, qi - dl, [dl], lambda dl=dl: win_mask([dl]))

    sweep(pass1)
    for br in range(2):
        for hh in range(hpg):
            mrun_ref[br, hh] = jnp.broadcast_to(jnp.max(mrun_ref[br, hh], axis=-1, keepdims=True), (tq, tq))
    sweep(pass2)

    for hh in range(hpg):
        o_b = []
        for br in range(2):
            l = jnp.sum(lrun_ref[br, hh], axis=-1, keepdims=True)
            o_b.append(acc_ref[br, hh * tq:(hh + 1) * tq, :] * (1.0 / l))
        out = gate_col(hh, 0) * o_c[hh] + gate_col(hh, 1) * o_b[0] + gate_col(hh, 2) * o_b[1]
        o_ref[:, hh * dh:(hh + 1) * dh] = out


def _overlap_matrix_t(s):
    n_rows = s // CMP_STRIDE
    n_cmp = (s - CMP_BLOCK) // CMP_STRIDE + 1
    n_sel = s // SEL_BLOCK
    cmp_start = np.arange(n_rows) * CMP_STRIDE
    cmp_end = cmp_start + CMP_BLOCK - 1
    sel_start = np.arange(n_sel) * SEL_BLOCK
    ov = np.clip(np.minimum(cmp_end[:, None] + 1, sel_start[None, :] + SEL_BLOCK)
                 - np.maximum(cmp_start[:, None], sel_start[None, :]), 0, None).astype(np.float32) / CMP_BLOCK
    ov[n_cmp:] = 0.0
    return jnp.asarray(ov.T)


def _nsa(q, small, kcmp, vcmp, kv, rel_bias, b, s):
    t = b * s
    tq = NSA_TQ
    nq = s // tq
    g = NSA_KV_GROUPS
    dh = NSA_DH
    gw = NSA_HPG * dh
    ncmp = kcmp.shape[2]
    nsel = s // SEL_BLOCK
    ovlt = _overlap_matrix_t(s)
    def seq_spec(br, kvi):
        return pl.BlockSpec((None, None, 1, 1, s, dh), lambda i, j, k: (br, kvi, j, i, 0, 0))
    cmp_spec = pl.BlockSpec((1, 1, ncmp, dh), lambda i, j, k: (i, j, 0, 0))
    return pl.pallas_call(
        functools.partial(_nsa_kernel, group_axis=1),
        grid=(b, g, nq),
        in_specs=[pl.BlockSpec(memory_space=pltpu.SMEM),
                  pl.BlockSpec((tq, gw), lambda i, j, k: (i * nq + k, j)),
                  pl.BlockSpec((tq, small.shape[1]), lambda i, j, k: (i * nq + k, 0)),
                  cmp_spec, cmp_spec, seq_spec(1, 0), seq_spec(1, 1), seq_spec(2, 0), seq_spec(2, 1),
                  pl.BlockSpec((nsel, ncmp), lambda i, j, k: (0, 0))],
        out_specs=pl.BlockSpec((tq, gw), lambda i, j, k: (i * nq + k, j)),
        out_shape=jax.ShapeDtypeStruct((t, g * gw), F32),
        scratch_shapes=[pltpu.VMEM((NSA_HPG, 2, tq, tq), F32),
                        pltpu.VMEM((NSA_HPG * tq, tq), F32),
                        pltpu.VMEM((2, NSA_HPG, tq, tq), F32),
                        pltpu.VMEM((2, NSA_HPG, tq, tq), F32),
                        pltpu.VMEM((2, NSA_HPG * tq, dh), F32),
                        pltpu.VMEM((max(nq // NSA_FAR_TILES, 1), NSA_HPG, tq, NSA_FAR_TILES * tq), F32),
                        pltpu.VMEM((NSA_HPG, tq, 2 * tq), F32),
                        pltpu.VMEM((NSA_HPG, tq, (WINDOW // tq + 1) * tq), F32)],
        compiler_params=pltpu.CompilerParams(dimension_semantics=("arbitrary", "arbitrary", "arbitrary"),
                                             vmem_limit_bytes=VMEM_LIMIT),
    )(rel_bias, q, small, kcmp, vcmp, kv, kv, kv, kv, ovlt)


def _merge_kernel(x_ref, oa_ref, ob_ref, mg_ref, wa_ref, wb_ref, wo_ref, nf_ref, wq_ref, x1_ref, h2_ref, qry_ref):
    d = x_ref.shape[1]
    ya = _dot(oa_ref[...].astype(BF16), wa_ref[...])
    yb = _dot(ob_ref[...].astype(BF16), wb_ref[...])
    mg = mg_ref[...]
    mixed = _sigmoid(mg[:, 0:d]) * ya + _sigmoid(mg[:, d:2 * d]) * yb
    x1 = x_ref[...] + _dot(mixed.astype(BF16), wo_ref[...])
    x1_ref[...] = x1
    h2 = x1 * lax.rsqrt(jnp.mean(x1 * x1, axis=-1, keepdims=True) + RMS_EPS) * nf_ref[...]
    h2_ref[...] = h2
    qry_ref[...] = _dot(h2.astype(BF16), wq_ref[...])


def _merge(xf, oa, ob, mg, w_a, w_b, w_o, norm_ffn, w_query, tm=256):
    t, d = xf.shape
    nq = w_query.shape[1]
    row = lambda w: pl.BlockSpec((tm, w), lambda i: (i, 0))
    full = lambda a: pl.BlockSpec(a.shape, lambda i: (0, 0))
    nf = norm_ffn.reshape(1, d)
    return pl.pallas_call(
        _merge_kernel,
        grid=(t // tm,),
        in_specs=[row(d), row(oa.shape[1]), row(ob.shape[1]), row(mg.shape[1]),
                  full(w_a), full(w_b), full(w_o), full(nf), full(w_query)],
        out_specs=[row(d), row(d), row(nq)],
        out_shape=[jax.ShapeDtypeStruct((t, d), F32), jax.ShapeDtypeStruct((t, d), F32),
                   jax.ShapeDtypeStruct((t, nq), F32)],
        compiler_params=pltpu.CompilerParams(dimension_semantics=("arbitrary",), vmem_limit_bytes=VMEM_LIMIT),
    )(xf, oa, ob, mg, w_a, w_b, w_o, nf, w_query)


def _top_rows(work, k, payload=None):
    n_rows = work.shape[0]
    rows = lax.broadcasted_iota(jnp.int32, work.shape, 0).astype(F32)
    vals, idxs = [], []
    for _ in range(k):
        mx = jnp.max(work, axis=0, keepdims=True)
        first = jnp.min(jnp.where(work == mx, rows, float(n_rows)), axis=0, keepdims=True)
        hit = rows == first
        vals.append(mx)
        if payload is None:
            idxs.append(first)
        else:
            idxs.append(jnp.max(jnp.where(hit, payload, -1.0), axis=0, keepdims=True))
        work = jnp.where(hit, -jnp.inf, work)
    return jnp.concatenate(vals, axis=0), jnp.concatenate(idxs, axis=0)


def _peersel_kernel(qry_ref, keys_ref, eidx_ref, gate_ref):
    kk = PEER_TOPK
    for h in range(PEER_HEADS):
        tops = []
        for p in range(2):
            c0 = (h * 2 + p) * PEER_DQH
            sc = _dot_nt(keys_ref[h, p], qry_ref[:, c0:c0 + PEER_DQH], HI)
            tops.append(_top_rows(sc, kk))
        (s1, i1), (s2, i2) = tops
        cand_p, cidx_p = [], []
        a = 0
        while a < kk:
            nb = kk // (a + 1)
            if nb == 1:
                cand_p.append(s1[a:kk, :] + s2[0:1, :])
                cidx_p.append(i1[a:kk, :] * float(PEER_NKEYS) + i2[0:1, :])
                break
            nbp = -(-nb // 8) * 8
            v = s1[a:a + 1, :] + s2[0:nbp, :]
            if nbp != nb:
                v = jnp.where(lax.broadcasted_iota(jnp.int32, v.shape, 0) < nb, v, -jnp.inf)
            cand_p.append(v)
            cidx_p.append(i1[a:a + 1, :] * float(PEER_NKEYS) + i2[0:nbp, :])
            a += 1
        cand = jnp.concatenate(cand_p, axis=0)
        cidx = jnp.concatenate(cidx_p, axis=0)
        top, eidx = _top_rows(cand, kk, payload=cidx)
        e = jnp.exp(top - top[0:1, :])
        gate = e * (1.0 / jnp.sum(e, axis=0, keepdims=True))
        eidx_ref[h * kk:(h + 1) * kk, :] = eidx.astype(jnp.int32)
        gate_ref[h * kk:(h + 1) * kk, :] = gate


def _peer_select(qry, sub_keys, tt=128):
    t, nq = qry.shape
    npair = PEER_HEADS * PEER_TOPK
    return pl.pallas_call(
        _peersel_kernel,
        grid=(t // tt,),
        in_specs=[pl.BlockSpec((tt, nq), lambda i: (i, 0)),
                  pl.BlockSpec(sub_keys.shape, lambda i: (0, 0, 0, 0))],
        out_specs=[pl.BlockSpec((npair, tt), lambda i: (0, i)), pl.BlockSpec((npair, tt), lambda i: (0, i))],
        out_shape=[jax.ShapeDtypeStruct((npair, t), jnp.int32), jax.ShapeDtypeStruct((npair, t), F32)],
        compiler_params=pltpu.CompilerParams(dimension_semantics=("arbitrary",), vmem_limit_bytes=VMEM_LIMIT),
    )(qry, sub_keys)


def _gelu_tanh(x):
    return 0.5 * x * (1.0 + jnp.tanh(math.sqrt(2.0 / math.pi) * (x + 0.044715 * (x * x * x))))


PEER_ROW_SUB = 4
PEER_TILE_STRIDE = PEER_HEADS * PEER_TOPK + 1


def _pack_half(table):
    n, d = table.shape
    bits = lax.bitcast_convert_type(table.astype(BF16), jnp.uint16).astype(jnp.uint32)
    words = bits[:, :d // 2] | (bits[:, d // 2:] << 16)
    return words.reshape(n * PEER_ROW_SUB, d // 2 // PEER_ROW_SUB)


def _load_table_once(tab_hbm, tab, sem):
    @pl.when(pl.program_id(0) == 0)
    def _():
        cp = pltpu.make_async_copy(tab_hbm, tab, sem.at[0])
        cp.start()
        cp.wait()


def _with_step_indices(idx_hbm, idx_smem, isem, body):
    i = pl.program_id(0)
    n = pl.num_programs(0)

    def copy(step, slot):
        return pltpu.make_async_copy(idx_hbm.at[step], idx_smem.at[slot], isem.at[slot])

    @pl.when(i == 0)
    def _():
        copy(0, 0).start()

    for slot in range(2):
        @pl.when(i % 2 == slot)
        def _():
            @pl.when(i + 1 < n)
            def _():
                copy(i + 1, 1 - slot).start()
            copy(i, slot).wait()
            body(slot)


def _gather_rows(idx_smem, slot, tab, tile, tt, per):
    for k in range(per):
        e4 = pl.multiple_of(idx_smem[slot, tt * per + k], PEER_ROW_SUB)
        tile[tt, pl.ds(k, PEER_ROW_SUB, stride=PEER_TILE_STRIDE), :] = tab[pl.ds(e4, PEER_ROW_SUB), :]
    xs = [tile[tt, pl.ds(j * PEER_TILE_STRIDE, per), :] for j in range(PEER_ROW_SUB)]
    lo = jnp.concatenate([lax.bitcast_convert_type(x << 16, F32) for x in xs], axis=1).astype(BF16)
    hi = jnp.concatenate([lax.bitcast_convert_type(x & jnp.uint32(0xFFFF0000), F32) for x in xs], axis=1)
    return lo, hi.astype(BF16)


def _peer_act_kernel(idx_hbm, h_ref, gate_ref, tab_hbm, w_ref, tab, tile, idx_smem, sem, isem):
    _load_table_once(tab_hbm, tab, sem)
    tok, per = gate_ref.shape
    hb = h_ref[...].astype(BF16)
    half = hb.shape[1] // 2
    gate = gate_ref[...]
    trow = lax.broadcasted_iota(jnp.int32, (tok, per), 0)

    def body(slot):
        w = jnp.zeros((tok, per), F32)
        for tt in range(tok):
            u_lo, u_hi = _gather_rows(idx_smem, slot, tab, tile, tt, per)
            act = _dot_nt(hb[:, :half], u_lo) + _dot_nt(hb[:, half:], u_hi)
            w = w + jnp.where(trow == tt, gate * _gelu_tanh(act), 0.0)
        w_ref[...] = w

    _with_step_indices(idx_hbm, idx_smem, isem, body)


def _peer_out_kernel(idx_hbm, w_ref, x1_ref, nw_ref, tab_hbm, o_ref, tab, tile, idx_smem, sem, isem):
    _load_table_once(tab_hbm, tab, sem)
    tok, per = w_ref.shape
    w = w_ref[...]
    trow = lax.broadcasted_iota(jnp.int32, (tok, per), 0)
    half = x1_ref.shape[1] // 2

    def body(slot):
        y_lo = jnp.zeros((tok, half), F32)
        y_hi = jnp.zeros((tok, half), F32)
        for tt in range(tok):
            v_lo, v_hi = _gather_rows(idx_smem, slot, tab, tile, tt, per)
            wt = jnp.where(trow == tt, w, 0.0).astype(BF16)
            y_lo = y_lo + _dot(wt, v_lo)
            y_hi = y_hi + _dot(wt, v_hi)
        x2 = x1_ref[...] + jnp.concatenate([y_lo, y_hi], axis=1)
        o_ref[...] = x2 * lax.rsqrt(jnp.mean(x2 * x2, axis=-1, keepdims=True) + RMS_EPS) * nw_ref[...]

    _with_step_indices(idx_hbm, idx_smem, isem, body)


def _peer_ffn(eidx_tok, gate_tok, h2, x1, norm_final, expert_u, expert_v):
    t, d = h2.shape
    tok = PEER_TOK
    per = eidx_tok.shape[1]
    n = t // tok
    idx = (eidx_tok * PEER_ROW_SUB).reshape(n, tok * per)
    tab_u = _pack_half(expert_u)
    tab_v = _pack_half(expert_v)
    idx_spec = pl.BlockSpec(memory_space=pl.ANY)
    row = lambda wd: pl.BlockSpec((tok, wd), lambda i: (i, 0))
    scratch = [pltpu.VMEM(tab_u.shape, jnp.uint32),
               pltpu.VMEM((tok, PEER_ROW_SUB * PEER_TILE_STRIDE, 128), jnp.uint32),
               pltpu.SMEM((2, tok * per), jnp.int32),
               pltpu.SemaphoreType.DMA((1,)),
               pltpu.SemaphoreType.DMA((2,))]
    params = pltpu.CompilerParams(dimension_semantics=("arbitrary",), vmem_limit_bytes=VMEM_LIMIT)
    w = pl.pallas_call(
        _peer_act_kernel,
        grid=(n,),
        in_specs=[idx_spec, row(d), row(per), pl.BlockSpec(memory_space=pl.ANY)],
        out_specs=row(per),
        out_shape=jax.ShapeDtypeStruct((t, per), F32),
        scratch_shapes=scratch,
        compiler_params=params,
    )(idx, h2, gate_tok, tab_u)
    return pl.pallas_call(
        _peer_out_kernel,
        grid=(n,),
        in_specs=[idx_spec, row(per), row(d), pl.BlockSpec((1, d), lambda i: (0, 0)),
                  pl.BlockSpec(memory_space=pl.ANY)],
        out_specs=row(d),
        out_shape=jax.ShapeDtypeStruct((t, d), F32),
        scratch_shapes=scratch,
        compiler_params=params,
    )(idx, w, x1, norm_final.reshape(1, d), tab_v)


def _layer(x, rel_bias, norm_mix, w_in, conv_w, a_log, dt_bias, gdn_norm, cmp_pe_k, cmp_pe_v, cmp_w_k, cmp_w_v,
           w_branch_a, w_branch_b, w_out, norm_ffn, w_query, sub_keys, expert_u, expert_v, norm_out):
    b, s, d = x.shape
    t = b * s
    xf = x.reshape(t, d)
    hk = GDN_HEADS * GDN_DK
    w_qkv = 2 * hk + GDN_HEADS * GDN_DV
    w_z = GDN_HEADS * GDN_DV
    w_nq = NSA_HEADS * NSA_DH
    w_nkv = NSA_BRANCHES * 2 * NSA_KV_GROUPS * NSA_DH
    w_gate = NSA_HEADS * NSA_BRANCHES
    w_small = 2 * GDN_HEADS + w_gate
    pad = (-w_small) % 128
    c0 = w_qkv + w_z
    c1 = c0 + 2 * GDN_HEADS
    c2 = c1 + w_nq
    c3 = c2 + w_nkv
    c4 = c3 + w_gate
    w_cat = jnp.concatenate([w_in[:, :c0], w_in[:, c1:c3], w_in[:, c4:], w_in[:, c0:c1], w_in[:, c3:c4],
                             jnp.zeros((d, pad), w_in.dtype)], axis=1).astype(BF16)
    widths = (w_qkv, w_z, w_nq, w_nkv, 2 * d, w_small + pad)
    qkv, z, nq, kvg, mg, small = _proj(xf, norm_mix, w_cat, widths, {3: NSA_DH})

    oa = _gdn(qkv, z, small, conv_w, a_log, dt_bias, gdn_norm, b, s)

    kv = kvg.reshape(NSA_BRANCHES, 2, NSA_KV_GROUPS, b, s, NSA_DH)
    kcmp, vcmp = _compress(kv[0, 0], kv[0, 1], cmp_pe_k, cmp_pe_v, cmp_w_k, cmp_w_v)
    ob = _nsa(nq, small, kcmp, vcmp, kv, rel_bias, b, s)

    x1, h2, qry = _merge(xf, oa, ob, mg, w_branch_a.astype(BF16), w_branch_b.astype(BF16), w_out.astype(BF16),
                         norm_ffn, w_query.astype(BF16))
    eidx, gate = _peer_select(qry, sub_keys)
    out = _peer_ffn(eidx.T, gate.T, h2, x1, norm_out, expert_u, expert_v)
    return out.reshape(b, s, d)


def kernel(x, rel_bias, norm_final, norm_mix, w_in, conv_w, a_log, dt_bias, gdn_norm, cmp_pe_k, cmp_pe_v, cmp_w_k,
           cmp_w_v, w_branch_a, w_branch_b, w_out, norm_ffn, w_query, sub_keys, expert_u, expert_v):
    assert norm_mix.shape[0] == 1, "single-layer block"
    return _layer(x, rel_bias, norm_mix[0], w_in[0], conv_w[0], a_log[0], dt_bias[0], gdn_norm[0], cmp_pe_k[0],
                  cmp_pe_v[0], cmp_w_k[0], cmp_w_v[0], w_branch_a[0], w_branch_b[0], w_out[0], norm_ffn[0],
                  w_query[0], sub_keys[0], expert_u[0], expert_v[0], norm_final)
```

```python
import functools
import math

import numpy as np
import jax
import jax.numpy as jnp
from jax import lax
from jax.experimental import pallas as pl
from jax.experimental.pallas import tpu as pltpu

F32 = jnp.float32
BF16 = jnp.bfloat16
HI = lax.Precision.HIGHEST

RMS_EPS = 1e-6
NEG = -1e30

GDN_HEADS = 8
GDN_DK = 64
GDN_DV = 64
GDN_CONV = 4
GDN_CHUNK = 64

NSA_HEADS = 8
NSA_KV_GROUPS = 2
NSA_HPG = NSA_HEADS // NSA_KV_GROUPS
NSA_DH = 64
NSA_BRANCHES = 3
CMP_BLOCK = 32
CMP_STRIDE = 16
SEL_BLOCK = 64
N_SELECT = 16
WINDOW = 512
FORCE_BONUS = 1e4
REL_BUCKETS = 32
REL_MAX_DIST = 128

PEER_HEADS = 8
PEER_NKEYS = 128
PEER_TOPK = 16
PEER_DQH = 128

NSA_TQ = 128
NSA_FAR_TILES = 4
PEER_TOK = 32
VMEM_LIMIT = 56 * 1024 * 1024


def _t5_thresholds():
    d = np.arange(0, 4 * REL_MAX_DIST)
    max_exact = REL_BUCKETS // 2
    dd = np.maximum(d, 1).astype(np.float64)
    large = max_exact + (np.log(dd / max_exact) / math.log(REL_MAX_DIST / max_exact)
                         * (REL_BUCKETS - max_exact)).astype(np.int32)
    large = np.minimum(large, REL_BUCKETS - 1)
    b = np.where(d < max_exact, d, large)
    return [int(np.argmax(b >= k)) for k in range(1, REL_BUCKETS)]


T5_THETA = _t5_thresholds()


def _dot(a, b, precision=None):
    return jnp.dot(a, b, preferred_element_type=F32, precision=precision)


def _dot_nt(a, b, precision=None):
    return lax.dot_general(a, b, (((1,), (1,)), ((), ())), preferred_element_type=F32, precision=precision)


def _dot_tn(a, b, precision=None):
    return lax.dot_general(a, b, (((0,), (0,)), ((), ())), preferred_element_type=F32, precision=precision)


def _dot_split(x, w_bf16):
    hi = x.astype(BF16)
    lo = (x - hi.astype(F32)).astype(BF16)
    return _dot(hi, w_bf16) + _dot(lo, w_bf16)


def _sigmoid(x):
    return 1.0 / (1.0 + jnp.exp(-x))


def _silu(x):
    return x * _sigmoid(x)


def _softplus(x):
    return jnp.maximum(x, 0.0) + jnp.log(1.0 + jnp.exp(-jnp.abs(x)))


def _proj_kernel(x_ref, nw_ref, w_ref, *out_refs, widths):
    x = x_ref[...]
    h = x * lax.rsqrt(jnp.mean(x * x, axis=-1, keepdims=True) + RMS_EPS) * nw_ref[...]
    hb = h.astype(BF16)
    off = 0
    for o_ref, wd in zip(out_refs, widths):
        res = _dot(hb, w_ref[:, off:off + wd])
        if len(o_ref.shape) == 2:
            o_ref[...] = res
        else:
            pw = o_ref.shape[2]
            for c in range(o_ref.shape[0]):
                o_ref[c] = res[:, c * pw:(c + 1) * pw]
        off += wd


def _proj(xf, norm_w, w_cat, widths, split, tm=256):
    t, d = xf.shape
    n = w_cat.shape[1]
    out_specs, out_shape = [], []
    for k, wd in enumerate(widths):
        if k in split:
            pw = split[k]
            out_specs.append(pl.BlockSpec((wd // pw, tm, pw), lambda i: (0, i, 0)))
            out_shape.append(jax.ShapeDtypeStruct((wd // pw, t, pw), F32))
        else:
            out_specs.append(pl.BlockSpec((tm, wd), lambda i: (i, 0)))
            out_shape.append(jax.ShapeDtypeStruct((t, wd), F32))
    return pl.pallas_call(
        functools.partial(_proj_kernel, widths=widths),
        grid=(t // tm,),
        in_specs=[pl.BlockSpec((tm, d), lambda i: (i, 0)),
                  pl.BlockSpec((1, d), lambda i: (0, 0)),
                  pl.BlockSpec((d, n), lambda i: (0, 0))],
        out_specs=out_specs,
        out_shape=out_shape,
        compiler_params=pltpu.CompilerParams(dimension_semantics=("arbitrary",), vmem_limit_bytes=VMEM_LIMIT),
    )(xf, norm_w.reshape(1, d), w_cat)


def _gdn_kernel(qkv_ref, z_ref, small_ref, convw_ref, alog_ref, dtb_ref, gnorm_ref, expand_ref, blk1_ref,
                o_ref, xbuf_ref, state_ref):
    c = GDN_CHUNK
    nh = GDN_HEADS
    dk = GDN_DK
    hd = nh * dk

    @pl.when(pl.program_id(1) == 0)
    def _():
        xbuf_ref[0:8, :] = jnp.zeros((8, xbuf_ref.shape[1]), F32)
        state_ref[...] = jnp.zeros(state_ref.shape, F32)

    xbuf_ref[8:8 + c, :] = qkv_ref[...]
    acc = xbuf_ref[pl.ds(8 - (GDN_CONV - 1), c), :] * convw_ref[0:1, :]
    for j in range(1, GDN_CONV):
        acc = acc + xbuf_ref[pl.ds(8 - (GDN_CONV - 1) + j, c), :] * convw_ref[j:j + 1, :]
    xbuf_ref[0:8, :] = xbuf_ref[c:c + 8, :]
    qkv = _silu(acc)

    row = lax.broadcasted_iota(jnp.int32, (c, c), 0)
    col = lax.broadcasted_iota(jnp.int32, (c, c), 1)
    lower = row >= col
    strict = row > col
    eye = (row == col).astype(F32)
    ltri = lower.astype(F32)

    expand = expand_ref[...]
    blk1 = blk1_ref[...]
    q = qkv[:, 0:hd]
    k = qkv[:, hd:2 * hd]
    v = qkv[:, 2 * hd:3 * hd]
    q = q * lax.rsqrt(_dot_split(q * q, blk1) + 1e-6) * (dk ** -0.5)
    k = k * lax.rsqrt(_dot_split(k * k, blk1) + 1e-6)
    small = small_ref[...]
    beta = _dot(_sigmoid(small[:, 0:nh]), expand, HI)
    g_c = -jnp.exp(alog_ref[...]) * _softplus(small[:, nh:2 * nh] + dtb_ref[...])
    gc_c = _dot(ltri, g_c, HI)
    gc = _dot(gc_c, expand, HI)
    eye_h = (lax.broadcasted_iota(jnp.int32, (nh, nh), 0) == lax.broadcasted_iota(jnp.int32, (nh, nh), 1))
    gc_r = _dot_nt(eye_h.astype(F32), gc_c, HI)
    eg = jnp.exp(gc)
    g_last = gc[c - 1:c, :]
    gl = jnp.exp(g_last)
    kb = k * beta
    vb = v * beta
    kbe = kb * eg
    q_dec = q * eg
    k_dec = k * jnp.exp(g_last - gc)

    def hs(x, h):
        return x[:, h * dk:(h + 1) * dk]

    def bf(x):
        return x.astype(BF16)

    heads = range(nh)
    decay = [jnp.where(lower, jnp.exp(jnp.where(lower, hs(gc, h) - gc_r[h:h + 1, :], 0.0)), 0.0) for h in heads]
    k_b = [bf(hs(k, h)) for h in heads]
    a = [jnp.where(strict, _dot_nt(bf(hs(kb, h)), k_b[h]) * decay[h], 0.0) for h in heads]
    tinv = [eye - a[h] for h in heads]
    p_b = [bf(a[h]) for h in heads]
    p_b = [bf(_dot(p_b[h], p_b[h])) for h in heads]
    for s in range(5):
        tinv = [tinv[h] + _dot(bf(tinv[h]), p_b[h]) for h in heads]
        if s < 4:
            p_b = [bf(_dot(p_b[h], p_b[h])) for h in heads]
    t_b = [bf(tinv[h]) for h in heads]
    u = [_dot(t_b[h], bf(hs(vb, h))) for h in heads]
    w = [_dot(t_b[h], bf(hs(kbe, h))) for h in heads]
    attn = [_dot_nt(bf(hs(q, h)), k_b[h]) * decay[h] for h in heads]
    st = [state_ref[h] for h in heads]
    st_b = [bf(st[h]) for h in heads]
    v_new = [u[h] - _dot(bf(w[h]), st_b[h]) for h in heads]
    vn_b = [bf(v_new[h]) for h in heads]
    o = [_dot(bf(hs(q_dec, h)), st_b[h]) + _dot(bf(attn[h]), vn_b[h]) for h in heads]
    for h in heads:
        state_ref[h] = st[h] * hs(gl, h) + _dot_tn(bf(hs(k_dec, h)), vn_b[h])
    o_all = jnp.concatenate(o, axis=1)
    ms = _dot_split(o_all * o_all, blk1) * (1.0 / GDN_DV)
    o_ref[...] = o_all * lax.rsqrt(ms + RMS_EPS) * gnorm_ref[...] * _silu(z_ref[...])


def _gdn(qkv, z, small, conv_w, a_log, dt_bias, gdn_norm, b, s):
    c = GDN_CHUNK
    nc = s // c
    t = b * s
    hh = GDN_HEADS
    hd = hh * GDN_DK
    wq = qkv.shape[1]
    lane_head = np.arange(hd) // GDN_DK
    expand = jnp.asarray((np.arange(hh)[:, None] == lane_head[None, :]).astype(np.float32))
    blk1 = jnp.asarray((lane_head[:, None] == lane_head[None, :]).astype(np.float32)).astype(BF16)
    const = lambda a: pl.BlockSpec(a.shape, lambda i, j: (0,) * a.ndim)
    args = (conv_w, a_log.reshape(1, hh), dt_bias.reshape(1, hh), jnp.tile(gdn_norm.reshape(1, GDN_DV), (1, hh)),
            expand, blk1)
    return pl.pallas_call(
        _gdn_kernel,
        grid=(b, nc),
        in_specs=[pl.BlockSpec((c, wq), lambda i, j: (i * nc + j, 0)),
                  pl.BlockSpec((c, hh * GDN_DV), lambda i, j: (i * nc + j, 0)),
                  pl.BlockSpec((c, small.shape[1]), lambda i, j: (i * nc + j, 0))] + [const(a) for a in args],
        out_specs=pl.BlockSpec((c, hh * GDN_DV), lambda i, j: (i * nc + j, 0)),
        out_shape=jax.ShapeDtypeStruct((t, hh * GDN_DV), F32),
        scratch_shapes=[pltpu.VMEM((c + 8, wq), F32), pltpu.VMEM((hh, GDN_DK, GDN_DV), F32)],
        compiler_params=pltpu.CompilerParams(dimension_semantics=("arbitrary", "arbitrary"),
                                             vmem_limit_bytes=VMEM_LIMIT),
    )(qkv, z, small, *args)


def _cmp_kernel(kc_ref, vc_ref, pek_ref, pev_ref, wk_ref, wv_ref, ko_ref, vo_ref):
    for src, pe, w, dst in ((kc_ref, pek_ref, wk_ref, ko_ref), (vc_ref, pev_ref, wv_ref, vo_ref)):
        r = src[0, 0]
        y_lo = _dot(r + pe[0], w[0], HI)
        y_hi = _dot(r + pe[1], w[1], HI)
        n = y_hi.shape[0]
        dst[0, 0] = y_lo + pltpu.roll(y_hi, n - 1, 0)


def _compress(kc, vc, pe_k, pe_v, w_k, w_v):
    g, b, s, dh = kc.shape
    half = CMP_STRIDE * dh
    nr = s // CMP_STRIDE
    kc2 = kc.reshape(g, b, nr, half)
    vc2 = vc.reshape(g, b, nr, half)
    spec_in = pl.BlockSpec((1, 1, nr, half), lambda i, j: (j, i, 0, 0))
    spec_pe = pl.BlockSpec((2, 1, half), lambda i, j: (0, 0, 0))
    spec_w = pl.BlockSpec((2, half, dh), lambda i, j: (0, 0, 0))
    spec_o = pl.BlockSpec((1, 1, nr, dh), lambda i, j: (i, j, 0, 0))
    return pl.pallas_call(
        _cmp_kernel,
        grid=(b, g),
        in_specs=[spec_in, spec_in, spec_pe, spec_pe, spec_w, spec_w],
        out_specs=[spec_o, spec_o],
        out_shape=[jax.ShapeDtypeStruct((b, g, nr, dh), F32)] * 2,
        compiler_params=pltpu.CompilerParams(dimension_semantics=("arbitrary", "arbitrary"),
                                             vmem_limit_bytes=VMEM_LIMIT),
    )(kc2, vc2, pe_k.reshape(2, 1, half), pe_v.reshape(2, 1, half),
      w_k.reshape(2, half, dh), w_v.reshape(2, half, dh))


def _bias_chain(dist, rb_ref, heads):
    accs = [jnp.full(dist.shape, rb_ref[0, hd], F32) for hd in heads]
    for k in range(1, REL_BUCKETS):
        m = dist >= T5_THETA[k - 1]
        accs = [jnp.where(m, rb_ref[k, hd], a) for a, hd in zip(accs, heads)]
    return accs


CMP_BAND_LO = (CMP_BLOCK - 1 + REL_MAX_DIST - 1) // CMP_STRIDE
CMP_BAND_HI = (NSA_TQ - 1 - (CMP_BLOCK - 1)) // CMP_STRIDE
CMP_BAND = CMP_BAND_LO + CMP_BAND_HI + 1


def _nsa_kernel(rb_ref, q_ref, small_ref, kcmp_ref, vcmp_ref, ks_ref, vs_ref, kw_ref, vw_ref, ovlt_ref, o_ref,
                btab_ref, cpatch_ref, mrun_ref, lrun_ref, acc_ref, sfar_ref, snear_ref, swin_ref, srem_ref, *,
                group_axis):
    tq = NSA_TQ
    dh = NSA_DH
    hpg = NSA_HPG
    scale = dh ** -0.5
    qi = pl.program_id(2)
    t0 = qi * tq
    ncmp = kcmp_ref.shape[2]
    nsel = ovlt_ref.shape[0]
    blk_per_tile = tq // SEL_BLOCK
    nwin = WINDOW // tq

    ri = lax.broadcasted_iota(jnp.int32, (tq, tq), 0)
    ci = lax.broadcasted_iota(jnp.int32, (tq, tq), 1)
    dloc = ri - ci

    gsel = pl.program_id(group_axis)
    heads = [gsel * hpg + hh for hh in range(hpg)]

    def far_bias(hh):
        return rb_ref[REL_BUCKETS - 1, heads[hh]]

    @pl.when(qi == 0)
    def _():
        for dl in range(2):
            tabs = _bias_chain(dloc + dl * tq, rb_ref, heads)
            for hh in range(hpg):
                btab_ref[hh, dl] = tabs[hh]
        tabs = _bias_chain(ri - CMP_STRIDE * (ci - CMP_BAND_LO) - (CMP_BLOCK - 1), rb_ref, heads)
        for hh in range(hpg):
            cpatch_ref[hh * tq:(hh + 1) * tq, :] = jnp.where(ci == CMP_BAND, far_bias(hh), tabs[hh] - far_bias(hh))

    q_all = q_ref[...] * scale
    qs = [q_all[:, hh * dh:(hh + 1) * dh] for hh in range(hpg)]
    qstk = jnp.concatenate(qs, axis=0).astype(BF16)
    gates_all = _sigmoid(small_ref[...])
    goff = 2 * GDN_HEADS

    def gate_col(hh, br):
        cols = [gates_all[:, goff + ((g * hpg + hh) * NSA_BRANCHES + br):goff + ((g * hpg + hh) * NSA_BRANCHES + br) + 1]
                for g in range(NSA_KV_GROUPS)]
        out = cols[0]
        for g in range(1, NSA_KV_GROUPS):
            out = jnp.where(gsel == g, cols[g], out)
        return out

    nrow = hpg * tq
    trow = t0 + lax.broadcasted_iota(jnp.int32, (nrow, ncmp), 0) % tq
    ncol = lax.broadcasted_iota(jnp.int32, (nrow, ncmp), 1)
    mask_c = trow - (ncol * CMP_STRIDE + CMP_BLOCK - 1) >= 0
    pr = lax.broadcasted_iota(jnp.int32, (tq, ncmp), 0)
    pc = lax.broadcasted_iota(jnp.int32, (tq, ncmp), 1)
    in_band = (pc == t0 // CMP_STRIDE - CMP_BAND_LO + pr) & (pr < CMP_BAND)
    place = jnp.where(in_band | (pr == CMP_BAND), 1.0, 0.0).astype(BF16)
    vcmp = vcmp_ref[0, 0].astype(BF16)
    bias = _dot_split(cpatch_ref[...], place)
    s = jnp.where(mask_c, _dot_nt(jnp.concatenate(qs, axis=0), kcmp_ref[0, 0], HI) + bias, NEG)
    mx = jnp.max(s, axis=-1, keepdims=True)
    e = jnp.where(mask_c, jnp.exp(s - mx), 0.0)
    l = jnp.sum(e, axis=-1, keepdims=True)
    p = e * (1.0 / jnp.where(l > 0.0, l, 1.0))
    o_c_all = _dot(p.astype(BF16), vcmp)
    o_c = [o_c_all[hh * tq:(hh + 1) * tq] for hh in range(hpg)]
    psum = p[0:tq]
    for hh in range(1, hpg):
        psum = psum + p[hh * tq:(hh + 1) * tq]
    imp_t = _dot_nt(ovlt_ref[...], psum, HI)

    blk_t = lax.broadcasted_iota(jnp.int32, (nsel, tq), 0)
    cur_t = (t0 + lax.broadcasted_iota(jnp.int32, (nsel, tq), 1)) // SEL_BLOCK
    valid_t = blk_t <= cur_t
    forced_t = (blk_t == 0) | (blk_t == cur_t) | (blk_t == cur_t - 1)
    work = jnp.where(valid_t, imp_t + jnp.where(forced_t, FORCE_BONUS, 0.0), -1.0)
    rank = jnp.zeros((nsel, tq), F32)
    for i in range(nsel):
        xi = work[i:i + 1, :]
        rank = rank + jnp.where(blk_t > i, jnp.where(xi >= work, 1.0, 0.0), jnp.where(xi > work, 1.0, 0.0))
    sel_t = jnp.where((rank < float(min(N_SELECT, nsel))) & valid_t, 1.0, 0.0)

    mrun_ref[...] = jnp.full(mrun_ref.shape, NEG, F32)
    lrun_ref[...] = jnp.zeros(lrun_ref.shape, F32)
    acc_ref[...] = jnp.zeros(acc_ref.shape, F32)

    def scores(k_ref, j, dls, mask_fn):
        mask = mask_fn()
        width = len(dls) * tq
        start = pl.multiple_of(j * tq, tq)
        kt = k_ref[0, 0, pl.ds(start, width), :].astype(BF16)
        s = _dot_nt(qstk, kt)
        out = []
        for hh in range(hpg):
            parts = []
            for cblk, dl in enumerate(dls):
                bias = far_bias(hh) if (dl is None or dl >= 2) else btab_ref[hh, dl]
                parts.append(s[hh * tq:(hh + 1) * tq, cblk * tq:(cblk + 1) * tq] + bias)
            sh = parts[0] if len(parts) == 1 else jnp.concatenate(parts, axis=1)
            out.append(jnp.where(mask, sh, NEG))
        return out

    def fold(x, op):
        r = x[:, 0:tq]
        for cblk in range(1, x.shape[1] // tq):
            r = op(r, x[:, cblk * tq:(cblk + 1) * tq])
        return r

    def pass1(br, k_ref, v_ref, j, dls, mask_fn, cache=None):
        for hh, s in enumerate(scores(k_ref, j, dls, mask_fn)):
            if cache is not None:
                cache[0][cache[1] + (hh,)] = s
            mrun_ref[br, hh] = jnp.maximum(mrun_ref[br, hh], fold(s, jnp.maximum))

    def pass2(br, k_ref, v_ref, j, dls, mask_fn, cache=None):
        start = pl.multiple_of(j * tq, tq)
        vt = v_ref[0, 0, pl.ds(start, len(dls) * tq), :].astype(BF16)
        ps = []
        if cache is None:
            masked = scores(k_ref, j, dls, mask_fn)
        else:
            masked = [cache[0][cache[1] + (hh,)] for hh in range(hpg)]
        for hh, s in enumerate(masked):
            m = mrun_ref[br, hh]
            p = jnp.exp(s - jnp.concatenate([m] * len(dls), axis=1))
            lrun_ref[br, hh] = lrun_ref[br, hh] + fold(p, jnp.add)
            ps.append(p.astype(BF16))
        acc_ref[br] = acc_ref[br] + _dot(jnp.concatenate(ps, axis=0), vt)

    def span_dist(dls):
        n = len(dls)
        r = lax.broadcasted_iota(jnp.int32, (tq, n * tq), 0)
        cidx = lax.broadcasted_iota(jnp.int32, (tq, n * tq), 1)
        return r - cidx + (dls[-1] + n - 1) * tq

    def sel_mask(j, dls):
        width = len(dls) * tq
        eb = lax.broadcasted_iota(jnp.int32, (nsel, width), 0)
        ek = lax.broadcasted_iota(jnp.int32, (nsel, width), 1) // SEL_BLOCK
        expand = jnp.where(eb == ek + j * blk_per_tile, 1.0, 0.0)
        mask = _dot_tn(sel_t, expand) > 0.5
        if dls[-1] == 0:
            mask = mask & (span_dist(dls) >= 0)
        return mask

    def win_mask(dls):
        dist = span_dist(dls)
        return (dist >= 0) & (dist < WINDOW)

    far_dls = [None] * NSA_FAR_TILES
    win_dls = list(range(nwin, -1, -1))

    def sweep(fn):
        nfar = jnp.maximum(qi - 1, 0)
        nspan = nfar // len(far_dls)

        def span_body(c, carry):
            fn(0, ks_ref, vs_ref, c * len(far_dls), far_dls, lambda: sel_mask(c * len(far_dls), far_dls),
               (sfar_ref, (c,)))
            return carry
        lax.fori_loop(0, nspan, span_body, 0)

        assert len(far_dls) == 4
        rem = nfar - nspan * len(far_dls)
        for width in (2, 1):
            first = nspan * len(far_dls) + (0 if width == 2 else (rem // 2) * 2)
            take = (rem >= 2) if width == 2 else (rem % 2 == 1)
            cache = (srem_ref, ()) if width == 2 else None

            @pl.when(take)
            def _():
                fn(0, ks_ref, vs_ref, first, [None] * width, lambda: sel_mask(first, [None] * width), cache)

        @pl.when(qi >= 1)
        def _():
            fn(0, ks_ref, vs_ref, qi - 1, [1, 0], lambda: sel_mask(qi - 1, [1, 0]), (snear_ref, ()))

        @pl.when(qi == 0)
        def _():
            fn(0, ks_ref, vs_ref, qi, [0], lambda: sel_mask(qi, [0]))

        @pl.when(qi >= nwin)
        def _():
            fn(1, kw_ref, vw_ref, qi - nwin, win_dls, lambda: win_mask(win_dls), (swin_ref, ()))

        for dl in range(nwin - 1, -1, -1):
            @pl.when((qi >= dl) & (qi < nwin))
            def _():
                fn(1, kw_ref, vw_ref, qi - dl, [dl], lambda dl=dl: win_mask([dl]))

    sweep(pass1)
    for br in range(2):
        for hh in range(hpg):
            mrun_ref[br, hh] = jnp.broadcast_to(jnp.max(mrun_ref[br, hh], axis=-1, keepdims=True), (tq, tq))
    sweep(pass2)

    for hh in range(hpg):
        o_b = []
        for br in range(2):
            l = jnp.sum(lrun_ref[br, hh], axis=-1, keepdims=True)
            o_b.append(acc_ref[br, hh * tq:(hh + 1) * tq, :] * (1.0 / l))
        out = gate_col(hh, 0) * o_c[hh] + gate_col(hh, 1) * o_b[0] + gate_col(hh, 2) * o_b[1]
        o_ref[:, hh * dh:(hh + 1) * dh] = out


def _overlap_matrix_t(s):
    n_rows = s // CMP_STRIDE
    n_cmp = (s - CMP_BLOCK) // CMP_STRIDE + 1
    n_sel = s // SEL_BLOCK
    cmp_start = np.arange(n_rows) * CMP_STRIDE
    cmp_end = cmp_start + CMP_BLOCK - 1
    sel_start = np.arange(n_sel) * SEL_BLOCK
    ov = np.clip(np.minimum(cmp_end[:, None] + 1, sel_start[None, :] + SEL_BLOCK)
                 - np.maximum(cmp_start[:, None], sel_start[None, :]), 0, None).astype(np.float32) / CMP_BLOCK
    ov[n_cmp:] = 0.0
    return jnp.asarray(ov.T)


def _nsa(q, small, kcmp, vcmp, kv, rel_bias, b, s):
    t = b * s
    tq = NSA_TQ
    nq = s // tq
    g = NSA_KV_GROUPS
    dh = NSA_DH
    gw = NSA_HPG * dh
    ncmp = kcmp.shape[2]
    nsel = s // SEL_BLOCK
    ovlt = _overlap_matrix_t(s)
    def seq_spec(br, kvi):
        return pl.BlockSpec((None, None, 1, 1, s, dh), lambda i, j, k: (br, kvi, j, i, 0, 0))
    cmp_spec = pl.BlockSpec((1, 1, ncmp, dh), lambda i, j, k: (i, j, 0, 0))
    return pl.pallas_call(
        functools.partial(_nsa_kernel, group_axis=1),
        grid=(b, g, nq),
        in_specs=[pl.BlockSpec(memory_space=pltpu.SMEM),
                  pl.BlockSpec((tq, gw), lambda i, j, k: (i * nq + k, j)),
                  pl.BlockSpec((tq, small.shape[1]), lambda i, j, k: (i * nq + k, 0)),
                  cmp_spec, cmp_spec, seq_spec(1, 0), seq_spec(1, 1), seq_spec(2, 0), seq_spec(2, 1),
                  pl.BlockSpec((nsel, ncmp), lambda i, j, k: (0, 0))],
        out_specs=pl.BlockSpec((tq, gw), lambda i, j, k: (i * nq + k, j)),
        out_shape=jax.ShapeDtypeStruct((t, g * gw), F32),
        scratch_shapes=[pltpu.VMEM((NSA_HPG, 2, tq, tq), F32),
                        pltpu.VMEM((NSA_HPG * tq, tq), F32),
                        pltpu.VMEM((2, NSA_HPG, tq, tq), F32),
                        pltpu.VMEM((2, NSA_HPG, tq, tq), F32),
                        pltpu.VMEM((2, NSA_HPG * tq, dh), F32),
                        pltpu.VMEM((max(nq // NSA_FAR_TILES, 1), NSA_HPG, tq, NSA_FAR_TILES * tq), F32),
                        pltpu.VMEM((NSA_HPG, tq, 2 * tq), F32),
                        pltpu.VMEM((NSA_HPG, tq, (WINDOW // tq + 1) * tq), F32),
                        pltpu.VMEM((NSA_HPG, tq, 2 * tq), F32)],
        compiler_params=pltpu.CompilerParams(dimension_semantics=("arbitrary", "arbitrary", "arbitrary"),
                                             vmem_limit_bytes=VMEM_LIMIT),
    )(rel_bias, q, small, kcmp, vcmp, kv, kv, kv, kv, ovlt)


def _merge_kernel(x_ref, oa_ref, ob_ref, mg_ref, wa_ref, wb_ref, wo_ref, nf_ref, wq_ref, x1_ref, h2_ref, qry_ref):
    d = x_ref.shape[1]
    ya = _dot(oa_ref[...].astype(BF16), wa_ref[...])
    yb = _dot(ob_ref[...].astype(BF16), wb_ref[...])
    mg = mg_ref[...]
    mixed = _sigmoid(mg[:, 0:d]) * ya + _sigmoid(mg[:, d:2 * d]) * yb
    x1 = x_ref[...] + _dot(mixed.astype(BF16), wo_ref[...])
    x1_ref[...] = x1
    h2 = x1 * lax.rsqrt(jnp.mean(x1 * x1, axis=-1, keepdims=True) + RMS_EPS) * nf_ref[...]
    h2_ref[...] = h2
    qry_ref[...] = _dot(h2.astype(BF16), wq_ref[...])


def _merge(xf, oa, ob, mg, w_a, w_b, w_o, norm_ffn, w_query, tm=256):
    t, d = xf.shape
    nq = w_query.shape[1]
    row = lambda w: pl.BlockSpec((tm, w), lambda i: (i, 0))
    full = lambda a: pl.BlockSpec(a.shape, lambda i: (0, 0))
    nf = norm_ffn.reshape(1, d)
    return pl.pallas_call(
        _merge_kernel,
        grid=(t // tm,),
        in_specs=[row(d), row(oa.shape[1]), row(ob.shape[1]), row(mg.shape[1]),
                  full(w_a), full(w_b), full(w_o), full(nf), full(w_query)],
        out_specs=[row(d), row(d), row(nq)],
        out_shape=[jax.ShapeDtypeStruct((t, d), F32), jax.ShapeDtypeStruct((t, d), F32),
                   jax.ShapeDtypeStruct((t, nq), F32)],
        compiler_params=pltpu.CompilerParams(dimension_semantics=("arbitrary",), vmem_limit_bytes=VMEM_LIMIT),
    )(xf, oa, ob, mg, w_a, w_b, w_o, nf, w_query)


def _top_rows(work, k, payload=None):
    n_rows = work.shape[0]
    rows = lax.broadcasted_iota(jnp.int32, work.shape, 0).astype(F32)
    vals, idxs = [], []
    for _ in range(k):
        mx = jnp.max(work, axis=0, keepdims=True)
        first = jnp.min(jnp.where(work == mx, rows, float(n_rows)), axis=0, keepdims=True)
        hit = rows == first
        vals.append(mx)
        if payload is None:
            idxs.append(first)
        else:
            idxs.append(jnp.max(jnp.where(hit, payload, -1.0), axis=0, keepdims=True))
        work = jnp.where(hit, -jnp.inf, work)
    return jnp.concatenate(vals, axis=0), jnp.concatenate(idxs, axis=0)


def _peersel_kernel(qry_ref, keys_ref, eidx_ref, gate_ref):
    kk = PEER_TOPK
    for h in range(PEER_HEADS):
        tops = []
        for p in range(2):
            c0 = (h * 2 + p) * PEER_DQH
            sc = _dot_nt(keys_ref[h, p], qry_ref[:, c0:c0 + PEER_DQH], HI)
            tops.append(_top_rows(sc, kk))
        (s1, i1), (s2, i2) = tops
        cand_p, cidx_p = [], []
        a = 0
        while a < kk:
            nb = kk // (a + 1)
            if nb == 1:
                cand_p.append(s1[a:kk, :] + s2[0:1, :])
                cidx_p.append(i1[a:kk, :] * float(PEER_NKEYS) + i2[0:1, :])
                break
            nbp = -(-nb // 8) * 8
            v = s1[a:a + 1, :] + s2[0:nbp, :]
            if nbp != nb:
                v = jnp.where(lax.broadcasted_iota(jnp.int32, v.shape, 0) < nb, v, -jnp.inf)
            cand_p.append(v)
            cidx_p.append(i1[a:a + 1, :] * float(PEER_NKEYS) + i2[0:nbp, :])
            a += 1
        cand = jnp.concatenate(cand_p, axis=0)
        cidx = jnp.concatenate(cidx_p, axis=0)
        top, eidx = _top_rows(cand, kk, payload=cidx)
        e = jnp.exp(top - top[0:1, :])
        gate = e * (1.0 / jnp.sum(e, axis=0, keepdims=True))
        eidx_ref[h * kk:(h + 1) * kk, :] = eidx.astype(jnp.int32)
        gate_ref[h * kk:(h + 1) * kk, :] = gate


def _peer_select(qry, sub_keys, tt=128):
    t, nq = qry.shape
    npair = PEER_HEADS * PEER_TOPK
    return pl.pallas_call(
        _peersel_kernel,
        grid=(t // tt,),
        in_specs=[pl.BlockSpec((tt, nq), lambda i: (i, 0)),
                  pl.BlockSpec(sub_keys.shape, lambda i: (0, 0, 0, 0))],
        out_specs=[pl.BlockSpec((npair, tt), lambda i: (0, i)), pl.BlockSpec((npair, tt), lambda i: (0, i))],
        out_shape=[jax.ShapeDtypeStruct((npair, t), jnp.int32), jax.ShapeDtypeStruct((npair, t), F32)],
        compiler_params=pltpu.CompilerParams(dimension_semantics=("arbitrary",), vmem_limit_bytes=VMEM_LIMIT),
    )(qry, sub_keys)


def _gelu_tanh(x):
    return 0.5 * x * (1.0 + jnp.tanh(math.sqrt(2.0 / math.pi) * (x + 0.044715 * (x * x * x))))


PEER_ROW_SUB = 4


def _pack_half(table):
    n, d = table.shape
    bits = lax.bitcast_convert_type(table.astype(BF16), jnp.uint16).astype(jnp.uint32)
    words = bits[:, :d // 2] | (bits[:, d // 2:] << 16)
    return words.reshape(n * PEER_ROW_SUB, d // 2 // PEER_ROW_SUB)


def _load_table_once(tab_hbm, tab, sem):
    @pl.when(pl.program_id(0) == 0)
    def _():
        cp = pltpu.make_async_copy(tab_hbm, tab, sem.at[0])
        cp.start()
        cp.wait()


def _with_step_indices(idx_hbm, idx_smem, isem, body):
    i = pl.program_id(0)
    n = pl.num_programs(0)

    def copy(step, slot):
        return pltpu.make_async_copy(idx_hbm.at[step], idx_smem.at[slot], isem.at[slot])

    @pl.when(i == 0)
    def _():
        copy(0, 0).start()

    for slot in range(2):
        @pl.when(i % 2 == slot)
        def _():
            @pl.when(i + 1 < n)
            def _():
                copy(i + 1, 1 - slot).start()
            copy(i, slot).wait()
            body(slot)


def _gather_rows(idx_smem, slot, tab, tile, tt, per):
    for k in range(per):
        e4 = pl.multiple_of(idx_smem[slot, tt * per + k], PEER_ROW_SUB)
        tile[tt, pl.ds(k * PEER_ROW_SUB, PEER_ROW_SUB), :] = tab[pl.ds(e4, PEER_ROW_SUB), :]
    xs = [tile[tt, pl.ds(j, per, stride=PEER_ROW_SUB), :] for j in range(PEER_ROW_SUB)]
    lo = jnp.concatenate([lax.bitcast_convert_type(x << 16, F32) for x in xs], axis=1).astype(BF16)
    hi = jnp.concatenate([lax.bitcast_convert_type(x & jnp.uint32(0xFFFF0000), F32) for x in xs], axis=1)
    return lo, hi.astype(BF16)


def _peer_act_kernel(idx_hbm, h_ref, gate_ref, tab_hbm, w_ref, tab, tile, idx_smem, sem, isem):
    _load_table_once(tab_hbm, tab, sem)
    tok, per = gate_ref.shape
    hb = h_ref[...].astype(BF16)
    half = hb.shape[1] // 2
    gate = gate_ref[...]
    trow = lax.broadcasted_iota(jnp.int32, (tok, per), 0)

    def body(slot):
        w = jnp.zeros((tok, per), F32)
        for tt in range(tok):
            u_lo, u_hi = _gather_rows(idx_smem, slot, tab, tile, tt, per)
            act = _dot_nt(hb[:, :half], u_lo) + _dot_nt(hb[:, half:], u_hi)
            w = w + jnp.where(trow == tt, gate * _gelu_tanh(act), 0.0)
        w_ref[...] = w

    _with_step_indices(idx_hbm, idx_smem, isem, body)


def _peer_out_kernel(idx_hbm, w_ref, x1_ref, nw_ref, tab_hbm, o_ref, tab, tile, idx_smem, sem, isem):
    _load_table_once(tab_hbm, tab, sem)
    tok, per = w_ref.shape
    w = w_ref[...]
    trow = lax.broadcasted_iota(jnp.int32, (tok, per), 0)
    half = x1_ref.shape[1] // 2

    def body(slot):
        y_lo = jnp.zeros((tok, half), F32)
        y_hi = jnp.zeros((tok, half), F32)
        for tt in range(tok):
            v_lo, v_hi = _gather_rows(idx_smem, slot, tab, tile, tt, per)
            wt = jnp.where(trow == tt, w, 0.0).astype(BF16)
            y_lo = y_lo + _dot(wt, v_lo)
            y_hi = y_hi + _dot(wt, v_hi)
        x2 = x1_ref[...] + jnp.concatenate([y_lo, y_hi], axis=1)
        o_ref[...] = x2 * lax.rsqrt(jnp.mean(x2 * x2, axis=-1, keepdims=True) + RMS_EPS) * nw_ref[...]

    _with_step_indices(idx_hbm, idx_smem, isem, body)


def _peer_ffn(eidx_tok, gate_tok, h2, x1, norm_final, expert_u, expert_v):
    t, d = h2.shape
    tok = PEER_TOK
    per = eidx_tok.shape[1]
    n = t // tok
    idx = (eidx_tok * PEER_ROW_SUB).reshape(n, tok * per)
    tab_u = _pack_half(expert_u)
    tab_v = _pack_half(expert_v)
    idx_spec = pl.BlockSpec(memory_space=pl.ANY)
    row = lambda wd: pl.BlockSpec((tok, wd), lambda i: (i, 0))
    scratch = [pltpu.VMEM(tab_u.shape, jnp.uint32),
               pltpu.VMEM((tok, PEER_ROW_SUB * per, 128), jnp.uint32),
               pltpu.SMEM((2, tok * per), jnp.int32),
               pltpu.SemaphoreType.DMA((1,)),
               pltpu.SemaphoreType.DMA((2,))]
    params = pltpu.CompilerParams(dimension_semantics=("arbitrary",), vmem_limit_bytes=VMEM_LIMIT)
    w = pl.pallas_call(
        _peer_act_kernel,
        grid=(n,),
        in_specs=[idx_spec, row(d), row(per), pl.BlockSpec(memory_space=pl.ANY)],
        out_specs=row(per),
        out_shape=jax.ShapeDtypeStruct((t, per), F32),
        scratch_shapes=scratch,
        compiler_params=params,
    )(idx, h2, gate_tok, tab_u)
    return pl.pallas_call(
        _peer_out_kernel,
        grid=(n,),
        in_specs=[idx_spec, row(per), row(d), pl.BlockSpec((1, d), lambda i: (0, 0)),
                  pl.BlockSpec(memory_space=pl.ANY)],
        out_specs=row(d),
        out_shape=jax.ShapeDtypeStruct((t, d), F32),
        scratch_shapes=scratch,
        compiler_params=params,
    )(idx, w, x1, norm_final.reshape(1, d), tab_v)


def _layer(x, rel_bias, norm_mix, w_in, conv_w, a_log, dt_bias, gdn_norm, cmp_pe_k, cmp_pe_v, cmp_w_k, cmp_w_v,
           w_branch_a, w_branch_b, w_out, norm_ffn, w_query, sub_keys, expert_u, expert_v, norm_out):
    b, s, d = x.shape
    t = b * s
    xf = x.reshape(t, d)
    hk = GDN_HEADS * GDN_DK
    w_qkv = 2 * hk + GDN_HEADS * GDN_DV
    w_z = GDN_HEADS * GDN_DV
    w_nq = NSA_HEADS * NSA_DH
    w_nkv = NSA_BRANCHES * 2 * NSA_KV_GROUPS * NSA_DH
    w_gate = NSA_HEADS * NSA_BRANCHES
    w_small = 2 * GDN_HEADS + w_gate
    pad = (-w_small) % 128
    c0 = w_qkv + w_z
    c1 = c0 + 2 * GDN_HEADS
    c2 = c1 + w_nq
    c3 = c2 + w_nkv
    c4 = c3 + w_gate
    w_cat = jnp.concatenate([w_in[:, :c0], w_in[:, c1:c3], w_in[:, c4:], w_in[:, c0:c1], w_in[:, c3:c4],
                             jnp.zeros((d, pad), w_in.dtype)], axis=1).astype(BF16)
    widths = (w_qkv, w_z, w_nq, w_nkv, 2 * d, w_small + pad)
    qkv, z, nq, kvg, mg, small = _proj(xf, norm_mix, w_cat, widths, {3: NSA_DH})

    oa = _gdn(qkv, z, small, conv_w, a_log, dt_bias, gdn_norm, b, s)

    kv = kvg.reshape(NSA_BRANCHES, 2, NSA_KV_GROUPS, b, s, NSA_DH)
    kcmp, vcmp = _compress(kv[0, 0], kv[0, 1], cmp_pe_k, cmp_pe_v, cmp_w_k, cmp_w_v)
    ob = _nsa(nq, small, kcmp, vcmp, kv, rel_bias, b, s)

    x1, h2, qry = _merge(xf, oa, ob, mg, w_branch_a.astype(BF16), w_branch_b.astype(BF16), w_out.astype(BF16),
                         norm_ffn, w_query.astype(BF16))
    eidx, gate = _peer_select(qry, sub_keys)
    out = _peer_ffn(eidx.T, gate.T, h2, x1, norm_out, expert_u, expert_v)
    return out.reshape(b, s, d)


def kernel(x, rel_bias, norm_final, norm_mix, w_in, conv_w, a_log, dt_bias, gdn_norm, cmp_pe_k, cmp_pe_v, cmp_w_k,
           cmp_w_v, w_branch_a, w_branch_b, w_out, norm_ffn, w_query, sub_keys, expert_u, expert_v):
    assert norm_mix.shape[0] == 1, "single-layer block"
    return _layer(x, rel_bias, norm_mix[0], w_in[0], conv_w[0], a_log[0], dt_bias[0], gdn_norm[0], cmp_pe_k[0],
                  cmp_pe_v[0], cmp_w_k[0], cmp_w_v[0], w_branch_a[0], w_branch_b[0], w_out[0], norm_ffn[0],
                  w_query[0], sub_keys[0], expert_u[0], expert_v[0], norm_final)
```

```python
import functools
import math

import numpy as np
import jax
import jax.numpy as jnp
from jax import lax
from jax.experimental import pallas as pl
from jax.experimental.pallas import tpu as pltpu

F32 = jnp.float32
BF16 = jnp.bfloat16
HI = lax.Precision.HIGHEST

RMS_EPS = 1e-6
NEG = -1e30

GDN_HEADS = 8
GDN_DK = 64
GDN_DV = 64
GDN_CONV = 4
GDN_CHUNK = 64

NSA_HEADS = 8
NSA_KV_GROUPS = 2
NSA_HPG = NSA_HEADS // NSA_KV_GROUPS
NSA_DH = 64
NSA_BRANCHES = 3
CMP_BLOCK = 32
CMP_STRIDE = 16
SEL_BLOCK = 64
N_SELECT = 16
WINDOW = 512
FORCE_BONUS = 1e4
REL_BUCKETS = 32
REL_MAX_DIST = 128

PEER_HEADS = 8
PEER_NKEYS = 128
PEER_TOPK = 16
PEER_DQH = 128

NSA_TQ = 128
NSA_FAR_TILES = 4
PEER_TOK = 64
VMEM_LIMIT = 56 * 1024 * 1024


def _t5_thresholds():
    d = np.arange(0, 4 * REL_MAX_DIST)
    max_exact = REL_BUCKETS // 2
    dd = np.maximum(d, 1).astype(np.float64)
    large = max_exact + (np.log(dd / max_exact) / math.log(REL_MAX_DIST / max_exact)
                         * (REL_BUCKETS - max_exact)).astype(np.int32)
    large = np.minimum(large, REL_BUCKETS - 1)
    b = np.where(d < max_exact, d, large)
    return [int(np.argmax(b >= k)) for k in range(1, REL_BUCKETS)]


T5_THETA = _t5_thresholds()


def _dot(a, b, precision=None):
    return jnp.dot(a, b, preferred_element_type=F32, precision=precision)


def _dot_nt(a, b, precision=None):
    return lax.dot_general(a, b, (((1,), (1,)), ((), ())), preferred_element_type=F32, precision=precision)


def _dot_tn(a, b, precision=None):
    return lax.dot_general(a, b, (((0,), (0,)), ((), ())), preferred_element_type=F32, precision=precision)


def _dot_split(x, w_bf16):
    hi = x.astype(BF16)
    lo = (x - hi.astype(F32)).astype(BF16)
    return _dot(hi, w_bf16) + _dot(lo, w_bf16)


def _sigmoid(x):
    return 1.0 / (1.0 + jnp.exp(-x))


def _silu(x):
    return x * _sigmoid(x)


def _softplus(x):
    return jnp.maximum(x, 0.0) + jnp.log(1.0 + jnp.exp(-jnp.abs(x)))


def _proj_kernel(x_ref, nw_ref, w_ref, *out_refs, widths):
    x = x_ref[...]
    h = x * lax.rsqrt(jnp.mean(x * x, axis=-1, keepdims=True) + RMS_EPS) * nw_ref[...]
    hb = h.astype(BF16)
    off = 0
    for o_ref, wd in zip(out_refs, widths):
        res = _dot(hb, w_ref[:, off:off + wd])
        if len(o_ref.shape) == 2:
            o_ref[...] = res
        else:
            pw = o_ref.shape[2]
            for c in range(o_ref.shape[0]):
                o_ref[c] = res[:, c * pw:(c + 1) * pw]
        off += wd


def _proj(xf, norm_w, w_cat, widths, split, tm=256):
    t, d = xf.shape
    n = w_cat.shape[1]
    out_specs, out_shape = [], []
    for k, wd in enumerate(widths):
        if k in split:
            pw = split[k]
            out_specs.append(pl.BlockSpec((wd // pw, tm, pw), lambda i: (0, i, 0)))
            out_shape.append(jax.ShapeDtypeStruct((wd // pw, t, pw), F32))
        else:
            out_specs.append(pl.BlockSpec((tm, wd), lambda i: (i, 0)))
            out_shape.append(jax.ShapeDtypeStruct((t, wd), F32))
    return pl.pallas_call(
        functools.partial(_proj_kernel, widths=widths),
        grid=(t // tm,),
        in_specs=[pl.BlockSpec((tm, d), lambda i: (i, 0)),
                  pl.BlockSpec((1, d), lambda i: (0, 0)),
                  pl.BlockSpec((d, n), lambda i: (0, 0))],
        out_specs=out_specs,
        out_shape=out_shape,
        compiler_params=pltpu.CompilerParams(dimension_semantics=("arbitrary",), vmem_limit_bytes=VMEM_LIMIT),
    )(xf, norm_w.reshape(1, d), w_cat)


def _gdn_kernel(qkv_ref, z_ref, small_ref, convw_ref, alog_ref, dtb_ref, gnorm_ref, expand_ref, blk1_ref,
                o_ref, xbuf_ref, state_ref):
    c = GDN_CHUNK
    nh = GDN_HEADS
    dk = GDN_DK
    hd = nh * dk

    @pl.when(pl.program_id(1) == 0)
    def _():
        xbuf_ref[0:8, :] = jnp.zeros((8, xbuf_ref.shape[1]), F32)
        state_ref[...] = jnp.zeros(state_ref.shape, F32)

    xbuf_ref[8:8 + c, :] = qkv_ref[...]
    acc = xbuf_ref[pl.ds(8 - (GDN_CONV - 1), c), :] * convw_ref[0:1, :]
    for j in range(1, GDN_CONV):
        acc = acc + xbuf_ref[pl.ds(8 - (GDN_CONV - 1) + j, c), :] * convw_ref[j:j + 1, :]
    xbuf_ref[0:8, :] = xbuf_ref[c:c + 8, :]
    qkv = _silu(acc)

    row = lax.broadcasted_iota(jnp.int32, (c, c), 0)
    col = lax.broadcasted_iota(jnp.int32, (c, c), 1)
    lower = row >= col
    strict = row > col
    eye = (row == col).astype(F32)
    ltri = lower.astype(F32)

    expand = expand_ref[...]
    blk1 = blk1_ref[...]
    q = qkv[:, 0:hd]
    k = qkv[:, hd:2 * hd]
    v = qkv[:, 2 * hd:3 * hd]
    q = q * lax.rsqrt(_dot_split(q * q, blk1) + 1e-6) * (dk ** -0.5)
    k = k * lax.rsqrt(_dot_split(k * k, blk1) + 1e-6)
    small = small_ref[...]
    beta = _dot(_sigmoid(small[:, 0:nh]), expand, HI)
    g_c = -jnp.exp(alog_ref[...]) * _softplus(small[:, nh:2 * nh] + dtb_ref[...])
    gc_c = _dot(ltri, g_c, HI)
    gc = _dot(gc_c, expand, HI)
    eye_h = (lax.broadcasted_iota(jnp.int32, (nh, nh), 0) == lax.broadcasted_iota(jnp.int32, (nh, nh), 1))
    gc_r = _dot_nt(eye_h.astype(F32), gc_c, HI)
    eg = jnp.exp(gc)
    g_last = gc[c - 1:c, :]
    gl = jnp.exp(g_last)
    kb = k * beta
    vb = v * beta
    kbe = kb * eg
    q_dec = q * eg
    k_dec = k * jnp.exp(g_last - gc)

    def hs(x, h):
        return x[:, h * dk:(h + 1) * dk]

    def bf(x):
        return x.astype(BF16)

    heads = range(nh)
    decay = [jnp.where(lower, jnp.exp(jnp.where(lower, hs(gc, h) - gc_r[h:h + 1, :], 0.0)), 0.0) for h in heads]
    k_b = [bf(hs(k, h)) for h in heads]
    a = [jnp.where(strict, _dot_nt(bf(hs(kb, h)), k_b[h]) * decay[h], 0.0) for h in heads]
    tinv = [eye - a[h] for h in heads]
    p_b = [bf(a[h]) for h in heads]
    p_b = [bf(_dot(p_b[h], p_b[h])) for h in heads]
    for s in range(5):
        tinv = [tinv[h] + _dot(bf(tinv[h]), p_b[h]) for h in heads]
        if s < 4:
            p_b = [bf(_dot(p_b[h], p_b[h])) for h in heads]
    t_b = [bf(tinv[h]) for h in heads]
    u = [_dot(t_b[h], bf(hs(vb, h))) for h in heads]
    w = [_dot(t_b[h], bf(hs(kbe, h))) for h in heads]
    attn = [_dot_nt(bf(hs(q, h)), k_b[h]) * decay[h] for h in heads]
    st = [state_ref[h] for h in heads]
    st_b = [bf(st[h]) for h in heads]
    v_new = [u[h] - _dot(bf(w[h]), st_b[h]) for h in heads]
    vn_b = [bf(v_new[h]) for h in heads]
    o = [_dot(bf(hs(q_dec, h)), st_b[h]) + _dot(bf(attn[h]), vn_b[h]) for h in heads]
    for h in heads:
        state_ref[h] = st[h] * hs(gl, h) + _dot_tn(bf(hs(k_dec, h)), vn_b[h])
    o_all = jnp.concatenate(o, axis=1)
    ms = _dot_split(o_all * o_all, blk1) * (1.0 / GDN_DV)
    o_ref[...] = o_all * lax.rsqrt(ms + RMS_EPS) * gnorm_ref[...] * _silu(z_ref[...])


def _gdn(qkv, z, small, conv_w, a_log, dt_bias, gdn_norm, b, s):
    c = GDN_CHUNK
    nc = s // c
    t = b * s
    hh = GDN_HEADS
    hd = hh * GDN_DK
    wq = qkv.shape[1]
    lane_head = np.arange(hd) // GDN_DK
    expand = jnp.asarray((np.arange(hh)[:, None] == lane_head[None, :]).astype(np.float32))
    blk1 = jnp.asarray((lane_head[:, None] == lane_head[None, :]).astype(np.float32)).astype(BF16)
    const = lambda a: pl.BlockSpec(a.shape, lambda i, j: (0,) * a.ndim)
    args = (conv_w, a_log.reshape(1, hh), dt_bias.reshape(1, hh), jnp.tile(gdn_norm.reshape(1, GDN_DV), (1, hh)),
            expand, blk1)
    return pl.pallas_call(
        _gdn_kernel,
        grid=(b, nc),
        in_specs=[pl.BlockSpec((c, wq), lambda i, j: (i * nc + j, 0)),
                  pl.BlockSpec((c, hh * GDN_DV), lambda i, j: (i * nc + j, 0)),
                  pl.BlockSpec((c, small.shape[1]), lambda i, j: (i * nc + j, 0))] + [const(a) for a in args],
        out_specs=pl.BlockSpec((c, hh * GDN_DV), lambda i, j: (i * nc + j, 0)),
        out_shape=jax.ShapeDtypeStruct((t, hh * GDN_DV), F32),
        scratch_shapes=[pltpu.VMEM((c + 8, wq), F32), pltpu.VMEM((hh, GDN_DK, GDN_DV), F32)],
        compiler_params=pltpu.CompilerParams(dimension_semantics=("arbitrary", "arbitrary"),
                                             vmem_limit_bytes=VMEM_LIMIT),
    )(qkv, z, small, *args)


def _cmp_kernel(kc_ref, vc_ref, pek_ref, pev_ref, wk_ref, wv_ref, ko_ref, vo_ref):
    for src, pe, w, dst in ((kc_ref, pek_ref, wk_ref, ko_ref), (vc_ref, pev_ref, wv_ref, vo_ref)):
        r = src[0, 0]
        y_lo = _dot(r + pe[0], w[0], HI)
        y_hi = _dot(r + pe[1], w[1], HI)
        n = y_hi.shape[0]
        dst[0, 0] = y_lo + pltpu.roll(y_hi, n - 1, 0)


def _compress(kc, vc, pe_k, pe_v, w_k, w_v):
    g, b, s, dh = kc.shape
    half = CMP_STRIDE * dh
    nr = s // CMP_STRIDE
    kc2 = kc.reshape(g, b, nr, half)
    vc2 = vc.reshape(g, b, nr, half)
    spec_in = pl.BlockSpec((1, 1, nr, half), lambda i, j: (j, i, 0, 0))
    spec_pe = pl.BlockSpec((2, 1, half), lambda i, j: (0, 0, 0))
    spec_w = pl.BlockSpec((2, half, dh), lambda i, j: (0, 0, 0))
    spec_o = pl.BlockSpec((1, 1, nr, dh), lambda i, j: (i, j, 0, 0))
    return pl.pallas_call(
        _cmp_kernel,
        grid=(b, g),
        in_specs=[spec_in, spec_in, spec_pe, spec_pe, spec_w, spec_w],
        out_specs=[spec_o, spec_o],
        out_shape=[jax.ShapeDtypeStruct((b, g, nr, dh), F32)] * 2,
        compiler_params=pltpu.CompilerParams(dimension_semantics=("arbitrary", "arbitrary"),
                                             vmem_limit_bytes=VMEM_LIMIT),
    )(kc2, vc2, pe_k.reshape(2, 1, half), pe_v.reshape(2, 1, half),
      w_k.reshape(2, half, dh), w_v.reshape(2, half, dh))


def _bias_chain(dist, rb_ref, heads):
    accs = [jnp.full(dist.shape, rb_ref[0, hd], F32) for hd in heads]
    for k in range(1, REL_BUCKETS):
        m = dist >= T5_THETA[k - 1]
        accs = [jnp.where(m, rb_ref[k, hd], a) for a, hd in zip(accs, heads)]
    return accs


CMP_BAND_LO = (CMP_BLOCK - 1 + REL_MAX_DIST - 1) // CMP_STRIDE
CMP_BAND_HI = (NSA_TQ - 1 - (CMP_BLOCK - 1)) // CMP_STRIDE
CMP_BAND = CMP_BAND_LO + CMP_BAND_HI + 1


def _nsa_kernel(rb_ref, q_ref, small_ref, kcmp_ref, vcmp_ref, ks_ref, vs_ref, kw_ref, vw_ref, ovlt_ref, o_ref,
                btab_ref, cpatch_ref, mrun_ref, lrun_ref, acc_ref, sfar_ref, snear_ref, swin_ref, srem_ref, *,
                group_axis):
    tq = NSA_TQ
    dh = NSA_DH
    hpg = NSA_HPG
    scale = dh ** -0.5
    qi = pl.program_id(2)
    t0 = qi * tq
    ncmp = kcmp_ref.shape[2]
    nsel = ovlt_ref.shape[0]
    blk_per_tile = tq // SEL_BLOCK
    nwin = WINDOW // tq

    ri = lax.broadcasted_iota(jnp.int32, (tq, tq), 0)
    ci = lax.broadcasted_iota(jnp.int32, (tq, tq), 1)
    dloc = ri - ci

    gsel = pl.program_id(group_axis)
    heads = [gsel * hpg + hh for hh in range(hpg)]

    def far_bias(hh):
        return rb_ref[REL_BUCKETS - 1, heads[hh]]

    @pl.when(qi == 0)
    def _():
        for dl in range(2):
            tabs = _bias_chain(dloc + dl * tq, rb_ref, heads)
            for hh in range(hpg):
                btab_ref[hh, dl] = tabs[hh]
        tabs = _bias_chain(ri - CMP_STRIDE * (ci - CMP_BAND_LO) - (CMP_BLOCK - 1), rb_ref, heads)
        for hh in range(hpg):
            cpatch_ref[hh * tq:(hh + 1) * tq, :] = jnp.where(ci == CMP_BAND, far_bias(hh), tabs[hh] - far_bias(hh))

    q_all = q_ref[...] * scale
    qs = [q_all[:, hh * dh:(hh + 1) * dh] for hh in range(hpg)]
    qstk = jnp.concatenate(qs, axis=0).astype(BF16)
    gates_all = _sigmoid(small_ref[...])
    goff = 2 * GDN_HEADS

    def gate_col(hh, br):
        cols = [gates_all[:, goff + ((g * hpg + hh) * NSA_BRANCHES + br):goff + ((g * hpg + hh) * NSA_BRANCHES + br) + 1]
                for g in range(NSA_KV_GROUPS)]
        out = cols[0]
        for g in range(1, NSA_KV_GROUPS):
            out = jnp.where(gsel == g, cols[g], out)
        return out

    nrow = hpg * tq
    trow = t0 + lax.broadcasted_iota(jnp.int32, (nrow, ncmp), 0) % tq
    ncol = lax.broadcasted_iota(jnp.int32, (nrow, ncmp), 1)
    mask_c = trow - (ncol * CMP_STRIDE + CMP_BLOCK - 1) >= 0
    pr = lax.broadcasted_iota(jnp.int32, (tq, ncmp), 0)
    pc = lax.broadcasted_iota(jnp.int32, (tq, ncmp), 1)
    in_band = (pc == t0 // CMP_STRIDE - CMP_BAND_LO + pr) & (pr < CMP_BAND)
    place = jnp.where(in_band | (pr == CMP_BAND), 1.0, 0.0).astype(BF16)
    vcmp = vcmp_ref[0, 0].astype(BF16)
    bias = _dot_split(cpatch_ref[...], place)
    s = jnp.where(mask_c, _dot_nt(jnp.concatenate(qs, axis=0), kcmp_ref[0, 0], HI) + bias, NEG)
    mx = jnp.max(s, axis=-1, keepdims=True)
    e = jnp.where(mask_c, jnp.exp(s - mx), 0.0)
    l = jnp.sum(e, axis=-1, keepdims=True)
    p = e * (1.0 / jnp.where(l > 0.0, l, 1.0))
    o_c_all = _dot(p.astype(BF16), vcmp)
    o_c = [o_c_all[hh * tq:(hh + 1) * tq] for hh in range(hpg)]
    psum = p[0:tq]
    for hh in range(1, hpg):
        psum = psum + p[hh * tq:(hh + 1) * tq]
    imp_t = _dot_nt(ovlt_ref[...], psum, HI)

    blk_t = lax.broadcasted_iota(jnp.int32, (nsel, tq), 0)
    cur_t = (t0 + lax.broadcasted_iota(jnp.int32, (nsel, tq), 1)) // SEL_BLOCK
    valid_t = blk_t <= cur_t
    forced_t = (blk_t == 0) | (blk_t == cur_t) | (blk_t == cur_t - 1)
    work = jnp.where(valid_t, imp_t + jnp.where(forced_t, FORCE_BONUS, 0.0), -1.0)
    rank = jnp.zeros((nsel, tq), F32)
    for i in range(nsel):
        xi = work[i:i + 1, :]
        rank = rank + jnp.where(blk_t > i, jnp.where(xi >= work, 1.0, 0.0), jnp.where(xi > work, 1.0, 0.0))
    sel_t = jnp.where((rank < float(min(N_SELECT, nsel))) & valid_t, 1.0, 0.0)

    mrun_ref[...] = jnp.full(mrun_ref.shape, NEG, F32)
    lrun_ref[...] = jnp.zeros(lrun_ref.shape, F32)
    acc_ref[...] = jnp.zeros(acc_ref.shape, F32)

    def scores(k_ref, j, dls, mask_fn):
        mask = mask_fn()
        width = len(dls) * tq
        start = pl.multiple_of(j * tq, tq)
        kt = k_ref[0, 0, pl.ds(start, width), :].astype(BF16)
        s = _dot_nt(qstk, kt)
        out = []
        for hh in range(hpg):
            parts = []
            for cblk, dl in enumerate(dls):
                bias = far_bias(hh) if (dl is None or dl >= 2) else btab_ref[hh, dl]
                parts.append(s[hh * tq:(hh + 1) * tq, cblk * tq:(cblk + 1) * tq] + bias)
            sh = parts[0] if len(parts) == 1 else jnp.concatenate(parts, axis=1)
            out.append(jnp.where(mask, sh, NEG))
        return out

    def fold(x, op):
        r = x[:, 0:tq]
        for cblk in range(1, x.shape[1] // tq):
            r = op(r, x[:, cblk * tq:(cblk + 1) * tq])
        return r

    def pass1(br, k_ref, v_ref, j, dls, mask_fn, cache=None):
        for hh, s in enumerate(scores(k_ref, j, dls, mask_fn)):
            if cache is not None:
                cache[0][cache[1] + (hh,)] = s
            mrun_ref[br, hh] = jnp.maximum(mrun_ref[br, hh], fold(s, jnp.maximum))

    def pass2(br, k_ref, v_ref, j, dls, mask_fn, cache=None):
        start = pl.multiple_of(j * tq, tq)
        vt = v_ref[0, 0, pl.ds(start, len(dls) * tq), :].astype(BF16)
        ps = []
        if cache is None:
            masked = scores(k_ref, j, dls, mask_fn)
        else:
            masked = [cache[0][cache[1] + (hh,)] for hh in range(hpg)]
        for hh, s in enumerate(masked):
            m = mrun_ref[br, hh]
            p = jnp.exp(s - jnp.concatenate([m] * len(dls), axis=1))
            lrun_ref[br, hh] = lrun_ref[br, hh] + fold(p, jnp.add)
            ps.append(p.astype(BF16))
        acc_ref[br] = acc_ref[br] + _dot(jnp.concatenate(ps, axis=0), vt)

    def span_dist(dls):
        n = len(dls)
        r = lax.broadcasted_iota(jnp.int32, (tq, n * tq), 0)
        cidx = lax.broadcasted_iota(jnp.int32, (tq, n * tq), 1)
        return r - cidx + (dls[-1] + n - 1) * tq

    def sel_mask(j, dls):
        width = len(dls) * tq
        eb = lax.broadcasted_iota(jnp.int32, (nsel, width), 0)
        ek = lax.broadcasted_iota(jnp.int32, (nsel, width), 1) // SEL_BLOCK
        expand = jnp.where(eb == ek + j * blk_per_tile, 1.0, 0.0)
        mask = _dot_tn(sel_t, expand) > 0.5
        if dls[-1] == 0:
            mask = mask & (span_dist(dls) >= 0)
        return mask

    def win_mask(dls):
        dist = span_dist(dls)
        return (dist >= 0) & (dist < WINDOW)

    far_dls = [None] * NSA_FAR_TILES
    win_dls = list(range(nwin, -1, -1))

    def sweep(fn):
        nfar = jnp.maximum(qi - 1, 0)
        nspan = nfar // len(far_dls)

        def span_body(c, carry):
            fn(0, ks_ref, vs_ref, c * len(far_dls), far_dls, lambda: sel_mask(c * len(far_dls), far_dls),
               (sfar_ref, (c,)))
            return carry
        lax.fori_loop(0, nspan, span_body, 0)

        assert len(far_dls) == 4
        rem = nfar - nspan * len(far_dls)
        for width in (2, 1):
            first = nspan * len(far_dls) + (0 if width == 2 else (rem // 2) * 2)
            take = (rem >= 2) if width == 2 else (rem % 2 == 1)
            cache = (srem_ref, ()) if width == 2 else None

            @pl.when(take)
            def _():
                fn(0, ks_ref, vs_ref, first, [None] * width, lambda: sel_mask(first, [None] * width), cache)

        @pl.when(qi >= 1)
        def _():
            fn(0, ks_ref, vs_ref, qi - 1, [1, 0], lambda: sel_mask(qi - 1, [1, 0]), (snear_ref, ()))

        @pl.when(qi == 0)
        def _():
            fn(0, ks_ref, vs_ref, qi, [0], lambda: sel_mask(qi, [0]))

        @pl.when(qi >= nwin)
        def _():
            fn(1, kw_ref, vw_ref, qi - nwin, win_dls, lambda: win_mask(win_dls), (swin_ref, ()))

        for dl in range(nwin - 1, -1, -1):
            @pl.when((qi >= dl) & (qi < nwin))
            def _():
                fn(1, kw_ref, vw_ref, qi - dl, [dl], lambda dl=dl: win_mask([dl]))

    sweep(pass1)
    for br in range(2):
        for hh in range(hpg):
            mrun_ref[br, hh] = jnp.broadcast_to(jnp.max(mrun_ref[br, hh], axis=-1, keepdims=True), (tq, tq))
    sweep(pass2)

    for hh in range(hpg):
        o_b = []
        for br in range(2):
            l = jnp.sum(lrun_ref[br, hh], axis=-1, keepdims=True)
            o_b.append(acc_ref[br, hh * tq:(hh + 1) * tq, :] * (1.0 / l))
        out = gate_col(hh, 0) * o_c[hh] + gate_col(hh, 1) * o_b[0] + gate_col(hh, 2) * o_b[1]
        o_ref[:, hh * dh:(hh + 1) * dh] = out


def _overlap_matrix_t(s):
    n_rows = s // CMP_STRIDE
    n_cmp = (s - CMP_BLOCK) // CMP_STRIDE + 1
    n_sel = s // SEL_BLOCK
    cmp_start = np.arange(n_rows) * CMP_STRIDE
    cmp_end = cmp_start + CMP_BLOCK - 1
    sel_start = np.arange(n_sel) * SEL_BLOCK
    ov = np.clip(np.minimum(cmp_end[:, None] + 1, sel_start[None, :] + SEL_BLOCK)
                 - np.maximum(cmp_start[:, None], sel_start[None, :]), 0, None).astype(np.float32) / CMP_BLOCK
    ov[n_cmp:] = 0.0
    return jnp.asarray(ov.T)


def _nsa(q, small, kcmp, vcmp, kv, rel_bias, b, s):
    t = b * s
    tq = NSA_TQ
    nq = s // tq
    g = NSA_KV_GROUPS
    dh = NSA_DH
    gw = NSA_HPG * dh
    ncmp = kcmp.shape[2]
    nsel = s // SEL_BLOCK
    ovlt = _overlap_matrix_t(s)
    def seq_spec(br, kvi):
        return pl.BlockSpec((None, None, 1, 1, s, dh), lambda i, j, k: (br, kvi, j, i, 0, 0))
    cmp_spec = pl.BlockSpec((1, 1, ncmp, dh), lambda i, j, k: (i, j, 0, 0))
    return pl.pallas_call(
        functools.partial(_nsa_kernel, group_axis=1),
        grid=(b, g, nq),
        in_specs=[pl.BlockSpec(memory_space=pltpu.SMEM),
                  pl.BlockSpec((tq, gw), lambda i, j, k: (i * nq + k, j)),
                  pl.BlockSpec((tq, small.shape[1]), lambda i, j, k: (i * nq + k, 0)),
                  cmp_spec, cmp_spec, seq_spec(1, 0), seq_spec(1, 1), seq_spec(2, 0), seq_spec(2, 1),
                  pl.BlockSpec((nsel, ncmp), lambda i, j, k: (0, 0))],
        out_specs=pl.BlockSpec((tq, gw), lambda i, j, k: (i * nq + k, j)),
        out_shape=jax.ShapeDtypeStruct((t, g * gw), F32),
        scratch_shapes=[pltpu.VMEM((NSA_HPG, 2, tq, tq), F32),
                        pltpu.VMEM((NSA_HPG * tq, tq), F32),
                        pltpu.VMEM((2, NSA_HPG, tq, tq), F32),
                        pltpu.VMEM((2, NSA_HPG, tq, tq), F32),
                        pltpu.VMEM((2, NSA_HPG * tq, dh), F32),
                        pltpu.VMEM((max(nq // NSA_FAR_TILES, 1), NSA_HPG, tq, NSA_FAR_TILES * tq), F32),
                        pltpu.VMEM((NSA_HPG, tq, 2 * tq), F32),
                        pltpu.VMEM((NSA_HPG, tq, (WINDOW // tq + 1) * tq), F32),
                        pltpu.VMEM((NSA_HPG, tq, 2 * tq), F32)],
        compiler_params=pltpu.CompilerParams(dimension_semantics=("arbitrary", "arbitrary", "arbitrary"),
                                             vmem_limit_bytes=VMEM_LIMIT),
    )(rel_bias, q, small, kcmp, vcmp, kv, kv, kv, kv, ovlt)


def _merge_kernel(x_ref, oa_ref, ob_ref, mg_ref, wa_ref, wb_ref, wo_ref, nf_ref, wq_ref, x1_ref, h2_ref, qry_ref):
    d = x_ref.shape[1]
    ya = _dot(oa_ref[...].astype(BF16), wa_ref[...])
    yb = _dot(ob_ref[...].astype(BF16), wb_ref[...])
    mg = mg_ref[...]
    mixed = _sigmoid(mg[:, 0:d]) * ya + _sigmoid(mg[:, d:2 * d]) * yb
    x1 = x_ref[...] + _dot(mixed.astype(BF16), wo_ref[...])
    x1_ref[...] = x1
    h2 = x1 * lax.rsqrt(jnp.mean(x1 * x1, axis=-1, keepdims=True) + RMS_EPS) * nf_ref[...]
    h2_ref[...] = h2
    qry_ref[...] = _dot(h2.astype(BF16), wq_ref[...])


def _merge(xf, oa, ob, mg, w_a, w_b, w_o, norm_ffn, w_query, tm=256):
    t, d = xf.shape
    nq = w_query.shape[1]
    row = lambda w: pl.BlockSpec((tm, w), lambda i: (i, 0))
    full = lambda a: pl.BlockSpec(a.shape, lambda i: (0, 0))
    nf = norm_ffn.reshape(1, d)
    return pl.pallas_call(
        _merge_kernel,
        grid=(t // tm,),
        in_specs=[row(d), row(oa.shape[1]), row(ob.shape[1]), row(mg.shape[1]),
                  full(w_a), full(w_b), full(w_o), full(nf), full(w_query)],
        out_specs=[row(d), row(d), row(nq)],
        out_shape=[jax.ShapeDtypeStruct((t, d), F32), jax.ShapeDtypeStruct((t, d), F32),
                   jax.ShapeDtypeStruct((t, nq), F32)],
        compiler_params=pltpu.CompilerParams(dimension_semantics=("arbitrary",), vmem_limit_bytes=VMEM_LIMIT),
    )(xf, oa, ob, mg, w_a, w_b, w_o, nf, w_query)


def _top_rows(work, k, payload=None):
    n_rows = work.shape[0]
    rows = lax.broadcasted_iota(jnp.int32, work.shape, 0).astype(F32)
    vals, idxs = [], []
    for _ in range(k):
        mx = jnp.max(work, axis=0, keepdims=True)
        first = jnp.min(jnp.where(work == mx, rows, float(n_rows)), axis=0, keepdims=True)
        hit = rows == first
        vals.append(mx)
        if payload is None:
            idxs.append(first)
        else:
            idxs.append(jnp.max(jnp.where(hit, payload, -1.0), axis=0, keepdims=True))
        work = jnp.where(hit, -jnp.inf, work)
    return jnp.concatenate(vals, axis=0), jnp.concatenate(idxs, axis=0)


def _peersel_kernel(qry_ref, keys_ref, eidx_ref, gate_ref):
    kk = PEER_TOPK
    for h in range(PEER_HEADS):
        tops = []
        for p in range(2):
            c0 = (h * 2 + p) * PEER_DQH
            sc = _dot_nt(keys_ref[h, p], qry_ref[:, c0:c0 + PEER_DQH], HI)
            tops.append(_top_rows(sc, kk))
        (s1, i1), (s2, i2) = tops
        cand_p, cidx_p = [], []
        a = 0
        while a < kk:
            nb = kk // (a + 1)
            if nb == 1:
                cand_p.append(s1[a:kk, :] + s2[0:1, :])
                cidx_p.append(i1[a:kk, :] * float(PEER_NKEYS) + i2[0:1, :])
                break
            nbp = -(-nb // 8) * 8
            v = s1[a:a + 1, :] + s2[0:nbp, :]
            if nbp != nb:
                v = jnp.where(lax.broadcasted_iota(jnp.int32, v.shape, 0) < nb, v, -jnp.inf)
            cand_p.append(v)
            cidx_p.append(i1[a:a + 1, :] * float(PEER_NKEYS) + i2[0:nbp, :])
            a += 1
        cand = jnp.concatenate(cand_p, axis=0)
        cidx = jnp.concatenate(cidx_p, axis=0)
        top, eidx = _top_rows(cand, kk, payload=cidx)
        e = jnp.exp(top - top[0:1, :])
        gate = e * (1.0 / jnp.sum(e, axis=0, keepdims=True))
        eidx_ref[h * kk:(h + 1) * kk, :] = eidx.astype(jnp.int32)
        gate_ref[h * kk:(h + 1) * kk, :] = gate


def _peer_select(qry, sub_keys, tt=128):
    t, nq = qry.shape
    npair = PEER_HEADS * PEER_TOPK
    return pl.pallas_call(
        _peersel_kernel,
        grid=(t // tt,),
        in_specs=[pl.BlockSpec((tt, nq), lambda i: (i, 0)),
                  pl.BlockSpec(sub_keys.shape, lambda i: (0, 0, 0, 0))],
        out_specs=[pl.BlockSpec((npair, tt), lambda i: (0, i)), pl.BlockSpec((npair, tt), lambda i: (0, i))],
        out_shape=[jax.ShapeDtypeStruct((npair, t), jnp.int32), jax.ShapeDtypeStruct((npair, t), F32)],
        compiler_params=pltpu.CompilerParams(dimension_semantics=("arbitrary",), vmem_limit_bytes=VMEM_LIMIT),
    )(qry, sub_keys)


def _gelu_tanh(x):
    return 0.5 * x * (1.0 + jnp.tanh(math.sqrt(2.0 / math.pi) * (x + 0.044715 * (x * x * x))))


PEER_ROW_SUB = 4


def _pack_half(table):
    n, d = table.shape
    bits = lax.bitcast_convert_type(table.astype(BF16), jnp.uint16).astype(jnp.uint32)
    words = bits[:, :d // 2] | (bits[:, d // 2:] << 16)
    return words.reshape(n * PEER_ROW_SUB, d // 2 // PEER_ROW_SUB)


def _load_table_once(tab_hbm, tab, sem):
    @pl.when(pl.program_id(0) == 0)
    def _():
        cp = pltpu.make_async_copy(tab_hbm, tab, sem.at[0])
        cp.start()
        cp.wait()


def _with_step_indices(idx_hbm, idx_smem, isem, body):
    i = pl.program_id(0)
    n = pl.num_programs(0)

    def copy(step, slot):
        return pltpu.make_async_copy(idx_hbm.at[step], idx_smem.at[slot], isem.at[slot])

    @pl.when(i == 0)
    def _():
        copy(0, 0).start()

    for slot in range(2):
        @pl.when(i % 2 == slot)
        def _():
            @pl.when(i + 1 < n)
            def _():
                copy(i + 1, 1 - slot).start()
            copy(i, slot).wait()
            body(slot)


def _gather_rows(idx_smem, slot, tab, tile, tt, per):
    for k in range(per):
        e4 = pl.multiple_of(idx_smem[slot, tt * per + k], PEER_ROW_SUB)
        tile[tt, pl.ds(k * PEER_ROW_SUB, PEER_ROW_SUB), :] = tab[pl.ds(e4, PEER_ROW_SUB), :]
    xs = [tile[tt, pl.ds(j, per, stride=PEER_ROW_SUB), :] for j in range(PEER_ROW_SUB)]
    lo = jnp.concatenate([lax.bitcast_convert_type(x << 16, F32) for x in xs], axis=1).astype(BF16)
    hi = jnp.concatenate([lax.bitcast_convert_type(x & jnp.uint32(0xFFFF0000), F32) for x in xs], axis=1)
    return lo, hi.astype(BF16)


def _peer_act_kernel(idx_hbm, h_ref, gate_ref, tab_hbm, w_ref, tab, tile, idx_smem, sem, isem):
    _load_table_once(tab_hbm, tab, sem)
    tok, per = gate_ref.shape
    hb = h_ref[...].astype(BF16)
    half = hb.shape[1] // 2
    gate = gate_ref[...]
    trow = lax.broadcasted_iota(jnp.int32, (tok, per), 0)

    def body(slot):
        w = jnp.zeros((tok, per), F32)
        for tt in range(tok):
            u_lo, u_hi = _gather_rows(idx_smem, slot, tab, tile, tt, per)
            act = _dot_nt(hb[:, :half], u_lo) + _dot_nt(hb[:, half:], u_hi)
            w = w + jnp.where(trow == tt, gate * _gelu_tanh(act), 0.0)
        w_ref[...] = w

    _with_step_indices(idx_hbm, idx_smem, isem, body)


def _peer_out_kernel(idx_hbm, w_ref, x1_ref, nw_ref, tab_hbm, o_ref, tab, tile, idx_smem, sem, isem):
    _load_table_once(tab_hbm, tab, sem)
    tok, per = w_ref.shape
    w = w_ref[...]
    trow = lax.broadcasted_iota(jnp.int32, (tok, per), 0)
    half = x1_ref.shape[1] // 2

    def body(slot):
        y_lo = jnp.zeros((tok, half), F32)
        y_hi = jnp.zeros((tok, half), F32)
        for tt in range(tok):
            v_lo, v_hi = _gather_rows(idx_smem, slot, tab, tile, tt, per)
            wt = jnp.where(trow == tt, w, 0.0).astype(BF16)
            y_lo = y_lo + _dot(wt, v_lo)
            y_hi = y_hi + _dot(wt, v_hi)
        x2 = x1_ref[...] + jnp.concatenate([y_lo, y_hi], axis=1)
        o_ref[...] = x2 * lax.rsqrt(jnp.mean(x2 * x2, axis=-1, keepdims=True) + RMS_EPS) * nw_ref[...]

    _with_step_indices(idx_hbm, idx_smem, isem, body)


def _peer_ffn(eidx_tok, gate_tok, h2, x1, norm_final, expert_u, expert_v):
    t, d = h2.shape
    tok = PEER_TOK
    per = eidx_tok.shape[1]
    n = t // tok
    idx = (eidx_tok * PEER_ROW_SUB).reshape(n, tok * per)
    tab_u = _pack_half(expert_u)
    tab_v = _pack_half(expert_v)
    idx_spec = pl.BlockSpec(memory_space=pl.ANY)
    row = lambda wd: pl.BlockSpec((tok, wd), lambda i: (i, 0))
    scratch = [pltpu.VMEM(tab_u.shape, jnp.uint32),
               pltpu.VMEM((tok, PEER_ROW_SUB * per, 128), jnp.uint32),
               pltpu.SMEM((2, tok * per), jnp.int32),
               pltpu.SemaphoreType.DMA((1,)),
               pltpu.SemaphoreType.DMA((2,))]
    params = pltpu.CompilerParams(dimension_semantics=("arbitrary",), vmem_limit_bytes=VMEM_LIMIT)
    w = pl.pallas_call(
        _peer_act_kernel,
        grid=(n,),
        in_specs=[idx_spec, row(d), row(per), pl.BlockSpec(memory_space=pl.ANY)],
        out_specs=row(per),
        out_shape=jax.ShapeDtypeStruct((t, per), F32),
        scratch_shapes=scratch,
        compiler_params=params,
    )(idx, h2, gate_tok, tab_u)
    return pl.pallas_call(
        _peer_out_kernel,
        grid=(n,),
        in_specs=[idx_spec, row(per), row(d), pl.BlockSpec((1, d), lambda i: (0, 0)),
                  pl.BlockSpec(memory_space=pl.ANY)],
        out_specs=row(d),
        out_shape=jax.ShapeDtypeStruct((t, d), F32),
        scratch_shapes=scratch,
        compiler_params=params,
    )(idx, w, x1, norm_final.reshape(1, d), tab_v)


def _layer(x, rel_bias, norm_mix, w_in, conv_w, a_log, dt_bias, gdn_norm, cmp_pe_k, cmp_pe_v, cmp_w_k, cmp_w_v,
           w_branch_a, w_branch_b, w_out, norm_ffn, w_query, sub_keys, expert_u, expert_v, norm_out):
    b, s, d = x.shape
    t = b * s
    xf = x.reshape(t, d)
    hk = GDN_HEADS * GDN_DK
    w_qkv = 2 * hk + GDN_HEADS * GDN_DV
    w_z = GDN_HEADS * GDN_DV
    w_nq = NSA_HEADS * NSA_DH
    w_nkv = NSA_BRANCHES * 2 * NSA_KV_GROUPS * NSA_DH
    w_gate = NSA_HEADS * NSA_BRANCHES
    w_small = 2 * GDN_HEADS + w_gate
    pad = (-w_small) % 128
    c0 = w_qkv + w_z
    c1 = c0 + 2 * GDN_HEADS
    c2 = c1 + w_nq
    c3 = c2 + w_nkv
    c4 = c3 + w_gate
    w_cat = jnp.concatenate([w_in[:, :c0], w_in[:, c1:c3], w_in[:, c4:], w_in[:, c0:c1], w_in[:, c3:c4],
                             jnp.zeros((d, pad), w_in.dtype)], axis=1).astype(BF16)
    widths = (w_qkv, w_z, w_nq, w_nkv, 2 * d, w_small + pad)
    qkv, z, nq, kvg, mg, small = _proj(xf, norm_mix, w_cat, widths, {3: NSA_DH})

    oa = _gdn(qkv, z, small, conv_w, a_log, dt_bias, gdn_norm, b, s)

    kv = kvg.reshape(NSA_BRANCHES, 2, NSA_KV_GROUPS, b, s, NSA_DH)
    kcmp, vcmp = _compress(kv[0, 0], kv[0, 1], cmp_pe_k, cmp_pe_v, cmp_w_k, cmp_w_v)
    ob = _nsa(nq, small, kcmp, vcmp, kv, rel_bias, b, s)

    x1, h2, qry = _merge(xf, oa, ob, mg, w_branch_a.astype(BF16), w_branch_b.astype(BF16), w_out.astype(BF16),
                         norm_ffn, w_query.astype(BF16))
    eidx, gate = _peer_select(qry, sub_keys)
    out = _peer_ffn(eidx.T, gate.T, h2, x1, norm_out, expert_u, expert_v)
    return out.reshape(b, s, d)


def kernel(x, rel_bias, norm_final, norm_mix, w_in, conv_w, a_log, dt_bias, gdn_norm, cmp_pe_k, cmp_pe_v, cmp_w_k,
           cmp_w_v, w_branch_a, w_branch_b, w_out, norm_ffn, w_query, sub_keys, expert_u, expert_v):
    assert norm_mix.shape[0] == 1, "single-layer block"
    return _layer(x, rel_bias, norm_mix[0], w_in[0], conv_w[0], a_log[0], dt_bias[0], gdn_norm[0], cmp_pe_k[0],
                  cmp_pe_v[0], cmp_w_k[0], cmp_w_v[0], w_branch_a[0], w_branch_b[0], w_out[0], norm_ffn[0],
                  w_query[0], sub_keys[0], expert_u[0], expert_v[0], norm_final)
```
